```python
import jax, jax.numpy as jnp
from jax import lax
import numpy as np

D_MODEL = 1024
BATCH = 8
SEQ = 2048
DEPTH = 2
DEC_BATCH = 128
DEC_SEQ = 1
PAST_LEN = 16384
PAGE_SIZE = 128

D_A = D_MODEL
N_GROUPS_A = 4
CHUNK = 128
D_B = D_MODEL
CONV_W = 3
D_C = D_MODEL
POOL_WINDOWS = (2, 4, 8, 16)
N_GROUPS_C = len(POOL_WINDOWS)
G_C = D_C // N_GROUPS_C
POOL_BUF = max(POOL_WINDOWS) - 1
D_FF = 4 * D_MODEL
N_BRANCH = 3
ALPHA = float((2 * DEPTH) ** 0.25)
BETA = float((8 * DEPTH) ** -0.25)
LN_EPS = 1e-5
IN_SPLITS = (D_A, 2 * D_A, 2 * D_A + D_B, 2 * D_A + 2 * D_B, 2 * D_A + 3 * D_B, 2 * D_A + 3 * D_B + D_C)
D_IN = 2 * D_A + 3 * D_B + D_C + N_BRANCH * D_MODEL

kernel_name = "hybrid_gmlp_conv_pool_decoder_step"


def layer_norm(x, g, b):
    xf = x.astype(jnp.float32)
    mu = jnp.mean(xf, axis=-1, keepdims=True)
    var = jnp.mean(jnp.square(xf - mu), axis=-1, keepdims=True)
    y = (xf - mu) * lax.rsqrt(var + LN_EPS) * g.astype(jnp.float32) + b.astype(jnp.float32)
    return y.astype(x.dtype)


def chunk_mixer(u, v, lnv_g, lnv_b, w_s, b_s):
    bn, t, _ = u.shape
    vn = layer_norm(v, lnv_g, lnv_b)
    n_chunks = -(-t // CHUNK)
    pad = n_chunks * CHUNK - t
    vp = jnp.pad(vn, ((0, 0), (0, pad), (0, 0)))
    vc = vp.reshape(bn, n_chunks, CHUNK, N_GROUPS_A, D_A // N_GROUPS_A)
    mask = jnp.tril(jnp.ones((CHUNK, CHUNK), dtype=bool))
    ws = jnp.where(mask[None], w_s, 0.0).astype(v.dtype)
    s = jnp.einsum('gts,bnsgc->bntgc', ws, vc) + b_s.T[None, None, :, :, None]
    s = s.reshape(bn, n_chunks * CHUNK, D_A)[:, :t]
    return u * s, vn


def conv_mixer(bg, cg, xb, buf, conv_w, conv_b):
    t = xb.shape[1]
    z = cg * xb
    zp = jnp.concatenate([buf.astype(z.dtype), z], axis=1)
    y = conv_b + sum(conv_w[k] * zp[:, k:k + t] for k in range(CONV_W))
    return bg * y, zp[:, -(CONV_W - 1):]


def pool_mixer(p, buf, pos0, w_pool, pool_scale):
    bn, t, _ = p.shape
    pp = jnp.concatenate([buf.astype(p.dtype), p], axis=1)
    cs = jnp.pad(jnp.cumsum(pp.astype(jnp.float32), axis=1), ((0, 0), (1, 0), (0, 0)))
    hi = cs[:, POOL_BUF + 1:POOL_BUF + 1 + t]
    pos = pos0 + jnp.arange(t)
    means = []
    for g, w in enumerate(POOL_WINDOWS):
        sl = slice(g * G_C, (g + 1) * G_C)
        lo = cs[:, POOL_BUF + 1 - w:POOL_BUF + 1 - w + t, sl]
        cnt = jnp.minimum(pos + 1, w).astype(jnp.float32)[None, :, None]
        means.append((hi[..., sl] - lo) / cnt)
    mean = jnp.concatenate(means, axis=-1).astype(p.dtype)
    d = (mean - p).reshape(bn, t, N_GROUPS_C, G_C)
    y = jnp.einsum('btgc,gcd->btgd', d, w_pool).reshape(bn, t, D_C) * pool_scale
    return y, pp[:, -POOL_BUF:]


def trunk_layer(x, buf_conv, buf_pool, pos0, w_in, lnv_g, lnv_b, w_spatial, b_spatial, w_proj_a,
                conv_w, conv_b, w_proj_b, w_pool, pool_scale, w_proj_c, w_o,
                ln1_g, ln1_b, w_ff1, w_ff2, ln2_g, ln2_b):
    bn, t, _ = x.shape
    proj = x @ w_in
    u, v, bg, cg, xb, xc, gates = jnp.split(proj, IN_SPLITS, axis=-1)
    ha, vn = chunk_mixer(u, v, lnv_g, lnv_b, w_spatial, b_spatial)
    hb, new_conv = conv_mixer(bg, cg, xb, buf_conv, conv_w, conv_b)
    hc, new_pool = pool_mixer(xc, buf_pool, pos0, w_pool, pool_scale)
    gt = jax.nn.sigmoid(gates.astype(jnp.float32)).astype(x.dtype).reshape(bn, t, N_BRANCH, D_MODEL)
    merged = gt[:, :, 0] * (ha @ w_proj_a) + gt[:, :, 1] * (hb @ w_proj_b) + gt[:, :, 2] * (hc @ w_proj_c)
    h = layer_norm(ALPHA * x + merged @ w_o, ln1_g, ln1_b)
    f = jnp.square(jax.nn.relu(h @ w_ff1)) @ w_ff2
    out = layer_norm(ALPHA * h + f, ln2_g, ln2_b)
    return out, new_conv, new_pool, vn


def setup_inputs(seed: int = 0) -> dict:
    key = jax.random.key(seed)
    ks = jax.random.split(key, 24)
    f32 = jnp.float32
    nrm = lambda k, shape, s: (jax.random.normal(k, shape, f32) * s).astype(f32)
    return {
        'x_prompt': nrm(ks[0], (BATCH, SEQ, D_MODEL), 1.0),
        'x_sample': nrm(ks[1], (DEC_BATCH, DEC_SEQ, D_MODEL), 1.0),
        'state_conv': nrm(ks[2], (DEPTH, DEC_BATCH, CONV_W - 1, D_B), 1.0),
        'state_pool': nrm(ks[3], (DEPTH, DEC_BATCH, POOL_BUF, D_C), 1.0),
        'w_in': nrm(ks[4], (DEPTH, D_MODEL, D_IN), D_MODEL ** -0.5),
        'lnv_g': 1.0 + nrm(ks[5], (DEPTH, D_A), 0.02),
        'lnv_b': nrm(ks[6], (DEPTH, D_A), 0.02),
        'w_spatial': nrm(ks[7], (DEPTH, N_GROUPS_A, CHUNK, CHUNK), CHUNK ** -0.5),
        'b_spatial': 1.0 + nrm(ks[8], (DEPTH, N_GROUPS_A, CHUNK), 0.02),
        'w_proj_a': nrm(ks[9], (DEPTH, D_A, D_MODEL), BETA * D_A ** -0.5),
        'conv_w': nrm(ks[10], (DEPTH, CONV_W, D_B), CONV_W ** -0.5),
        'conv_b': nrm(ks[11], (DEPTH, D_B), 0.02),
        'w_proj_b': nrm(ks[12], (DEPTH, D_B, D_MODEL), BETA * D_B ** -0.5),
        'w_pool': nrm(ks[13], (DEPTH, N_GROUPS_C, G_C, G_C), G_C ** -0.5),
        'pool_scale': 0.5 + nrm(ks[14], (DEPTH, D_C), 0.1),
        'w_proj_c': nrm(ks[15], (DEPTH, D_C, D_MODEL), BETA * D_C ** -0.5),
        'w_o': nrm(ks[16], (DEPTH, D_MODEL, D_MODEL), BETA * D_MODEL ** -0.5),
        'ln1_g': 1.0 + nrm(ks[17], (DEPTH, D_MODEL), 0.02),
        'ln1_b': nrm(ks[18], (DEPTH, D_MODEL), 0.02),
        'w_ff1': nrm(ks[19], (DEPTH, D_MODEL, D_FF), D_MODEL ** -0.5),
        'w_ff2': nrm(ks[20], (DEPTH, D_FF, D_MODEL), BETA * D_FF ** -0.5),
        'ln2_g': 1.0 + nrm(ks[21], (DEPTH, D_MODEL), 0.02),
        'ln2_b': nrm(ks[22], (DEPTH, D_MODEL), 0.02),
    }


def reference(x_prompt, x_sample, state_conv, state_pool, w_in, lnv_g, lnv_b, w_spatial, b_spatial,
              w_proj_a, conv_w, conv_b, w_proj_b, w_pool, pool_scale, w_proj_c, w_o,
              ln1_g, ln1_b, w_ff1, w_ff2, ln2_g, ln2_b):
    bp = x_prompt.shape[0]
    xp = x_prompt
    xs = x_sample
    conv_p, pool_p, conv_s, pool_s, chunk_v_s = [], [], [], [], []
    for l in range(DEPTH):
        params = (w_in[l], lnv_g[l], lnv_b[l], w_spatial[l], b_spatial[l], w_proj_a[l],
                  conv_w[l], conv_b[l], w_proj_b[l], w_pool[l], pool_scale[l], w_proj_c[l], w_o[l],
                  ln1_g[l], ln1_b[l], w_ff1[l], w_ff2[l], ln2_g[l], ln2_b[l])
        zero_conv = jnp.zeros((bp, CONV_W - 1, D_B), xp.dtype)
        zero_pool = jnp.zeros((bp, POOL_BUF, D_C), xp.dtype)
        xp, nc_p, np_p, _ = trunk_layer(xp, zero_conv, zero_pool, 0, *params)
        xs, nc_s, np_s, vn_s = trunk_layer(xs, state_conv[l], state_pool[l], PAST_LEN, *params)
        conv_p.append(nc_p)
        pool_p.append(np_p)
        conv_s.append(nc_s)
        pool_s.append(np_s)
        chunk_v_s.append(vn_s)
    new_conv_prompt = jnp.stack(conv_p)
    new_pool_prompt = jnp.stack(pool_p)
    new_conv_sample = jnp.stack(conv_s)
    new_pool_sample = jnp.stack(pool_s)
    chunk_v_sample = jnp.stack(chunk_v_s)
    return (xp, xs, new_conv_prompt, new_pool_prompt, new_conv_sample, new_pool_sample, chunk_v_sample)
```

```python
import functools

import jax
import jax.numpy as jnp
from jax import lax
from jax.experimental import pallas as pl
from jax.experimental.pallas import tpu as pltpu

D = 1024
N_GROUPS_A = 4
CHUNK = 128
G_A = D // N_GROUPS_A
CONV_W = 3
POOL_WINDOWS = (2, 4, 8, 16)
G_C = D // len(POOL_WINDOWS)
POOL_BUF = max(POOL_WINDOWS) - 1
D_FF = 4 * D
DEPTH = 2
ALPHA = float((2 * DEPTH) ** 0.25)
LN_EPS = 1e-5
PAST_LEN = 16384

COL_U, COL_V, COL_BG, COL_CG, COL_XB, COL_XC, COL_GATE = 0, 1, 2, 3, 4, 5, 6

SUBLANES = 8
Z_HEAD = SUBLANES
P_HEAD = 32
P_LO = 16

VMEM_LIMIT_MIXER = 56 * 1024 * 1024
VMEM_LIMIT_FFN = 48 * 1024 * 1024

F32 = jnp.float32
BF16 = jnp.bfloat16


def _dot(a, b):
    return jnp.dot(a, b, preferred_element_type=F32)


def _layer_norm(x, g, b):
    mu = jnp.mean(x, axis=-1, keepdims=True)
    xc = x - mu
    var = jnp.mean(xc * xc, axis=-1, keepdims=True)
    return xc * lax.rsqrt(var + LN_EPS) * g + b


def _mixer_prompt_kernel(x_ref, win_ref, lnvg_ref, lnvb_ref, wsp_ref, bsp_ref, wpa_ref, cw_ref, cb_ref,
                         wpb_ref, wpool_ref, psc_ref, wpc_ref, wo_ref, l1g_ref, l1b_ref,
                         h_ref, nconv_ref, npool_ref,
                         xb_s, vn_s, u_s, ha_s, hb_s, hc_s, acc_s, z_s, p_s, la_s, lb_s, *, tm):
    t = pl.program_id(1)

    @pl.when(t == 0)
    def _():
        z_s[0:Z_HEAD, :] = jnp.zeros((Z_HEAD, D), F32)
        p_s[0:P_HEAD, :] = jnp.zeros((P_HEAD, D), F32)
        la_s[0:P_LO, :] = jnp.zeros((P_LO, D), F32)
        lb_s[0:P_LO, :] = jnp.zeros((P_LO, D - G_C), F32)

    xb_s[...] = x_ref[...].astype(BF16)

    def proj(k):
        return _dot(xb_s[...], win_ref[:, k * D:(k + 1) * D])

    vn_s[...] = _layer_norm(proj(COL_V), lnvg_ref[...], lnvb_ref[...]).astype(BF16)
    u_s[...] = proj(COL_U)
    row = lax.broadcasted_iota(jnp.int32, (CHUNK, CHUNK), 0)
    col = lax.broadcasted_iota(jnp.int32, (CHUNK, CHUNK), 1)
    causal = col <= row
    for g in range(N_GROUPS_A):
        wg = jnp.where(causal, wsp_ref[g], 0.0).astype(BF16)
        bias = bsp_ref[:, g:g + 1]
        cs = slice(g * G_A, (g + 1) * G_A)
        for c in range(tm // CHUNK):
            rs = slice(c * CHUNK, (c + 1) * CHUNK)
            s = _dot(wg, vn_s[rs, cs]) + bias
            ha_s[rs, cs] = (u_s[rs, cs] * s).astype(BF16)
    acc_s[...] = jax.nn.sigmoid(proj(COL_GATE + 0)) * _dot(ha_s[...], wpa_ref[...])

    z = proj(COL_CG) * proj(COL_XB)
    z_s[Z_HEAD:Z_HEAD + tm, :] = z
    cw = cw_ref[...]
    y = cb_ref[...] + (cw[0:1] * z_s[Z_HEAD - 2:Z_HEAD - 2 + tm, :]
                       + cw[1:2] * z_s[Z_HEAD - 1:Z_HEAD - 1 + tm, :]
                       + cw[2:3] * z)
    tail = z_s[tm:tm + Z_HEAD, :]
    nconv_ref[...] = tail
    z_s[0:Z_HEAD, :] = tail
    hb_s[...] = (proj(COL_BG) * y).astype(BF16)
    acc_s[...] += jax.nn.sigmoid(proj(COL_GATE + 1)) * _dot(hb_s[...], wpb_ref[...])

    xc = proj(COL_XC)
    p_s[P_HEAD:P_HEAD + tm, :] = xc
    n = P_HEAD + tm
    la_s[P_LO:n, :] = p_s[P_LO:n, :] + p_s[P_LO - 1:n - 1, :]
    lb_s[P_LO:n, :] = la_s[P_LO:n, G_C:] + la_s[P_LO - 2:n - 2, G_C:]
    la_s[P_LO:n, 2 * G_C:] = lb_s[P_LO:n, G_C:] + lb_s[P_LO - 4:n - 4, G_C:]
    sum16 = la_s[P_HEAD:n, 3 * G_C:] + la_s[P_HEAD - 8:n - 8, 3 * G_C:]
    sums = (la_s[P_HEAD:n, 0:G_C], lb_s[P_HEAD:n, 0:G_C], la_s[P_HEAD:n, 2 * G_C:3 * G_C], sum16)
    pos1 = t * tm + lax.broadcasted_iota(jnp.int32, (tm, 1), 0) + 1
    psc = psc_ref[...]
    for g, w in enumerate(POOL_WINDOWS):
        cs = slice(g * G_C, (g + 1) * G_C)
        inv_cnt = 1.0 / jnp.minimum(pos1, w).astype(F32)
        d = sums[g] * inv_cnt - xc[:, cs]
        hc_s[:, cs] = (_dot(d.astype(BF16), wpool_ref[g]) * psc[:, cs]).astype(BF16)
    hist = p_s[tm + P_HEAD - 16:tm + P_HEAD, :]
    npool_ref[...] = hist
    p_s[P_HEAD - 16:P_HEAD, :] = hist
    acc_s[...] += jax.nn.sigmoid(proj(COL_GATE + 2)) * _dot(hc_s[...], wpc_ref[...])

    o = _dot(acc_s[...].astype(BF16), wo_ref[...])
    h_ref[...] = _layer_norm(ALPHA * x_ref[...] + o, l1g_ref[...], l1b_ref[...])


def _mixer_sample_kernel(x_ref, sconv_ref, spool_ref, win_ref, lnvg_ref, lnvb_ref, wsd_ref, bsd_ref, wpa_ref,
                         cw_ref, cb_ref, wpb_ref, wpool_ref, psc_ref, wpc_ref, wo_ref, l1g_ref, l1b_ref,
                         h_ref, z_ref, xc_ref, vn_ref):
    x = x_ref[...]
    xb = x.astype(BF16)

    def proj(k):
        return _dot(xb, win_ref[:, k * D:(k + 1) * D])

    vn = _layer_norm(proj(COL_V), lnvg_ref[...], lnvb_ref[...])
    vn_ref[...] = vn
    s = wsd_ref[...] * vn + bsd_ref[...]
    ha = (proj(COL_U) * s).astype(BF16)
    acc = jax.nn.sigmoid(proj(COL_GATE + 0)) * _dot(ha, wpa_ref[...])

    z = proj(COL_CG) * proj(COL_XB)
    z_ref[...] = z
    cw = cw_ref[...]
    y = cb_ref[...] + (cw[0:1] * sconv_ref[0] + cw[1:2] * sconv_ref[1] + cw[2:3] * z)
    hb = (proj(COL_BG) * y).astype(BF16)
    acc = acc + jax.nn.sigmoid(proj(COL_GATE + 1)) * _dot(hb, wpb_ref[...])

    xc = proj(COL_XC)
    xc_ref[...] = xc
    psc = psc_ref[...]
    hc = []
    for g, w in enumerate(POOL_WINDOWS):
        cs = slice(g * G_C, (g + 1) * G_C)
        tot = xc[:, cs]
        for j in range(w - 1):
            tot = tot + spool_ref[POOL_BUF - 1 - j, :, cs]
        d = tot * (1.0 / w) - xc[:, cs]
        hc.append((_dot(d.astype(BF16), wpool_ref[g]) * psc[:, cs]).astype(BF16))
    hc = jnp.concatenate(hc, axis=-1)
    acc = acc + jax.nn.sigmoid(proj(COL_GATE + 2)) * _dot(hc, wpc_ref[...])

    o = _dot(acc.astype(BF16), wo_ref[...])
    h_ref[...] = _layer_norm(ALPHA * x + o, l1g_ref[...], l1b_ref[...])


def _ffn_kernel(h_ref, w1_ref, w2_ref, g_ref, b_ref, o_ref, hb_s, acc_s):
    h = h_ref[...]
    hb_s[...] = h.astype(BF16)
    for j in range(D_FF // D):
        f = _dot(hb_s[...], w1_ref[:, j * D:(j + 1) * D])
        a = jnp.square(jnp.maximum(f, 0.0)).astype(BF16)
        c = _dot(a, w2_ref[j * D:(j + 1) * D, :])
        if j == 0:
            acc_s[...] = c
        else:
            acc_s[...] += c
    o_ref[...] = _layer_norm(ALPHA * h + acc_s[...], g_ref[...], b_ref[...])


def _resident(shape):
    nd = len(shape)
    return pl.BlockSpec(shape, lambda *_: (0,) * nd, pipeline_mode=pl.Buffered(1))


def _mixer_prompt(x2d, batch, seq, tm, p):
    nt = seq // tm
    row_block = pl.BlockSpec((tm, D), lambda b, t: (b * nt + t, 0))
    weights = (p['w_in'], p['lnv_g'], p['lnv_b'], p['w_spatial'], p['b_spatial_t'], p['w_proj_a'], p['conv_w'],
               p['conv_b'], p['w_proj_b'], p['w_pool'], p['pool_scale'], p['w_proj_c'], p['w_o'],
               p['ln1_g'], p['ln1_b'])
    return pl.pallas_call(
        functools.partial(_mixer_prompt_kernel, tm=tm),
        grid=(batch, nt),
        in_specs=[row_block] + [_resident(w.shape) for w in weights],
        out_specs=[row_block,
                   pl.BlockSpec((None, Z_HEAD, D), lambda b, t: (b, 0, 0)),
                   pl.BlockSpec((None, 16, D), lambda b, t: (b, 0, 0))],
        out_shape=[jax.ShapeDtypeStruct((batch * seq, D), F32),
                   jax.ShapeDtypeStruct((batch, Z_HEAD, D), F32),
                   jax.ShapeDtypeStruct((batch, 16, D), F32)],
        scratch_shapes=[pltpu.VMEM((tm, D), BF16),
                        pltpu.VMEM((tm, D), BF16),
                        pltpu.VMEM((tm, D), F32),
                        pltpu.VMEM((tm, D), BF16),
                        pltpu.VMEM((tm, D), BF16),
                        pltpu.VMEM((tm, D), BF16),
                        pltpu.VMEM((tm, D), F32),
                        pltpu.VMEM((Z_HEAD + tm, D), F32),
                        pltpu.VMEM((P_HEAD + tm, D), F32),
                        pltpu.VMEM((P_HEAD + tm, D), F32),
                        pltpu.VMEM((P_HEAD + tm, D - G_C), F32)],
        compiler_params=pltpu.CompilerParams(dimension_semantics=("arbitrary", "arbitrary"),
                                             vmem_limit_bytes=VMEM_LIMIT_MIXER),
        name="mixer_prompt",
    )(x2d, *weights)


def _mixer_sample(x2d, sconv_t, spool_t, p):
    n = x2d.shape[0]
    args = (x2d, sconv_t, spool_t, p['w_in'], p['lnv_g'], p['lnv_b'], p['w_spatial_d'], p['b_spatial_d'],
            p['w_proj_a'], p['conv_w'], p['conv_b'], p['w_proj_b'], p['w_pool'], p['pool_scale'], p['w_proj_c'],
            p['w_o'], p['ln1_g'], p['ln1_b'])
    return pl.pallas_call(
        _mixer_sample_kernel,
        grid=(1,),
        in_specs=[_resident(a.shape) for a in args],
        out_specs=[pl.BlockSpec((n, D), lambda i: (0, 0))] * 4,
        out_shape=[jax.ShapeDtypeStruct((n, D), F32)] * 4,
        compiler_params=pltpu.CompilerParams(dimension_semantics=("arbitrary",),
                                             vmem_limit_bytes=VMEM_LIMIT_MIXER),
        name="mixer_sample",
    )(*args)


def _ffn(h2d, tm, p):
    n = h2d.shape[0]
    row_block = pl.BlockSpec((tm, D), lambda i: (i, 0))
    weights = (p['w_ff1'], p['w_ff2'], p['ln2_g'], p['ln2_b'])
    return pl.pallas_call(
        _ffn_kernel,
        grid=(n // tm,),
        in_specs=[row_block] + [_resident(w.shape) for w in weights],
        out_specs=row_block,
        out_shape=jax.ShapeDtypeStruct((n, D), F32),
        scratch_shapes=[pltpu.VMEM((tm, D), BF16), pltpu.VMEM((tm, D), F32)],
        compiler_params=pltpu.CompilerParams(dimension_semantics=("arbitrary",),
                                             vmem_limit_bytes=VMEM_LIMIT_FFN),
        name="ffn",
    )(h2d, *weights)


TM_MIXER = 256
TM_FFN = 512


def kernel(x_prompt, x_sample, state_conv, state_pool, w_in, lnv_g, lnv_b, w_spatial, b_spatial, w_proj_a, conv_w, conv_b, w_proj_b, w_pool, pool_scale, w_proj_c, w_o, ln1_g, ln1_b, w_ff1, w_ff2, ln2_g, ln2_b):
    bp, seq, _ = x_prompt.shape
    bs = x_sample.shape[0]
    xp = x_prompt.reshape(bp * seq, D)
    xs = x_sample.reshape(bs, D)
    conv_p, pool_p, conv_s, pool_s, chunk_v_s = [], [], [], [], []
    for l in range(DEPTH):
        row = lambda a: a[l][None, :]
        p = {
            'w_in': w_in[l].astype(BF16), 'lnv_g': row(lnv_g), 'lnv_b': row(lnv_b),
            'w_spatial': w_spatial[l], 'b_spatial_t': b_spatial[l].T,
            'w_spatial_d': jnp.repeat(w_spatial[l][:, 0, 0], G_A)[None, :],
            'b_spatial_d': jnp.repeat(b_spatial[l][:, 0], G_A)[None, :],
            'w_proj_a': w_proj_a[l].astype(BF16), 'conv_w': conv_w[l], 'conv_b': row(conv_b),
            'w_proj_b': w_proj_b[l].astype(BF16), 'w_pool': w_pool[l].astype(BF16), 'pool_scale': row(pool_scale),
            'w_proj_c': w_proj_c[l].astype(BF16), 'w_o': w_o[l].astype(BF16), 'ln1_g': row(ln1_g), 'ln1_b': row(ln1_b),
            'w_ff1': w_ff1[l].astype(BF16), 'w_ff2': w_ff2[l].astype(BF16), 'ln2_g': row(ln2_g), 'ln2_b': row(ln2_b),
        }
        hp, nconv, npool = _mixer_prompt(xp, bp, seq, TM_MIXER, p)
        xp = _ffn(hp, TM_FFN, p)
        conv_p.append(nconv[:, Z_HEAD - (CONV_W - 1):, :])
        pool_p.append(npool[:, 16 - POOL_BUF:, :])
        sc, sp = state_conv[l], state_pool[l]
        hs, z_new, xc_new, vn = _mixer_sample(xs, jnp.swapaxes(sc, 0, 1), jnp.swapaxes(sp, 0, 1), p)
        xs = _ffn(hs, bs, p)
        conv_s.append(jnp.concatenate([sc[:, 1:], z_new[:, None, :]], axis=1))
        pool_s.append(jnp.concatenate([sp[:, 1:], xc_new[:, None, :]], axis=1))
        chunk_v_s.append(vn[:, None, :])
    return (xp.reshape(bp, seq, D), xs.reshape(bs, 1, D), jnp.stack(conv_p), jnp.stack(pool_p),
            jnp.stack(conv_s), jnp.stack(pool_s), jnp.stack(chunk_v_s))
```

```python
import functools

import jax
import jax.numpy as jnp
from jax import lax
from jax.experimental import pallas as pl
from jax.experimental.pallas import tpu as pltpu

D = 1024
N_GROUPS_A = 4
CHUNK = 128
G_A = D // N_GROUPS_A
CONV_W = 3
POOL_WINDOWS = (2, 4, 8, 16)
G_C = D // len(POOL_WINDOWS)
POOL_BUF = max(POOL_WINDOWS) - 1
D_FF = 4 * D
DEPTH = 2
ALPHA = float((2 * DEPTH) ** 0.25)
LN_EPS = 1e-5
PAST_LEN = 16384

COL_U, COL_V, COL_BG, COL_CG, COL_XB, COL_XC, COL_GATE = 0, 1, 2, 3, 4, 5, 6

SUBLANES = 8
Z_HEAD = SUBLANES
P_HEAD = 32
P_LO = 16

VMEM_LIMIT_MIXER = 56 * 1024 * 1024
VMEM_LIMIT_FFN = 48 * 1024 * 1024

F32 = jnp.float32
BF16 = jnp.bfloat16


def _dot(a, b):
    return jnp.dot(a, b, preferred_element_type=F32)


def _layer_norm(x, g, b):
    mu = jnp.mean(x, axis=-1, keepdims=True)
    xc = x - mu
    var = jnp.mean(xc * xc, axis=-1, keepdims=True)
    return xc * lax.rsqrt(var + LN_EPS) * g + b


def _mixer_prompt_kernel(x_ref, win_ref, lnvg_ref, lnvb_ref, wsp_ref, bsp_ref, wpa_ref, cw_ref, cb_ref,
                         wpb_ref, wpool_ref, psc_ref, wpc_ref, wo_ref, l1g_ref, l1b_ref,
                         h_ref, nconv_ref, npool_ref,
                         xb_s, vn_s, u_s, ha_s, hb_s, hc_s, acc_s, z_s, p_s, la_s, lb_s, *, tm):
    t = pl.program_id(1)

    @pl.when(t == 0)
    def _():
        z_s[0:Z_HEAD, :] = jnp.zeros((Z_HEAD, D), F32)
        p_s[0:P_HEAD, :] = jnp.zeros((P_HEAD, D), F32)
        la_s[0:P_LO, :] = jnp.zeros((P_LO, D), F32)
        lb_s[0:P_LO, :] = jnp.zeros((P_LO, D - G_C), F32)

    xb_s[...] = x_ref[...].astype(BF16)

    def proj(k):
        return _dot(xb_s[...], win_ref[:, k * D:(k + 1) * D])

    vn_s[...] = _layer_norm(proj(COL_V), lnvg_ref[...], lnvb_ref[...]).astype(BF16)
    u_s[...] = proj(COL_U)
    row = lax.broadcasted_iota(jnp.int32, (CHUNK, CHUNK), 0)
    col = lax.broadcasted_iota(jnp.int32, (CHUNK, CHUNK), 1)
    causal = col <= row
    for g in range(N_GROUPS_A):
        wg = jnp.where(causal, wsp_ref[g], 0.0).astype(BF16)
        bias = bsp_ref[:, g:g + 1]
        cs = slice(g * G_A, (g + 1) * G_A)
        for c in range(tm // CHUNK):
            rs = slice(c * CHUNK, (c + 1) * CHUNK)
            s = _dot(wg, vn_s[rs, cs]) + bias
            ha_s[rs, cs] = (u_s[rs, cs] * s).astype(BF16)
    acc_s[...] = jax.nn.sigmoid(proj(COL_GATE + 0)) * _dot(ha_s[...], wpa_ref[...])

    z = proj(COL_CG) * proj(COL_XB)
    z_s[Z_HEAD:Z_HEAD + tm, :] = z
    cw = cw_ref[...]
    y = cb_ref[...] + (cw[0:1] * z_s[Z_HEAD - 2:Z_HEAD - 2 + tm, :]
                       + cw[1:2] * z_s[Z_HEAD - 1:Z_HEAD - 1 + tm, :]
                       + cw[2:3] * z)
    tail = z_s[tm:tm + Z_HEAD, :]
    nconv_ref[...] = tail
    z_s[0:Z_HEAD, :] = tail
    hb_s[...] = (proj(COL_BG) * y).astype(BF16)
    acc_s[...] += jax.nn.sigmoid(proj(COL_GATE + 1)) * _dot(hb_s[...], wpb_ref[...])

    xc = proj(COL_XC)
    p_s[P_HEAD:P_HEAD + tm, :] = xc
    n = P_HEAD + tm
    la_s[P_LO:n, :] = p_s[P_LO:n, :] + p_s[P_LO - 1:n - 1, :]
    lb_s[P_LO:n, :] = la_s[P_LO:n, G_C:] + la_s[P_LO - 2:n - 2, G_C:]
    la_s[P_LO:n, 2 * G_C:] = lb_s[P_LO:n, G_C:] + lb_s[P_LO - 4:n - 4, G_C:]
    sum16 = la_s[P_HEAD:n, 3 * G_C:] + la_s[P_HEAD - 8:n - 8, 3 * G_C:]
    sums = (la_s[P_HEAD:n, 0:G_C], lb_s[P_HEAD:n, 0:G_C], la_s[P_HEAD:n, 2 * G_C:3 * G_C], sum16)
    pos1 = t * tm + lax.broadcasted_iota(jnp.int32, (tm, 1), 0) + 1
    psc = psc_ref[...]
    for g, w in enumerate(POOL_WINDOWS):
        cs = slice(g * G_C, (g + 1) * G_C)
        inv_cnt = 1.0 / jnp.minimum(pos1, w).astype(F32)
        d = sums[g] * inv_cnt - xc[:, cs]
        hc_s[:, cs] = (_dot(d.astype(BF16), wpool_ref[g]) * psc[:, cs]).astype(BF16)
    hist = p_s[tm + P_HEAD - 16:tm + P_HEAD, :]
    npool_ref[...] = hist
    p_s[P_HEAD - 16:P_HEAD, :] = hist
    acc_s[...] += jax.nn.sigmoid(proj(COL_GATE + 2)) * _dot(hc_s[...], wpc_ref[...])

    o = _dot(acc_s[...].astype(BF16), wo_ref[...])
    h_ref[...] = _layer_norm(ALPHA * x_ref[...] + o, l1g_ref[...], l1b_ref[...])


def _mixer_sample_kernel(x_ref, sconv_ref, spool_ref, win_ref, lnvg_ref, lnvb_ref, wsd_ref, bsd_ref, wpa_ref,
                         cw_ref, cb_ref, wpb_ref, wpool_ref, psc_ref, wpc_ref, wo_ref, l1g_ref, l1b_ref,
                         h_ref, z_ref, xc_ref, vn_ref):
    x = x_ref[...]
    xb = x.astype(BF16)

    def proj(k):
        return _dot(xb, win_ref[:, k * D:(k + 1) * D])

    vn = _layer_norm(proj(COL_V), lnvg_ref[...], lnvb_ref[...])
    vn_ref[...] = vn
    s = wsd_ref[...] * vn + bsd_ref[...]
    ha = (proj(COL_U) * s).astype(BF16)
    acc = jax.nn.sigmoid(proj(COL_GATE + 0)) * _dot(ha, wpa_ref[...])

    z = proj(COL_CG) * proj(COL_XB)
    z_ref[...] = z
    cw = cw_ref[...]
    y = cb_ref[...] + (cw[0:1] * sconv_ref[0] + cw[1:2] * sconv_ref[1] + cw[2:3] * z)
    hb = (proj(COL_BG) * y).astype(BF16)
    acc = acc + jax.nn.sigmoid(proj(COL_GATE + 1)) * _dot(hb, wpb_ref[...])

    xc = proj(COL_XC)
    xc_ref[...] = xc
    psc = psc_ref[...]
    hc = []
    for g, w in enumerate(POOL_WINDOWS):
        cs = slice(g * G_C, (g + 1) * G_C)
        tot = xc[:, cs]
        for j in range(w - 1):
            tot = tot + spool_ref[POOL_BUF - 1 - j, :, cs]
        d = tot * (1.0 / w) - xc[:, cs]
        hc.append((_dot(d.astype(BF16), wpool_ref[g]) * psc[:, cs]).astype(BF16))
    hc = jnp.concatenate(hc, axis=-1)
    acc = acc + jax.nn.sigmoid(proj(COL_GATE + 2)) * _dot(hc, wpc_ref[...])

    o = _dot(acc.astype(BF16), wo_ref[...])
    h_ref[...] = _layer_norm(ALPHA * x + o, l1g_ref[...], l1b_ref[...])


def _ffn_kernel(h_ref, w1_ref, w2_ref, g_ref, b_ref, o_ref, hb_s, acc_s):
    h = h_ref[...]
    hb_s[...] = h.astype(BF16)
    for j in range(D_FF // D):
        f = _dot(hb_s[...], w1_ref[:, j * D:(j + 1) * D])
        a = jnp.square(jnp.maximum(f, 0.0)).astype(BF16)
        c = _dot(a, w2_ref[j * D:(j + 1) * D, :])
        if j == 0:
            acc_s[...] = c
        else:
            acc_s[...] += c
    o_ref[...] = _layer_norm(ALPHA * h + acc_s[...], g_ref[...], b_ref[...])


def _resident(shape):
    nd = len(shape)
    return pl.BlockSpec(shape, lambda *_: (0,) * nd, pipeline_mode=pl.Buffered(1))


def _layer_resident(a, l):
    return pl.BlockSpec((None,) + a.shape[1:], lambda *_: (l,) + (0,) * (a.ndim - 1), pipeline_mode=pl.Buffered(1))


def _mixer_prompt(x2d, batch, seq, tm, p, l):
    nt = seq // tm
    row_block = pl.BlockSpec((tm, D), lambda b, t: (b * nt + t, 0))
    weights = (p['w_in'], p['lnv_g'], p['lnv_b'], p['w_spatial'], p['b_spatial_t'], p['w_proj_a'], p['conv_w'],
               p['conv_b'], p['w_proj_b'], p['w_pool'], p['pool_scale'], p['w_proj_c'], p['w_o'],
               p['ln1_g'], p['ln1_b'])
    return pl.pallas_call(
        functools.partial(_mixer_prompt_kernel, tm=tm),
        grid=(batch, nt),
        in_specs=[row_block] + [_layer_resident(w, l) for w in weights],
        out_specs=[row_block,
                   pl.BlockSpec((None, Z_HEAD, D), lambda b, t: (b, 0, 0)),
                   pl.BlockSpec((None, 16, D), lambda b, t: (b, 0, 0))],
        out_shape=[jax.ShapeDtypeStruct((batch * seq, D), F32),
                   jax.ShapeDtypeStruct((batch, Z_HEAD, D), F32),
                   jax.ShapeDtypeStruct((batch, 16, D), F32)],
        scratch_shapes=[pltpu.VMEM((tm, D), BF16),
                        pltpu.VMEM((tm, D), BF16),
                        pltpu.VMEM((tm, D), F32),
                        pltpu.VMEM((tm, D), BF16),
                        pltpu.VMEM((tm, D), BF16),
                        pltpu.VMEM((tm, D), BF16),
                        pltpu.VMEM((tm, D), F32),
                        pltpu.VMEM((Z_HEAD + tm, D), F32),
                        pltpu.VMEM((P_HEAD + tm, D), F32),
                        pltpu.VMEM((P_HEAD + tm, D), F32),
                        pltpu.VMEM((P_HEAD + tm, D - G_C), F32)],
        compiler_params=pltpu.CompilerParams(dimension_semantics=("arbitrary", "arbitrary"),
                                             vmem_limit_bytes=VMEM_LIMIT_MIXER),
        name="mixer_prompt",
    )(x2d, *weights)


def _mixer_sample(x2d, p, l):
    n = x2d.shape[0]
    per_layer = (p['state_conv_t'], p['state_pool_t'], p['w_in'], p['lnv_g'], p['lnv_b'], p['w_spatial_d'],
                 p['b_spatial_d'], p['w_proj_a'], p['conv_w'], p['conv_b'], p['w_proj_b'], p['w_pool'],
                 p['pool_scale'], p['w_proj_c'], p['w_o'], p['ln1_g'], p['ln1_b'])
    return pl.pallas_call(
        _mixer_sample_kernel,
        grid=(1,),
        in_specs=[_resident(x2d.shape)] + [_layer_resident(a, l) for a in per_layer],
        out_specs=[pl.BlockSpec((n, D), lambda i: (0, 0))] * 4,
        out_shape=[jax.ShapeDtypeStruct((n, D), F32)] * 4,
        compiler_params=pltpu.CompilerParams(dimension_semantics=("arbitrary",),
                                             vmem_limit_bytes=VMEM_LIMIT_MIXER),
        name="mixer_sample",
    )(x2d, *per_layer)


def _ffn(h2d, tm, p, l):
    n = h2d.shape[0]
    row_block = pl.BlockSpec((tm, D), lambda i: (i, 0))
    weights = (p['w_ff1'], p['w_ff2'], p['ln2_g'], p['ln2_b'])
    return pl.pallas_call(
        _ffn_kernel,
        grid=(n // tm,),
        in_specs=[row_block] + [_layer_resident(w, l) for w in weights],
        out_specs=row_block,
        out_shape=jax.ShapeDtypeStruct((n, D), F32),
        scratch_shapes=[pltpu.VMEM((tm, D), BF16), pltpu.VMEM((tm, D), F32)],
        compiler_params=pltpu.CompilerParams(dimension_semantics=("arbitrary",),
                                             vmem_limit_bytes=VMEM_LIMIT_FFN),
        name="ffn",
    )(h2d, *weights)


TM_MIXER = 512
TM_FFN = 1024


def kernel(x_prompt, x_sample, state_conv, state_pool, w_in, lnv_g, lnv_b, w_spatial, b_spatial, w_proj_a, conv_w, conv_b, w_proj_b, w_pool, pool_scale, w_proj_c, w_o, ln1_g, ln1_b, w_ff1, w_ff2, ln2_g, ln2_b):
    bp, seq, _ = x_prompt.shape
    bs = x_sample.shape[0]
    xp = x_prompt.reshape(bp * seq, D)
    xs = x_sample.reshape(bs, D)
    conv_p, pool_p, conv_s, pool_s, chunk_v_s = [], [], [], [], []
    rows = lambda a: a[:, None, :]
    p = {
        'w_in': w_in.astype(BF16), 'lnv_g': rows(lnv_g), 'lnv_b': rows(lnv_b),
        'w_spatial': w_spatial, 'b_spatial_t': jnp.swapaxes(b_spatial, 1, 2),
        'w_spatial_d': rows(jnp.repeat(w_spatial[:, :, 0, 0], G_A, axis=1)),
        'b_spatial_d': rows(jnp.repeat(b_spatial[:, :, 0], G_A, axis=1)),
        'w_proj_a': w_proj_a.astype(BF16), 'conv_w': conv_w, 'conv_b': rows(conv_b),
        'w_proj_b': w_proj_b.astype(BF16), 'w_pool': w_pool.astype(BF16), 'pool_scale': rows(pool_scale),
        'w_proj_c': w_proj_c.astype(BF16), 'w_o': w_o.astype(BF16), 'ln1_g': rows(ln1_g), 'ln1_b': rows(ln1_b),
        'w_ff1': w_ff1.astype(BF16), 'w_ff2': w_ff2.astype(BF16), 'ln2_g': rows(ln2_g), 'ln2_b': rows(ln2_b),
        'state_conv_t': jnp.swapaxes(state_conv, 1, 2), 'state_pool_t': jnp.swapaxes(state_pool, 1, 2),
    }
    for l in range(DEPTH):
        hp, nconv, npool = _mixer_prompt(xp, bp, seq, TM_MIXER, p, l)
        xp = _ffn(hp, TM_FFN, p, l)
        conv_p.append(nconv[:, Z_HEAD - (CONV_W - 1):, :])
        pool_p.append(npool[:, 16 - POOL_BUF:, :])
        hs, z_new, xc_new, vn = _mixer_sample(xs, p, l)
        xs = _ffn(hs, bs, p, l)
        conv_s.append(jnp.concatenate([state_conv[l][:, 1:], z_new[:, None, :]], axis=1))
        pool_s.append(jnp.concatenate([state_pool[l][:, 1:], xc_new[:, None, :]], axis=1))
        chunk_v_s.append(vn[:, None, :])
    return (xp.reshape(bp, seq, D), xs.reshape(bs, 1, D), jnp.stack(conv_p), jnp.stack(pool_p),
            jnp.stack(conv_s), jnp.stack(pool_s), jnp.stack(chunk_v_s))
```

```python
import functools

import jax
import jax.numpy as jnp
from jax import lax
from jax.experimental import pallas as pl
from jax.experimental.pallas import tpu as pltpu

D = 1024
N_GROUPS_A = 4
CHUNK = 128
G_A = D // N_GROUPS_A
CONV_W = 3
POOL_WINDOWS = (2, 4, 8, 16)
G_C = D // len(POOL_WINDOWS)
POOL_BUF = max(POOL_WINDOWS) - 1
D_FF = 4 * D
DEPTH = 2
ALPHA = float((2 * DEPTH) ** 0.25)
LN_EPS = 1e-5
PAST_LEN = 16384

COL_U, COL_V, COL_BG, COL_CG, COL_XB, COL_XC, COL_GATE = 0, 1, 2, 3, 4, 5, 6

SUBLANES = 8
Z_HEAD = SUBLANES
P_HEAD = 32
P_LO = 16

VMEM_LIMIT_MIXER = 56 * 1024 * 1024
VMEM_LIMIT_FFN = 48 * 1024 * 1024

F32 = jnp.float32
BF16 = jnp.bfloat16


def _dot(a, b):
    return jnp.dot(a, b, preferred_element_type=F32)


def _layer_norm(x, g, b):
    mu = jnp.mean(x, axis=-1, keepdims=True)
    xc = x - mu
    var = jnp.mean(xc * xc, axis=-1, keepdims=True)
    return xc * lax.rsqrt(var + LN_EPS) * g + b


def _mixer_prompt_kernel(x_ref, win_ref, lnvg_ref, lnvb_ref, wsp_ref, bsp_ref, wpa_ref, cw_ref, cb_ref,
                         wpb_ref, wpool_ref, psc_ref, wpc_ref, wo_ref, l1g_ref, l1b_ref,
                         h_ref, nconv_ref, npool_ref,
                         xb_s, vn_s, u_s, ha_s, hb_s, hc_s, acc_s, z_s, p_s, la_s, lb_s, lc_s, *, tm, sub):
    t = pl.program_id(1)

    @pl.when(t == 0)
    def _():
        z_s[0:Z_HEAD, :] = jnp.zeros((Z_HEAD, D), F32)
        p_s[0:P_HEAD, :] = jnp.zeros((P_HEAD, D), F32)
        la_s[0:P_LO, :] = jnp.zeros((P_LO, D), F32)
        lb_s[0:P_LO, :] = jnp.zeros((P_LO, D - G_C), F32)
        lc_s[0:P_LO, :] = jnp.zeros((P_LO, D - 2 * G_C), F32)

    row = lax.broadcasted_iota(jnp.int32, (CHUNK, CHUNK), 0)
    col = lax.broadcasted_iota(jnp.int32, (CHUNK, CHUNK), 1)
    causal = col <= row
    w_spatial = [jnp.where(causal, wsp_ref[g], 0.0).astype(BF16) for g in range(N_GROUPS_A)]
    cw = cw_ref[...]
    psc = psc_ref[...]

    for r0 in range(0, tm, sub):
        rs = slice(r0, r0 + sub)
        xb_s[rs, :] = x_ref[rs, :].astype(BF16)

        def proj(k):
            return _dot(xb_s[rs, :], win_ref[:, k * D:(k + 1) * D])

        xc = proj(COL_XC)
        p0, p1 = P_HEAD + r0, P_HEAD + r0 + sub
        p_s[p0:p1, :] = xc
        lo = P_LO if r0 == 0 else p0
        la_s[lo:p1, :] = p_s[lo:p1, :] + p_s[lo - 1:p1 - 1, :]
        lb_s[lo:p1, :] = la_s[lo:p1, G_C:] + la_s[lo - 2:p1 - 2, G_C:]
        lc_s[lo:p1, :] = lb_s[lo:p1, G_C:] + lb_s[lo - 4:p1 - 4, G_C:]
        sum16 = lc_s[p0:p1, G_C:] + lc_s[p0 - 8:p1 - 8, G_C:]
        sums = (la_s[p0:p1, 0:G_C], lb_s[p0:p1, 0:G_C], lc_s[p0:p1, 0:G_C], sum16)
        pos1 = t * tm + r0 + lax.broadcasted_iota(jnp.int32, (sub, 1), 0) + 1
        for g, w in enumerate(POOL_WINDOWS):
            cs = slice(g * G_C, (g + 1) * G_C)
            inv_cnt = 1.0 / jnp.minimum(pos1, w).astype(F32)
            d = sums[g] * inv_cnt - xc[:, cs]
            hc_s[rs, cs] = (_dot(d.astype(BF16), wpool_ref[g]) * psc[:, cs]).astype(BF16)
        acc_s[rs, :] = jax.nn.sigmoid(proj(COL_GATE + 2)) * _dot(hc_s[rs, :], wpc_ref[...])

        vn_s[rs, :] = _layer_norm(proj(COL_V), lnvg_ref[...], lnvb_ref[...]).astype(BF16)
        u_s[rs, :] = proj(COL_U)
        for g in range(N_GROUPS_A):
            bias = bsp_ref[:, g:g + 1]
            cs = slice(g * G_A, (g + 1) * G_A)
            for c0 in range(r0, r0 + sub, CHUNK):
                ch = slice(c0, c0 + CHUNK)
                s = _dot(w_spatial[g], vn_s[ch, cs]) + bias
                ha_s[ch, cs] = (u_s[ch, cs] * s).astype(BF16)
        acc_s[rs, :] += jax.nn.sigmoid(proj(COL_GATE + 0)) * _dot(ha_s[rs, :], wpa_ref[...])

        z = proj(COL_CG) * proj(COL_XB)
        z0 = Z_HEAD + r0
        z_s[z0:z0 + sub, :] = z
        y = cb_ref[...] + (cw[0:1] * z_s[z0 - 2:z0 - 2 + sub, :]
                           + cw[1:2] * z_s[z0 - 1:z0 - 1 + sub, :]
                           + cw[2:3] * z)
        hb_s[rs, :] = (proj(COL_BG) * y).astype(BF16)
        acc_s[rs, :] += jax.nn.sigmoid(proj(COL_GATE + 1)) * _dot(hb_s[rs, :], wpb_ref[...])

        o = _dot(acc_s[rs, :].astype(BF16), wo_ref[...])
        h_ref[rs, :] = _layer_norm(ALPHA * x_ref[rs, :] + o, l1g_ref[...], l1b_ref[...])

    tail = z_s[tm:tm + Z_HEAD, :]
    nconv_ref[...] = tail
    z_s[0:Z_HEAD, :] = tail
    hist = p_s[tm + P_HEAD - 16:tm + P_HEAD, :]
    npool_ref[...] = hist
    p_s[P_HEAD - 16:P_HEAD, :] = hist


def _mixer_sample_kernel(x_ref, sconv_ref, spool_ref, win_ref, lnvg_ref, lnvb_ref, wsd_ref, bsd_ref, wpa_ref,
                         cw_ref, cb_ref, wpb_ref, wpool_ref, psc_ref, wpc_ref, wo_ref, l1g_ref, l1b_ref,
                         h_ref, z_ref, xc_ref, vn_ref):
    x = x_ref[...]
    xb = x.astype(BF16)

    def proj(k):
        return _dot(xb, win_ref[:, k * D:(k + 1) * D])

    vn = _layer_norm(proj(COL_V), lnvg_ref[...], lnvb_ref[...])
    vn_ref[...] = vn
    s = wsd_ref[...] * vn + bsd_ref[...]
    ha = (proj(COL_U) * s).astype(BF16)
    acc = jax.nn.sigmoid(proj(COL_GATE + 0)) * _dot(ha, wpa_ref[...])

    z = proj(COL_CG) * proj(COL_XB)
    z_ref[...] = z
    cw = cw_ref[...]
    y = cb_ref[...] + (cw[0:1] * sconv_ref[0] + cw[1:2] * sconv_ref[1] + cw[2:3] * z)
    hb = (proj(COL_BG) * y).astype(BF16)
    acc = acc + jax.nn.sigmoid(proj(COL_GATE + 1)) * _dot(hb, wpb_ref[...])

    xc = proj(COL_XC)
    xc_ref[...] = xc
    psc = psc_ref[...]
    hc = []
    for g, w in enumerate(POOL_WINDOWS):
        cs = slice(g * G_C, (g + 1) * G_C)
        tot = xc[:, cs]
        for j in range(w - 1):
            tot = tot + spool_ref[POOL_BUF - 1 - j, :, cs]
        d = tot * (1.0 / w) - xc[:, cs]
        hc.append((_dot(d.astype(BF16), wpool_ref[g]) * psc[:, cs]).astype(BF16))
    hc = jnp.concatenate(hc, axis=-1)
    acc = acc + jax.nn.sigmoid(proj(COL_GATE + 2)) * _dot(hc, wpc_ref[...])

    o = _dot(acc.astype(BF16), wo_ref[...])
    h_ref[...] = _layer_norm(ALPHA * x + o, l1g_ref[...], l1b_ref[...])


def _ffn_kernel(h_ref, w1_ref, w2_ref, g_ref, b_ref, o_ref, hb_s, acc_s, *, sub):
    for r in range(h_ref.shape[0] // sub):
        rs = slice(r * sub, (r + 1) * sub)
        hb_s[rs, :] = h_ref[rs, :].astype(BF16)
        for j in range(D_FF // D):
            f = _dot(hb_s[rs, :], w1_ref[:, j * D:(j + 1) * D])
            a = jnp.square(jnp.maximum(f, 0.0)).astype(BF16)
            c = _dot(a, w2_ref[j * D:(j + 1) * D, :])
            if j == 0:
                acc_s[rs, :] = c
            else:
                acc_s[rs, :] += c
        o_ref[rs, :] = _layer_norm(ALPHA * h_ref[rs, :] + acc_s[rs, :], g_ref[...], b_ref[...])


def _resident(shape):
    nd = len(shape)
    return pl.BlockSpec(shape, lambda *_: (0,) * nd, pipeline_mode=pl.Buffered(1))


def _layer_resident(a, l):
    return pl.BlockSpec((None,) + a.shape[1:], lambda *_: (l,) + (0,) * (a.ndim - 1), pipeline_mode=pl.Buffered(1))


def _mixer_prompt(x2d, batch, seq, tm, p, l):
    nt = seq // tm
    row_block = pl.BlockSpec((tm, D), lambda b, t: (b * nt + t, 0))
    weights = (p['w_in'], p['lnv_g'], p['lnv_b'], p['w_spatial'], p['b_spatial_t'], p['w_proj_a'], p['conv_w'],
               p['conv_b'], p['w_proj_b'], p['w_pool'], p['pool_scale'], p['w_proj_c'], p['w_o'],
               p['ln1_g'], p['ln1_b'])
    return pl.pallas_call(
        functools.partial(_mixer_prompt_kernel, tm=tm, sub=min(tm, SUB_MIXER)),
        grid=(batch, nt),
        in_specs=[row_block] + [_layer_resident(w, l) for w in weights],
        out_specs=[row_block,
                   pl.BlockSpec((None, Z_HEAD, D), lambda b, t: (b, 0, 0)),
                   pl.BlockSpec((None, 16, D), lambda b, t: (b, 0, 0))],
        out_shape=[jax.ShapeDtypeStruct((batch * seq, D), F32),
                   jax.ShapeDtypeStruct((batch, Z_HEAD, D), F32),
                   jax.ShapeDtypeStruct((batch, 16, D), F32)],
        scratch_shapes=[pltpu.VMEM((tm, D), BF16),
                        pltpu.VMEM((tm, D), BF16),
                        pltpu.VMEM((tm, D), F32),
                        pltpu.VMEM((tm, D), BF16),
                        pltpu.VMEM((tm, D), BF16),
                        pltpu.VMEM((tm, D), BF16),
                        pltpu.VMEM((tm, D), F32),
                        pltpu.VMEM((Z_HEAD + tm, D), F32),
                        pltpu.VMEM((P_HEAD + tm, D), F32),
                        pltpu.VMEM((P_HEAD + tm, D), F32),
                        pltpu.VMEM((P_HEAD + tm, D - G_C), F32),
                        pltpu.VMEM((P_HEAD + tm, D - 2 * G_C), F32)],
        compiler_params=pltpu.CompilerParams(dimension_semantics=("arbitrary", "arbitrary"),
                                             vmem_limit_bytes=VMEM_LIMIT_MIXER),
        name="mixer_prompt",
    )(x2d, *weights)


def _mixer_sample(x2d, p, l):
    n = x2d.shape[0]
    per_layer = (p['state_conv_t'], p['state_pool_t'], p['w_in'], p['lnv_g'], p['lnv_b'], p['w_spatial_d'],
                 p['b_spatial_d'], p['w_proj_a'], p['conv_w'], p['conv_b'], p['w_proj_b'], p['w_pool'],
                 p['pool_scale'], p['w_proj_c'], p['w_o'], p['ln1_g'], p['ln1_b'])
    return pl.pallas_call(
        _mixer_sample_kernel,
        grid=(1,),
        in_specs=[_resident(x2d.shape)] + [_layer_resident(a, l) for a in per_layer],
        out_specs=[pl.BlockSpec((n, D), lambda i: (0, 0))] * 4,
        out_shape=[jax.ShapeDtypeStruct((n, D), F32)] * 4,
        compiler_params=pltpu.CompilerParams(dimension_semantics=("arbitrary",),
                                             vmem_limit_bytes=VMEM_LIMIT_MIXER),
        name="mixer_sample",
    )(x2d, *per_layer)


def _ffn(h2d, tm, p, l):
    n = h2d.shape[0]
    row_block = pl.BlockSpec((tm, D), lambda i: (i, 0))
    weights = (p['w_ff1'], p['w_ff2'], p['ln2_g'], p['ln2_b'])
    return pl.pallas_call(
        functools.partial(_ffn_kernel, sub=min(tm, SUB_FFN)),
        grid=(n // tm,),
        in_specs=[row_block] + [_layer_resident(w, l) for w in weights],
        out_specs=row_block,
        out_shape=jax.ShapeDtypeStruct((n, D), F32),
        scratch_shapes=[pltpu.VMEM((tm, D), BF16), pltpu.VMEM((tm, D), F32)],
        compiler_params=pltpu.CompilerParams(dimension_semantics=("arbitrary",),
                                             vmem_limit_bytes=VMEM_LIMIT_FFN),
        name="ffn",
    )(h2d, *weights)


TM_MIXER = 512
TM_FFN = 1024
SUB_FFN = 256
SUB_MIXER = 256


def kernel(x_prompt, x_sample, state_conv, state_pool, w_in, lnv_g, lnv_b, w_spatial, b_spatial, w_proj_a, conv_w, conv_b, w_proj_b, w_pool, pool_scale, w_proj_c, w_o, ln1_g, ln1_b, w_ff1, w_ff2, ln2_g, ln2_b):
    bp, seq, _ = x_prompt.shape
    bs = x_sample.shape[0]
    xp = x_prompt.reshape(bp * seq, D)
    xs = x_sample.reshape(bs, D)
    conv_p, pool_p, conv_s, pool_s, chunk_v_s = [], [], [], [], []
    rows = lambda a: a[:, None, :]
    p = {
        'w_in': w_in.astype(BF16), 'lnv_g': rows(lnv_g), 'lnv_b': rows(lnv_b),
        'w_spatial': w_spatial, 'b_spatial_t': jnp.swapaxes(b_spatial, 1, 2),
        'w_spatial_d': rows(jnp.repeat(w_spatial[:, :, 0, 0], G_A, axis=1)),
        'b_spatial_d': rows(jnp.repeat(b_spatial[:, :, 0], G_A, axis=1)),
        'w_proj_a': w_proj_a.astype(BF16), 'conv_w': conv_w, 'conv_b': rows(conv_b),
        'w_proj_b': w_proj_b.astype(BF16), 'w_pool': w_pool.astype(BF16), 'pool_scale': rows(pool_scale),
        'w_proj_c': w_proj_c.astype(BF16), 'w_o': w_o.astype(BF16), 'ln1_g': rows(ln1_g), 'ln1_b': rows(ln1_b),
        'w_ff1': w_ff1.astype(BF16), 'w_ff2': w_ff2.astype(BF16), 'ln2_g': rows(ln2_g), 'ln2_b': rows(ln2_b),
        'state_conv_t': jnp.swapaxes(state_conv, 1, 2), 'state_pool_t': jnp.swapaxes(state_pool, 1, 2),
    }
    for l in range(DEPTH):
        hp, nconv, npool = _mixer_prompt(xp, bp, seq, TM_MIXER, p, l)
        xp = _ffn(hp, TM_FFN, p, l)
        conv_p.append(nconv[:, Z_HEAD - (CONV_W - 1):, :])
        pool_p.append(npool[:, 16 - POOL_BUF:, :])
        hs, z_new, xc_new, vn = _mixer_sample(xs, p, l)
        xs = _ffn(hs, bs, p, l)
        conv_s.append(jnp.concatenate([state_conv[l][:, 1:], z_new[:, None, :]], axis=1))
        pool_s.append(jnp.concatenate([state_pool[l][:, 1:], xc_new[:, None, :]], axis=1))
        chunk_v_s.append(vn[:, None, :])
    return (xp.reshape(bp, seq, D), xs.reshape(bs, 1, D), jnp.stack(conv_p), jnp.stack(pool_p),
            jnp.stack(conv_s), jnp.stack(pool_s), jnp.stack(chunk_v_s))
```

```python
import functools

import jax
import jax.numpy as jnp
from jax import lax
from jax.experimental import pallas as pl
from jax.experimental.pallas import tpu as pltpu

D = 1024
N_GROUPS_A = 4
CHUNK = 128
G_A = D // N_GROUPS_A
CONV_W = 3
POOL_WINDOWS = (2, 4, 8, 16)
G_C = D // len(POOL_WINDOWS)
POOL_BUF = max(POOL_WINDOWS) - 1
D_FF = 4 * D
DEPTH = 2
ALPHA = float((2 * DEPTH) ** 0.25)
LN_EPS = 1e-5
PAST_LEN = 16384

COL_U, COL_V, COL_BG, COL_CG, COL_XB, COL_XC, COL_GATE = 0, 1, 2, 3, 4, 5, 6

SUBLANES = 8
Z_HEAD = SUBLANES
P_HEAD = 32
P_LO = 16

VMEM_LIMIT_MIXER = 56 * 1024 * 1024
VMEM_LIMIT_FFN = 48 * 1024 * 1024

F32 = jnp.float32
BF16 = jnp.bfloat16


def _dot(a, b):
    return jnp.dot(a, b, preferred_element_type=F32)


def _layer_norm(x, g, b):
    mu = jnp.mean(x, axis=-1, keepdims=True)
    xc = x - mu
    var = jnp.mean(xc * xc, axis=-1, keepdims=True)
    return xc * lax.rsqrt(var + LN_EPS) * g + b


def _mixer_prompt_kernel(x_ref, win_ref, lnvg_ref, lnvb_ref, wsp_ref, bsp_ref, wpa_ref, cw_ref, cb_ref,
                         wpb_ref, wpool_ref, psc_ref, wpc_ref, wo_ref, l1g_ref, l1b_ref, wf1_ref, wf2_ref,
                         h_ref, nconv_ref, npool_ref, wf1o_ref, wf2o_ref,
                         xb_s, vn_s, u_s, ha_s, hb_s, hc_s, acc_s, z_s, p_s, la_s, lb_s, lc_s, *, tm, sub):
    t = pl.program_id(1)

    @pl.when(t == 0)
    def _():
        z_s[0:Z_HEAD, :] = jnp.zeros((Z_HEAD, D), F32)
        p_s[0:P_HEAD, :] = jnp.zeros((P_HEAD, D), F32)
        la_s[0:P_LO, :] = jnp.zeros((P_LO, D), F32)
        lb_s[0:P_LO, :] = jnp.zeros((P_LO, D - G_C), F32)
        lc_s[0:P_LO, :] = jnp.zeros((P_LO, D - 2 * G_C), F32)

    row = lax.broadcasted_iota(jnp.int32, (CHUNK, CHUNK), 0)
    col = lax.broadcasted_iota(jnp.int32, (CHUNK, CHUNK), 1)
    causal = col <= row
    w_spatial = [jnp.where(causal, wsp_ref[g], 0.0).astype(BF16) for g in range(N_GROUPS_A)]
    cw = cw_ref[...]
    psc = psc_ref[...]

    for r0 in range(0, tm, sub):
        rs = slice(r0, r0 + sub)
        xb_s[rs, :] = x_ref[rs, :].astype(BF16)

        def proj(k):
            return _dot(xb_s[rs, :], win_ref[:, k * D:(k + 1) * D])

        xc = proj(COL_XC)
        p0, p1 = P_HEAD + r0, P_HEAD + r0 + sub
        p_s[p0:p1, :] = xc
        lo = P_LO if r0 == 0 else p0
        la_s[lo:p1, :] = p_s[lo:p1, :] + p_s[lo - 1:p1 - 1, :]
        lb_s[lo:p1, :] = la_s[lo:p1, G_C:] + la_s[lo - 2:p1 - 2, G_C:]
        lc_s[lo:p1, :] = lb_s[lo:p1, G_C:] + lb_s[lo - 4:p1 - 4, G_C:]
        sum16 = lc_s[p0:p1, G_C:] + lc_s[p0 - 8:p1 - 8, G_C:]
        sums = (la_s[p0:p1, 0:G_C], lb_s[p0:p1, 0:G_C], lc_s[p0:p1, 0:G_C], sum16)
        pos1 = t * tm + r0 + lax.broadcasted_iota(jnp.int32, (sub, 1), 0) + 1
        for g, w in enumerate(POOL_WINDOWS):
            cs = slice(g * G_C, (g + 1) * G_C)
            inv_cnt = 1.0 / jnp.minimum(pos1, w).astype(F32)
            d = sums[g] * inv_cnt - xc[:, cs]
            hc_s[rs, cs] = (_dot(d.astype(BF16), wpool_ref[g]) * psc[:, cs]).astype(BF16)
        acc_s[rs, :] = jax.nn.sigmoid(proj(COL_GATE + 2)) * _dot(hc_s[rs, :], wpc_ref[...])

        vn_s[rs, :] = _layer_norm(proj(COL_V), lnvg_ref[...], lnvb_ref[...]).astype(BF16)
        u_s[rs, :] = proj(COL_U)
        for g in range(N_GROUPS_A):
            bias = bsp_ref[:, g:g + 1]
            cs = slice(g * G_A, (g + 1) * G_A)
            for c0 in range(r0, r0 + sub, CHUNK):
                ch = slice(c0, c0 + CHUNK)
                s = _dot(w_spatial[g], vn_s[ch, cs]) + bias
                ha_s[ch, cs] = (u_s[ch, cs] * s).astype(BF16)
        acc_s[rs, :] += jax.nn.sigmoid(proj(COL_GATE + 0)) * _dot(ha_s[rs, :], wpa_ref[...])

        z = proj(COL_CG) * proj(COL_XB)
        z0 = Z_HEAD + r0
        z_s[z0:z0 + sub, :] = z
        y = cb_ref[...] + (cw[0:1] * z_s[z0 - 2:z0 - 2 + sub, :]
                           + cw[1:2] * z_s[z0 - 1:z0 - 1 + sub, :]
                           + cw[2:3] * z)
        hb_s[rs, :] = (proj(COL_BG) * y).astype(BF16)
        acc_s[rs, :] += jax.nn.sigmoid(proj(COL_GATE + 1)) * _dot(hb_s[rs, :], wpb_ref[...])

        o = _dot(acc_s[rs, :].astype(BF16), wo_ref[...])
        h_ref[rs, :] = _layer_norm(ALPHA * x_ref[rs, :] + o, l1g_ref[...], l1b_ref[...])

    tail = z_s[tm:tm + Z_HEAD, :]
    nconv_ref[...] = tail
    z_s[0:Z_HEAD, :] = tail
    hist = p_s[tm + P_HEAD - 16:tm + P_HEAD, :]
    npool_ref[...] = hist
    p_s[P_HEAD - 16:P_HEAD, :] = hist

    wf1o_ref[...] = wf1_ref[...].astype(BF16)
    wf2o_ref[...] = wf2_ref[...].astype(BF16)


def _mixer_sample_kernel(x_ref, sconv_ref, spool_ref, win_ref, lnvg_ref, lnvb_ref, wsd_ref, bsd_ref, wpa_ref,
                         cw_ref, cb_ref, wpb_ref, wpool_ref, psc_ref, wpc_ref, wo_ref, l1g_ref, l1b_ref,
                         h_ref, z_ref, xc_ref, vn_ref):
    x = x_ref[...]
    xb = x.astype(BF16)

    def proj(k):
        return _dot(xb, win_ref[:, k * D:(k + 1) * D])

    vn = _layer_norm(proj(COL_V), lnvg_ref[...], lnvb_ref[...])
    vn_ref[...] = vn
    s = wsd_ref[...] * vn + bsd_ref[...]
    ha = (proj(COL_U) * s).astype(BF16)
    acc = jax.nn.sigmoid(proj(COL_GATE + 0)) * _dot(ha, wpa_ref[...])

    z = proj(COL_CG) * proj(COL_XB)
    z_ref[...] = z
    cw = cw_ref[...]
    y = cb_ref[...] + (cw[0:1] * sconv_ref[0] + cw[1:2] * sconv_ref[1] + cw[2:3] * z)
    hb = (proj(COL_BG) * y).astype(BF16)
    acc = acc + jax.nn.sigmoid(proj(COL_GATE + 1)) * _dot(hb, wpb_ref[...])

    xc = proj(COL_XC)
    xc_ref[...] = xc
    psc = psc_ref[...]
    hc = []
    for g, w in enumerate(POOL_WINDOWS):
        cs = slice(g * G_C, (g + 1) * G_C)
        tot = xc[:, cs]
        for j in range(w - 1):
            tot = tot + spool_ref[POOL_BUF - 1 - j, :, cs]
        d = tot * (1.0 / w) - xc[:, cs]
        hc.append((_dot(d.astype(BF16), wpool_ref[g]) * psc[:, cs]).astype(BF16))
    hc = jnp.concatenate(hc, axis=-1)
    acc = acc + jax.nn.sigmoid(proj(COL_GATE + 2)) * _dot(hc, wpc_ref[...])

    o = _dot(acc.astype(BF16), wo_ref[...])
    h_ref[...] = _layer_norm(ALPHA * x + o, l1g_ref[...], l1b_ref[...])


def _ffn_rows(h, hb, w1_ref, w2_ref, g_ref, b_ref):
    acc = None
    for j in range(D_FF // D):
        f = _dot(hb, w1_ref[:, j * D:(j + 1) * D])
        a = jnp.square(jnp.maximum(f, 0.0)).astype(BF16)
        c = _dot(a, w2_ref[j * D:(j + 1) * D, :])
        acc = c if acc is None else acc + c
    return _layer_norm(ALPHA * h + acc, g_ref[...], b_ref[...])


def _ffn_kernel(*refs, sub, n_cvt):
    h_ref, hs_ref, w1_ref, w2_ref, g_ref, b_ref = refs[:6]
    cvt_in = refs[6:6 + n_cvt]
    o_ref, os_ref = refs[6 + n_cvt:8 + n_cvt]
    cvt_out = refs[8 + n_cvt:8 + 2 * n_cvt]
    (hb_s,) = refs[8 + 2 * n_cvt:]

    for r in range(h_ref.shape[0] // sub):
        rs = slice(r * sub, (r + 1) * sub)
        hb_s[rs, :] = h_ref[rs, :].astype(BF16)
        o_ref[rs, :] = _ffn_rows(h_ref[rs, :], hb_s[rs, :], w1_ref, w2_ref, g_ref, b_ref)

    @pl.when(pl.program_id(0) == pl.num_programs(0) - 1)
    def _():
        hs = hs_ref[...]
        os_ref[...] = _ffn_rows(hs, hs.astype(BF16), w1_ref, w2_ref, g_ref, b_ref)

    for src, dst in zip(cvt_in, cvt_out):
        dst[...] = src[...].astype(BF16)


def _resident(shape):
    nd = len(shape)
    return pl.BlockSpec(shape, lambda *_: (0,) * nd, pipeline_mode=pl.Buffered(1))


def _layer_resident(a, l):
    return pl.BlockSpec((None,) + a.shape[1:], lambda *_: (l,) + (0,) * (a.ndim - 1), pipeline_mode=pl.Buffered(1))


MIXER_MATS = ('w_in', 'w_proj_a', 'w_proj_b', 'w_pool', 'w_proj_c', 'w_o')


def _mixer_prompt(x2d, batch, seq, tm, mats, p, l):
    nt = seq // tm
    nsteps = batch * nt
    row_block = pl.BlockSpec((tm, D), lambda b, t: (b * nt + t, 0))
    r1, r2 = D // nsteps, D_FF // nsteps
    lay = lambda k: (p[k], _layer_resident(p[k], l))
    mat = lambda k: (mats[k], _resident(mats[k].shape))
    operands = [(x2d, row_block), mat('w_in'), lay('lnv_g'), lay('lnv_b'), lay('w_spatial'), lay('b_spatial_t'),
                mat('w_proj_a'), lay('conv_w'), lay('conv_b'), mat('w_proj_b'), mat('w_pool'), lay('pool_scale'),
                mat('w_proj_c'), mat('w_o'), lay('ln1_g'), lay('ln1_b'),
                (p['w_ff1'], pl.BlockSpec((None, r1, D_FF), lambda b, t: (l, b * nt + t, 0))),
                (p['w_ff2'], pl.BlockSpec((None, r2, D), lambda b, t: (l, b * nt + t, 0)))]
    return pl.pallas_call(
        functools.partial(_mixer_prompt_kernel, tm=tm, sub=min(tm, SUB_MIXER)),
        grid=(batch, nt),
        in_specs=[spec for _, spec in operands],
        out_specs=[row_block,
                   pl.BlockSpec((None, Z_HEAD, D), lambda b, t: (b, 0, 0)),
                   pl.BlockSpec((None, 16, D), lambda b, t: (b, 0, 0)),
                   pl.BlockSpec((r1, D_FF), lambda b, t: (b * nt + t, 0)),
                   pl.BlockSpec((r2, D), lambda b, t: (b * nt + t, 0))],
        out_shape=[jax.ShapeDtypeStruct((batch * seq, D), F32),
                   jax.ShapeDtypeStruct((batch, Z_HEAD, D), F32),
                   jax.ShapeDtypeStruct((batch, 16, D), F32),
                   jax.ShapeDtypeStruct((D, D_FF), BF16),
                   jax.ShapeDtypeStruct((D_FF, D), BF16)],
        scratch_shapes=[pltpu.VMEM((tm, D), BF16),
                        pltpu.VMEM((tm, D), BF16),
                        pltpu.VMEM((tm, D), F32),
                        pltpu.VMEM((tm, D), BF16),
                        pltpu.VMEM((tm, D), BF16),
                        pltpu.VMEM((tm, D), BF16),
                        pltpu.VMEM((tm, D), F32),
                        pltpu.VMEM((Z_HEAD + tm, D), F32),
                        pltpu.VMEM((P_HEAD + tm, D), F32),
                        pltpu.VMEM((P_HEAD + tm, D), F32),
                        pltpu.VMEM((P_HEAD + tm, D - G_C), F32),
                        pltpu.VMEM((P_HEAD + tm, D - 2 * G_C), F32)],
        compiler_params=pltpu.CompilerParams(dimension_semantics=("arbitrary", "arbitrary"),
                                             vmem_limit_bytes=VMEM_LIMIT_MIXER),
        name="mixer_prompt",
    )(*[a for a, _ in operands])


def _mixer_sample(x2d, mats, p, l):
    n = x2d.shape[0]
    lay = lambda k: (p[k], _layer_resident(p[k], l))
    mat = lambda k: (mats[k], _resident(mats[k].shape))
    operands = [(x2d, _resident(x2d.shape)), lay('state_conv_t'), lay('state_pool_t'), mat('w_in'), lay('lnv_g'),
                lay('lnv_b'), lay('w_spatial_d'), lay('b_spatial_d'), mat('w_proj_a'), lay('conv_w'), lay('conv_b'),
                mat('w_proj_b'), mat('w_pool'), lay('pool_scale'), mat('w_proj_c'), mat('w_o'), lay('ln1_g'),
                lay('ln1_b')]
    return pl.pallas_call(
        _mixer_sample_kernel,
        grid=(1,),
        in_specs=[spec for _, spec in operands],
        out_specs=[pl.BlockSpec((n, D), lambda i: (0, 0))] * 4,
        out_shape=[jax.ShapeDtypeStruct((n, D), F32)] * 4,
        compiler_params=pltpu.CompilerParams(dimension_semantics=("arbitrary",),
                                             vmem_limit_bytes=VMEM_LIMIT_MIXER),
        name="mixer_sample",
    )(*[a for a, _ in operands])


def _ffn(h2d, hs2d, tm, w1, w2, p, l, convert_next):
    n = h2d.shape[0]
    nsteps = n // tm
    row_block = pl.BlockSpec((tm, D), lambda i: (i, 0))
    lay = lambda k: (p[k], _layer_resident(p[k], l))
    operands = [(h2d, row_block), (hs2d, _resident(hs2d.shape)), (w1, _resident(w1.shape)), (w2, _resident(w2.shape)),
                lay('ln2_g'), lay('ln2_b')]
    out_specs = [row_block, pl.BlockSpec(hs2d.shape, lambda i: (0, 0))]
    out_shape = [jax.ShapeDtypeStruct((n, D), F32), jax.ShapeDtypeStruct(hs2d.shape, F32)]
    n_cvt = len(MIXER_MATS) if convert_next else 0
    if convert_next:
        for k in MIXER_MATS:
            a = p[k]
            r = a.shape[-2] // nsteps
            lead = a.ndim - 3
            blk = a.shape[1:-2] + (r, a.shape[-1])
            operands.append((a, pl.BlockSpec((None,) + blk, lambda i, lead=lead: (l + 1,) + (0,) * lead + (i, 0))))
            out_specs.append(pl.BlockSpec(blk, lambda i, lead=lead: (0,) * lead + (i, 0)))
            out_shape.append(jax.ShapeDtypeStruct(a.shape[1:], BF16))
    outs = pl.pallas_call(
        functools.partial(_ffn_kernel, sub=min(tm, SUB_FFN), n_cvt=n_cvt),
        grid=(nsteps,),
        in_specs=[spec for _, spec in operands],
        out_specs=out_specs,
        out_shape=out_shape,
        scratch_shapes=[pltpu.VMEM((tm, D), BF16)],
        compiler_params=pltpu.CompilerParams(dimension_semantics=("arbitrary",),
                                             vmem_limit_bytes=VMEM_LIMIT_FFN),
        name="ffn",
    )(*[a for a, _ in operands])
    return outs[0], outs[1], dict(zip(MIXER_MATS, outs[2:]))


TM_MIXER = 512
TM_FFN = 1024
SUB_FFN = 256
SUB_MIXER = 256


def kernel(x_prompt, x_sample, state_conv, state_pool, w_in, lnv_g, lnv_b, w_spatial, b_spatial, w_proj_a, conv_w, conv_b, w_proj_b, w_pool, pool_scale, w_proj_c, w_o, ln1_g, ln1_b, w_ff1, w_ff2, ln2_g, ln2_b):
    bp, seq, _ = x_prompt.shape
    bs = x_sample.shape[0]
    xp = x_prompt.reshape(bp * seq, D)
    xs = x_sample.reshape(bs, D)
    conv_p, pool_p, conv_s, pool_s, chunk_v_s = [], [], [], [], []
    rows = lambda a: a[:, None, :]
    p = {
        'w_in': w_in, 'lnv_g': rows(lnv_g), 'lnv_b': rows(lnv_b),
        'w_spatial': w_spatial, 'b_spatial_t': jnp.swapaxes(b_spatial, 1, 2),
        'w_spatial_d': rows(jnp.repeat(w_spatial[:, :, 0, 0], G_A, axis=1)),
        'b_spatial_d': rows(jnp.repeat(b_spatial[:, :, 0], G_A, axis=1)),
        'w_proj_a': w_proj_a, 'conv_w': conv_w, 'conv_b': rows(conv_b),
        'w_proj_b': w_proj_b, 'w_pool': w_pool, 'pool_scale': rows(pool_scale),
        'w_proj_c': w_proj_c, 'w_o': w_o, 'ln1_g': rows(ln1_g), 'ln1_b': rows(ln1_b),
        'w_ff1': w_ff1, 'w_ff2': w_ff2, 'ln2_g': rows(ln2_g), 'ln2_b': rows(ln2_b),
        'state_conv_t': jnp.swapaxes(state_conv, 1, 2), 'state_pool_t': jnp.swapaxes(state_pool, 1, 2),
    }
    mats = {k: p[k][0].astype(BF16) for k in MIXER_MATS}
    for l in range(DEPTH):
        hp, nconv, npool, w1, w2 = _mixer_prompt(xp, bp, seq, TM_MIXER, mats, p, l)
        conv_p.append(nconv[:, Z_HEAD - (CONV_W - 1):, :])
        pool_p.append(npool[:, 16 - POOL_BUF:, :])
        hs, z_new, xc_new, vn = _mixer_sample(xs, mats, p, l)
        conv_s.append(jnp.concatenate([state_conv[l][:, 1:], z_new[:, None, :]], axis=1))
        pool_s.append(jnp.concatenate([state_pool[l][:, 1:], xc_new[:, None, :]], axis=1))
        chunk_v_s.append(vn[:, None, :])
        xp, xs, mats = _ffn(hp, hs, TM_FFN, w1, w2, p, l, convert_next=l + 1 < DEPTH)
    return (xp.reshape(bp, seq, D), xs.reshape(bs, 1, D), jnp.stack(conv_p), jnp.stack(pool_p),
            jnp.stack(conv_s), jnp.stack(pool_s), jnp.stack(chunk_v_s))
```

```python
import functools
import itertools

import jax
import jax.numpy as jnp
from jax import lax
from jax.experimental import pallas as pl
from jax.experimental.pallas import tpu as pltpu

D = 1024
N_GROUPS_A = 4
CHUNK = 128
G_A = D // N_GROUPS_A
CONV_W = 3
POOL_WINDOWS = (2, 4, 8, 16)
G_C = D // len(POOL_WINDOWS)
POOL_BUF = max(POOL_WINDOWS) - 1
D_FF = 4 * D
DEPTH = 2
ALPHA = float((2 * DEPTH) ** 0.25)
LN_EPS = 1e-5
PAST_LEN = 16384

COL_U, COL_V, COL_BG, COL_CG, COL_XB, COL_XC, COL_GATE = 0, 1, 2, 3, 4, 5, 6

SUBLANES = 8
Z_HEAD = SUBLANES
P_HEAD = 32
P_LO = 16

VMEM_LIMIT_MIXER = 60 * 1024 * 1024
VMEM_LIMIT_FFN = 48 * 1024 * 1024

F32 = jnp.float32
BF16 = jnp.bfloat16


def _dot(a, b):
    return jnp.dot(a, b, preferred_element_type=F32)


def _layer_norm(x, g, b):
    mu = jnp.mean(x, axis=-1, keepdims=True)
    xc = x - mu
    var = jnp.mean(xc * xc, axis=-1, keepdims=True)
    return xc * lax.rsqrt(var + LN_EPS) * g + b


def _mixer_prompt_kernel(x_ref, win_ref, lnvg_ref, lnvb_ref, wsp_ref, bsp_ref, wpa_ref, cw_ref, cb_ref,
                         wpb_ref, wpool_ref, psc_ref, wpc_ref, wo_ref, l1g_ref, l1b_ref, wf1_ref, wf2_ref,
                         xs_ref, sconv_ref, sp0_ref, sp1_ref, sp2_ref, sp3_ref, wsd_ref, bsd_ref,
                         h_ref, nconv_ref, npool_ref, wf1o_ref, wf2o_ref, hs_ref, zs_ref, xcs_ref, vns_ref,
                         xb_s, vn_s, u_s, ha_s, hb_s, hc_s, acc_s, z_s, p_s, la_s, lb_s, lc_s, *, tm, subs):
    t = pl.program_id(1)

    @pl.when(t == 0)
    def _():
        z_s[0:Z_HEAD, :] = jnp.zeros((Z_HEAD, D), F32)
        p_s[0:P_HEAD, :] = jnp.zeros((P_HEAD, D), F32)
        la_s[0:P_LO, :] = jnp.zeros((P_LO, D), F32)
        lb_s[0:P_LO, :] = jnp.zeros((P_LO, D - G_C), F32)
        lc_s[0:P_LO, :] = jnp.zeros((P_LO, D - 2 * G_C), F32)

    row = lax.broadcasted_iota(jnp.int32, (CHUNK, CHUNK), 0)
    col = lax.broadcasted_iota(jnp.int32, (CHUNK, CHUNK), 1)
    causal = col <= row
    w_spatial = [jnp.where(causal, wsp_ref[g], 0.0).astype(BF16) for g in range(N_GROUPS_A)]
    cw = cw_ref[...]
    psc = psc_ref[...]

    for r0, sub in zip(itertools.accumulate((0,) + subs), subs):
        rs = slice(r0, r0 + sub)
        xb_s[rs, :] = x_ref[rs, :].astype(BF16)

        def proj(k):
            return _dot(xb_s[rs, :], win_ref[:, k * D:(k + 1) * D])

        xc = proj(COL_XC)
        p0, p1 = P_HEAD + r0, P_HEAD + r0 + sub
        p_s[p0:p1, :] = xc
        lo = P_LO if r0 == 0 else p0
        la_s[lo:p1, :] = p_s[lo:p1, :] + p_s[lo - 1:p1 - 1, :]
        lb_s[lo:p1, :] = la_s[lo:p1, G_C:] + la_s[lo - 2:p1 - 2, G_C:]
        lc_s[lo:p1, :] = lb_s[lo:p1, G_C:] + lb_s[lo - 4:p1 - 4, G_C:]
        sum16 = lc_s[p0:p1, G_C:] + lc_s[p0 - 8:p1 - 8, G_C:]
        sums = (la_s[p0:p1, 0:G_C], lb_s[p0:p1, 0:G_C], lc_s[p0:p1, 0:G_C], sum16)
        pos1 = t * tm + r0 + lax.broadcasted_iota(jnp.int32, (sub, 1), 0) + 1
        for g, w in enumerate(POOL_WINDOWS):
            cs = slice(g * G_C, (g + 1) * G_C)
            inv_cnt = 1.0 / jnp.minimum(pos1, w).astype(F32)
            d = sums[g] * inv_cnt - xc[:, cs]
            hc_s[rs, cs] = (_dot(d.astype(BF16), wpool_ref[g]) * psc[:, cs]).astype(BF16)
        acc_s[rs, :] = jax.nn.sigmoid(proj(COL_GATE + 2)) * _dot(hc_s[rs, :], wpc_ref[...])

        vn_s[rs, :] = _layer_norm(proj(COL_V), lnvg_ref[...], lnvb_ref[...]).astype(BF16)
        u_s[rs, :] = proj(COL_U)
        for g in range(N_GROUPS_A):
            bias = bsp_ref[:, g:g + 1]
            cs = slice(g * G_A, (g + 1) * G_A)
            for c0 in range(r0, r0 + sub, CHUNK):
                ch = slice(c0, c0 + CHUNK)
                s = _dot(w_spatial[g], vn_s[ch, cs]) + bias
                ha_s[ch, cs] = (u_s[ch, cs] * s).astype(BF16)
        acc_s[rs, :] += jax.nn.sigmoid(proj(COL_GATE + 0)) * _dot(ha_s[rs, :], wpa_ref[...])

        z = proj(COL_CG) * proj(COL_XB)
        z0 = Z_HEAD + r0
        z_s[z0:z0 + sub, :] = z
        y = cb_ref[...] + (cw[0:1] * z_s[z0 - 2:z0 - 2 + sub, :]
                           + cw[1:2] * z_s[z0 - 1:z0 - 1 + sub, :]
                           + cw[2:3] * z)
        hb_s[rs, :] = (proj(COL_BG) * y).astype(BF16)
        acc_s[rs, :] += jax.nn.sigmoid(proj(COL_GATE + 1)) * _dot(hb_s[rs, :], wpb_ref[...])

        o = _dot(acc_s[rs, :].astype(BF16), wo_ref[...])
        h_ref[rs, :] = _layer_norm(ALPHA * x_ref[rs, :] + o, l1g_ref[...], l1b_ref[...])

    tail = z_s[tm:tm + Z_HEAD, :]
    nconv_ref[...] = tail
    z_s[0:Z_HEAD, :] = tail
    hist = p_s[tm + P_HEAD - 16:tm + P_HEAD, :]
    npool_ref[...] = hist
    p_s[P_HEAD - 16:P_HEAD, :] = hist

    wf1o_ref[...] = wf1_ref[...].astype(BF16)
    wf2o_ref[...] = wf2_ref[...].astype(BF16)

    @pl.when((pl.program_id(0) == pl.num_programs(0) - 1) & (t == pl.num_programs(1) - 1))
    def _():
        _mixer_sample_rows(xs_ref, sconv_ref, (sp0_ref, sp1_ref, sp2_ref, sp3_ref), win_ref, lnvg_ref, lnvb_ref,
                           wsd_ref, bsd_ref, wpa_ref, cw_ref, cb_ref, wpb_ref, wpool_ref, psc_ref, wpc_ref, wo_ref,
                           l1g_ref, l1b_ref, hs_ref, zs_ref, xcs_ref, vns_ref)


def _mixer_sample_rows(x_ref, sconv_ref, spool_refs, win_ref, lnvg_ref, lnvb_ref, wsd_ref, bsd_ref, wpa_ref,
                       cw_ref, cb_ref, wpb_ref, wpool_ref, psc_ref, wpc_ref, wo_ref, l1g_ref, l1b_ref,
                       h_ref, z_ref, xc_ref, vn_ref):
    x = x_ref[...]
    xb = x.astype(BF16)

    def proj(k):
        return _dot(xb, win_ref[:, k * D:(k + 1) * D])

    vn = _layer_norm(proj(COL_V), lnvg_ref[...], lnvb_ref[...])
    vn_ref[...] = vn
    s = wsd_ref[...] * vn + bsd_ref[...]
    ha = (proj(COL_U) * s).astype(BF16)
    acc = jax.nn.sigmoid(proj(COL_GATE + 0)) * _dot(ha, wpa_ref[...])

    z = proj(COL_CG) * proj(COL_XB)
    z_ref[...] = z
    cw = cw_ref[...]
    y = cb_ref[...] + (cw[0:1] * sconv_ref[0] + cw[1:2] * sconv_ref[1] + cw[2:3] * z)
    hb = (proj(COL_BG) * y).astype(BF16)
    acc = acc + jax.nn.sigmoid(proj(COL_GATE + 1)) * _dot(hb, wpb_ref[...])

    xc = proj(COL_XC)
    xc_ref[...] = xc
    psc = psc_ref[...]
    hc = []
    for g, w in enumerate(POOL_WINDOWS):
        cs = slice(g * G_C, (g + 1) * G_C)
        tot = xc[:, cs]
        for j in range(w - 1):
            tot = tot + spool_refs[g][w - 2 - j]
        d = tot * (1.0 / w) - xc[:, cs]
        hc.append((_dot(d.astype(BF16), wpool_ref[g]) * psc[:, cs]).astype(BF16))
    hc = jnp.concatenate(hc, axis=-1)
    acc = acc + jax.nn.sigmoid(proj(COL_GATE + 2)) * _dot(hc, wpc_ref[...])

    o = _dot(acc.astype(BF16), wo_ref[...])
    h_ref[...] = _layer_norm(ALPHA * x + o, l1g_ref[...], l1b_ref[...])


def _ffn_rows(h, hb, w1_ref, w2_ref, g_ref, b_ref):
    acc = None
    for j in range(D_FF // D):
        f = _dot(hb, w1_ref[:, j * D:(j + 1) * D])
        a = jnp.square(jnp.maximum(f, 0.0)).astype(BF16)
        c = _dot(a, w2_ref[j * D:(j + 1) * D, :])
        acc = c if acc is None else acc + c
    return _layer_norm(ALPHA * h + acc, g_ref[...], b_ref[...])


def _ffn_kernel(*refs, subs, n_cvt):
    h_ref, hs_ref, w1_ref, w2_ref, g_ref, b_ref = refs[:6]
    cvt_in = refs[6:6 + n_cvt]
    o_ref, os_ref = refs[6 + n_cvt:8 + n_cvt]
    cvt_out = refs[8 + n_cvt:8 + 2 * n_cvt]
    (hb_s,) = refs[8 + 2 * n_cvt:]

    for r0, sub in zip(itertools.accumulate((0,) + subs), subs):
        rs = slice(r0, r0 + sub)
        hb_s[rs, :] = h_ref[rs, :].astype(BF16)
        o_ref[rs, :] = _ffn_rows(h_ref[rs, :], hb_s[rs, :], w1_ref, w2_ref, g_ref, b_ref)

    @pl.when(pl.program_id(0) == pl.num_programs(0) - 1)
    def _():
        hs = hs_ref[...]
        os_ref[...] = _ffn_rows(hs, hs.astype(BF16), w1_ref, w2_ref, g_ref, b_ref)

    for src, dst in zip(cvt_in, cvt_out):
        dst[...] = src[...].astype(BF16)


def _resident(shape):
    nd = len(shape)
    return pl.BlockSpec(shape, lambda *_: (0,) * nd, pipeline_mode=pl.Buffered(1))


def _layer_resident(a, l):
    return pl.BlockSpec((None,) + a.shape[1:], lambda *_: (l,) + (0,) * (a.ndim - 1), pipeline_mode=pl.Buffered(1))


MIXER_MATS = ('w_in', 'w_proj_a', 'w_proj_b', 'w_pool', 'w_proj_c', 'w_o')


def _mixer(x2d, xs2d, batch, seq, tm, mats, p, l):
    nt = seq // tm
    nsteps = batch * nt
    ns = xs2d.shape[0]
    row_block = pl.BlockSpec((tm, D), lambda b, t: (b * nt + t, 0))
    sample_block = pl.BlockSpec((ns, D), lambda b, t: (0, 0), pipeline_mode=pl.Buffered(1))
    r1, r2 = D // nsteps, D_FF // nsteps
    lay = lambda k: (p[k], _layer_resident(p[k], l))
    mat = lambda k: (mats[k], _resident(mats[k].shape))
    operands = [(x2d, row_block), mat('w_in'), lay('lnv_g'), lay('lnv_b'), lay('w_spatial'), lay('b_spatial_t'),
                mat('w_proj_a'), lay('conv_w'), lay('conv_b'), mat('w_proj_b'), mat('w_pool'), lay('pool_scale'),
                mat('w_proj_c'), mat('w_o'), lay('ln1_g'), lay('ln1_b'),
                (p['w_ff1'], pl.BlockSpec((None, r1, D_FF), lambda b, t: (l, b * nt + t, 0))),
                (p['w_ff2'], pl.BlockSpec((None, r2, D), lambda b, t: (l, b * nt + t, 0))),
                (xs2d, sample_block), lay('state_conv_t'), lay('state_pool_t0'), lay('state_pool_t1'),
                lay('state_pool_t2'), lay('state_pool_t3'), lay('w_spatial_d'), lay('b_spatial_d')]
    return pl.pallas_call(
        functools.partial(_mixer_prompt_kernel, tm=tm, subs=SUBS_MIXER),
        grid=(batch, nt),
        in_specs=[spec for _, spec in operands],
        out_specs=[row_block,
                   pl.BlockSpec((None, Z_HEAD, D), lambda b, t: (b, 0, 0)),
                   pl.BlockSpec((None, 16, D), lambda b, t: (b, 0, 0)),
                   pl.BlockSpec((r1, D_FF), lambda b, t: (b * nt + t, 0)),
                   pl.BlockSpec((r2, D), lambda b, t: (b * nt + t, 0))] + [sample_block] * 4,
        out_shape=[jax.ShapeDtypeStruct((batch * seq, D), F32),
                   jax.ShapeDtypeStruct((batch, Z_HEAD, D), F32),
                   jax.ShapeDtypeStruct((batch, 16, D), F32),
                   jax.ShapeDtypeStruct((D, D_FF), BF16),
                   jax.ShapeDtypeStruct((D_FF, D), BF16)] + [jax.ShapeDtypeStruct((ns, D), F32)] * 4,
        scratch_shapes=[pltpu.VMEM((tm, D), BF16),
                        pltpu.VMEM((tm, D), BF16),
                        pltpu.VMEM((tm, D), F32),
                        pltpu.VMEM((tm, D), BF16),
                        pltpu.VMEM((tm, D), BF16),
                        pltpu.VMEM((tm, D), BF16),
                        pltpu.VMEM((tm, D), F32),
                        pltpu.VMEM((Z_HEAD + tm, D), F32),
                        pltpu.VMEM((P_HEAD + tm, D), F32),
                        pltpu.VMEM((P_HEAD + tm, D), F32),
                        pltpu.VMEM((P_HEAD + tm, D - G_C), F32),
                        pltpu.VMEM((P_HEAD + tm, D - 2 * G_C), F32)],
        compiler_params=pltpu.CompilerParams(dimension_semantics=("arbitrary", "arbitrary"),
                                             vmem_limit_bytes=VMEM_LIMIT_MIXER),
        name="mixer",
    )(*[a for a, _ in operands])


def _ffn(h2d, hs2d, tm, w1, w2, p, l, convert_next):
    n = h2d.shape[0]
    nsteps = n // tm
    row_block = pl.BlockSpec((tm, D), lambda i: (i, 0))
    lay = lambda k: (p[k], _layer_resident(p[k], l))
    operands = [(h2d, row_block), (hs2d, _resident(hs2d.shape)), (w1, _resident(w1.shape)), (w2, _resident(w2.shape)),
                lay('ln2_g'), lay('ln2_b')]
    out_specs = [row_block, pl.BlockSpec(hs2d.shape, lambda i: (0, 0))]
    out_shape = [jax.ShapeDtypeStruct((n, D), F32), jax.ShapeDtypeStruct(hs2d.shape, F32)]
    n_cvt = len(MIXER_MATS) if convert_next else 0
    if convert_next:
        for k in MIXER_MATS:
            a = p[k]
            r = a.shape[-2] // nsteps
            lead = a.ndim - 3
            blk = a.shape[1:-2] + (r, a.shape[-1])
            operands.append((a, pl.BlockSpec((None,) + blk, lambda i, lead=lead: (l + 1,) + (0,) * lead + (i, 0))))
            out_specs.append(pl.BlockSpec(blk, lambda i, lead=lead: (0,) * lead + (i, 0)))
            out_shape.append(jax.ShapeDtypeStruct(a.shape[1:], BF16))
    outs = pl.pallas_call(
        functools.partial(_ffn_kernel, subs=SUBS_FFN, n_cvt=n_cvt),
        grid=(nsteps,),
        in_specs=[spec for _, spec in operands],
        out_specs=out_specs,
        out_shape=out_shape,
        scratch_shapes=[pltpu.VMEM((tm, D), BF16)],
        compiler_params=pltpu.CompilerParams(dimension_semantics=("arbitrary",),
                                             vmem_limit_bytes=VMEM_LIMIT_FFN),
        name="ffn",
    )(*[a for a, _ in operands])
    return outs[0], outs[1], dict(zip(MIXER_MATS, outs[2:]))


TM_MIXER = 512
TM_FFN = 1024
SUBS_MIXER = (256, 256)
SUBS_FFN = (256, 256, 256, 256)
assert sum(SUBS_MIXER) == TM_MIXER and sum(SUBS_FFN) == TM_FFN


def kernel(x_prompt, x_sample, state_conv, state_pool, w_in, lnv_g, lnv_b, w_spatial, b_spatial, w_proj_a, conv_w, conv_b, w_proj_b, w_pool, pool_scale, w_proj_c, w_o, ln1_g, ln1_b, w_ff1, w_ff2, ln2_g, ln2_b):
    bp, seq, _ = x_prompt.shape
    bs = x_sample.shape[0]
    xp = x_prompt.reshape(bp * seq, D)
    xs = x_sample.reshape(bs, D)
    conv_p, pool_p, conv_s, pool_s, chunk_v_s = [], [], [], [], []
    rows = lambda a: a[:, None, :]
    p = {
        'w_in': w_in, 'lnv_g': rows(lnv_g), 'lnv_b': rows(lnv_b),
        'w_spatial': w_spatial, 'b_spatial_t': jnp.swapaxes(b_spatial, 1, 2),
        'w_spatial_d': rows(jnp.repeat(w_spatial[:, :, 0, 0], G_A, axis=1)),
        'b_spatial_d': rows(jnp.repeat(b_spatial[:, :, 0], G_A, axis=1)),
        'w_proj_a': w_proj_a, 'conv_w': conv_w, 'conv_b': rows(conv_b),
        'w_proj_b': w_proj_b, 'w_pool': w_pool, 'pool_scale': rows(pool_scale),
        'w_proj_c': w_proj_c, 'w_o': w_o, 'ln1_g': rows(ln1_g), 'ln1_b': rows(ln1_b),
        'w_ff1': w_ff1, 'w_ff2': w_ff2, 'ln2_g': rows(ln2_g), 'ln2_b': rows(ln2_b),
        'state_conv_t': jnp.swapaxes(state_conv, 1, 2),
    }
    for g, w in enumerate(POOL_WINDOWS):
        p['state_pool_t%d' % g] = jnp.swapaxes(state_pool[:, :, POOL_BUF - (w - 1):, g * G_C:(g + 1) * G_C], 1, 2)
    mats = {k: p[k][0].astype(BF16) for k in MIXER_MATS}
    for l in range(DEPTH):
        hp, nconv, npool, w1, w2, hs, z_new, xc_new, vn = _mixer(xp, xs, bp, seq, TM_MIXER, mats, p, l)
        conv_p.append(nconv[:, Z_HEAD - (CONV_W - 1):, :])
        pool_p.append(npool[:, 16 - POOL_BUF:, :])
        conv_s.append(jnp.concatenate([state_conv[l][:, 1:], z_new[:, None, :]], axis=1))
        pool_s.append(jnp.concatenate([state_pool[l][:, 1:], xc_new[:, None, :]], axis=1))
        chunk_v_s.append(vn[:, None, :])
        xp, xs, mats = _ffn(hp, hs, TM_FFN, w1, w2, p, l, convert_next=l + 1 < DEPTH)
    return (xp.reshape(bp, seq, D), xs.reshape(bs, 1, D), jnp.stack(conv_p), jnp.stack(pool_p),
            jnp.stack(conv_s), jnp.stack(pool_s), jnp.stack(chunk_v_s))
```

```python
import functools

import jax
import jax.numpy as jnp
from jax import lax
from jax.experimental import pallas as pl
from jax.experimental.pallas import tpu as pltpu

D = 1024
N_GROUPS_A = 4
CHUNK = 128
G_A = D // N_GROUPS_A
CONV_W = 3
POOL_WINDOWS = (2, 4, 8, 16)
G_C = D // len(POOL_WINDOWS)
POOL_BUF = max(POOL_WINDOWS) - 1
D_FF = 4 * D
DEPTH = 2
ALPHA = float((2 * DEPTH) ** 0.25)
LN_EPS = 1e-5
PAST_LEN = 16384

COL_U, COL_V, COL_BG, COL_CG, COL_XB, COL_XC, COL_GATE = 0, 1, 2, 3, 4, 5, 6
N_COLS = 9

SUBLANES = 8
Z_HEAD = SUBLANES
P_HEAD = 32
P_LO = 16

VMEM_LIMIT_MIXER = 56 * 1024 * 1024
VMEM_LIMIT_FFN = 48 * 1024 * 1024

F32 = jnp.float32
BF16 = jnp.bfloat16


def _dot(a, b):
    return jnp.dot(a, b, preferred_element_type=F32)


def _layer_norm(x, g, b):
    mu = jnp.mean(x, axis=-1, keepdims=True)
    xc = x - mu
    var = jnp.mean(xc * xc, axis=-1, keepdims=True)
    return xc * lax.rsqrt(var + LN_EPS) * g + b


def _mixer_prompt_kernel(x_ref, win_ref, lnvg_ref, lnvb_ref, wsp_ref, bsp_ref, wpa_ref, cw_ref, cb_ref,
                         wpb_ref, wpool_ref, psc_ref, wpc_ref, wo_ref, l1g_ref, l1b_ref, wf1_ref, wf2_ref,
                         h_ref, nconv_ref, npool_ref, wf1o_ref, wf2o_ref,
                         xb_s, vn_s, u_s, ha_s, hb_s, hc_s, acc_s, z_s, p_s, la_s, lb_s, lc_s, *, tm, sub):
    t = pl.program_id(1)

    @pl.when(t == 0)
    def _():
        z_s[0:Z_HEAD, :] = jnp.zeros((Z_HEAD, D), F32)
        p_s[0:P_HEAD, :] = jnp.zeros((P_HEAD, D), F32)
        la_s[0:P_LO, :] = jnp.zeros((P_LO, D), F32)
        lb_s[0:P_LO, :] = jnp.zeros((P_LO, D - G_C), F32)
        lc_s[0:P_LO, :] = jnp.zeros((P_LO, D - 2 * G_C), F32)

    row = lax.broadcasted_iota(jnp.int32, (CHUNK, CHUNK), 0)
    col = lax.broadcasted_iota(jnp.int32, (CHUNK, CHUNK), 1)
    causal = col <= row
    w_spatial = [jnp.where(causal, wsp_ref[g], 0.0).astype(BF16) for g in range(N_GROUPS_A)]
    cw = cw_ref[...]
    psc = psc_ref[...]

    for r0 in range(0, tm, sub):
        rs = slice(r0, r0 + sub)
        xb_s[rs, :] = x_ref[rs, :].astype(BF16)

        def proj(k):
            return _dot(xb_s[rs, :], win_ref[:, k * D:(k + 1) * D])

        xc = proj(COL_XC)
        p0, p1 = P_HEAD + r0, P_HEAD + r0 + sub
        p_s[p0:p1, :] = xc
        lo = P_LO if r0 == 0 else p0
        la_s[lo:p1, :] = p_s[lo:p1, :] + p_s[lo - 1:p1 - 1, :]
        lb_s[lo:p1, :] = la_s[lo:p1, G_C:] + la_s[lo - 2:p1 - 2, G_C:]
        lc_s[lo:p1, :] = lb_s[lo:p1, G_C:] + lb_s[lo - 4:p1 - 4, G_C:]
        sum16 = lc_s[p0:p1, G_C:] + lc_s[p0 - 8:p1 - 8, G_C:]
        sums = (la_s[p0:p1, 0:G_C], lb_s[p0:p1, 0:G_C], lc_s[p0:p1, 0:G_C], sum16)
        pos1 = t * tm + r0 + lax.broadcasted_iota(jnp.int32, (sub, 1), 0) + 1
        for g, w in enumerate(POOL_WINDOWS):
            cs = slice(g * G_C, (g + 1) * G_C)
            inv_cnt = 1.0 / jnp.minimum(pos1, w).astype(F32)
            d = sums[g] * inv_cnt - xc[:, cs]
            hc_s[rs, cs] = (_dot(d.astype(BF16), wpool_ref[g]) * psc[:, cs]).astype(BF16)
        acc_s[rs, :] = jax.nn.sigmoid(proj(COL_GATE + 2)) * _dot(hc_s[rs, :], wpc_ref[...])

        vn_s[rs, :] = _layer_norm(proj(COL_V), lnvg_ref[...], lnvb_ref[...]).astype(BF16)
        u_s[rs, :] = proj(COL_U)
        for g in range(N_GROUPS_A):
            bias = bsp_ref[:, g:g + 1]
            cs = slice(g * G_A, (g + 1) * G_A)
            for c0 in range(r0, r0 + sub, CHUNK):
                ch = slice(c0, c0 + CHUNK)
                s = _dot(w_spatial[g], vn_s[ch, cs]) + bias
                ha_s[ch, cs] = (u_s[ch, cs] * s).astype(BF16)
        acc_s[rs, :] += jax.nn.sigmoid(proj(COL_GATE + 0)) * _dot(ha_s[rs, :], wpa_ref[...])

        z = proj(COL_CG) * proj(COL_XB)
        z0 = Z_HEAD + r0
        z_s[z0:z0 + sub, :] = z
        y = cb_ref[...] + (cw[0:1] * z_s[z0 - 2:z0 - 2 + sub, :]
                           + cw[1:2] * z_s[z0 - 1:z0 - 1 + sub, :]
                           + cw[2:3] * z)
        hb_s[rs, :] = (proj(COL_BG) * y).astype(BF16)
        acc_s[rs, :] += jax.nn.sigmoid(proj(COL_GATE + 1)) * _dot(hb_s[rs, :], wpb_ref[...])

        o = _dot(acc_s[rs, :].astype(BF16), wo_ref[...])
        h_ref[rs, :] = _layer_norm(ALPHA * x_ref[rs, :] + o, l1g_ref[...], l1b_ref[...])

    tail = z_s[tm:tm + Z_HEAD, :]
    nconv_ref[...] = tail
    z_s[0:Z_HEAD, :] = tail
    hist = p_s[tm + P_HEAD - 16:tm + P_HEAD, :]
    npool_ref[...] = hist
    p_s[P_HEAD - 16:P_HEAD, :] = hist

    wf1o_ref[...] = wf1_ref[...].astype(BF16)
    wf2o_ref[...] = wf2_ref[...].astype(BF16)


def _sample_mixer_kernel(x_ref, sconv_ref, sp0_ref, sp1_ref, sp2_ref, sp3_ref, win_ref, lnvg_ref, lnvb_ref, wsd_ref,
                         bsd_ref, wpa_ref, cw_ref, cb_ref, wpb_ref, wpool_ref, psc_ref, wpc_ref, wo_ref, l1g_ref,
                         l1b_ref,
                         h_ref, z_ref, xc_ref, vn_ref, wino_ref, wpao_ref, wpbo_ref, wpoolo_ref, wpco_ref, woo_ref,
                         xb_s, proj_s):
    k = pl.program_id(0)

    @pl.when(k == 0)
    def _():
        xb_s[...] = x_ref[...].astype(BF16)

    @pl.when(k < N_COLS)
    def _():
        w = win_ref[...].astype(BF16)
        wino_ref[...] = w
        proj_s[k] = _dot(xb_s[...], w)

    @pl.when(k == N_COLS)
    def _():
        x = x_ref[...]
        wpa = wpa_ref[...].astype(BF16)
        wpb = wpb_ref[...].astype(BF16)
        wpc = wpc_ref[...].astype(BF16)
        wo = wo_ref[...].astype(BF16)
        wpool = wpool_ref[...].astype(BF16)
        wpao_ref[...] = wpa
        wpbo_ref[...] = wpb
        wpco_ref[...] = wpc
        woo_ref[...] = wo
        wpoolo_ref[...] = wpool

        vn = _layer_norm(proj_s[COL_V], lnvg_ref[...], lnvb_ref[...])
        vn_ref[...] = vn
        s = wsd_ref[...] * vn + bsd_ref[...]
        ha = (proj_s[COL_U] * s).astype(BF16)
        acc = jax.nn.sigmoid(proj_s[COL_GATE + 0]) * _dot(ha, wpa)

        z = proj_s[COL_CG] * proj_s[COL_XB]
        z_ref[...] = z
        cw = cw_ref[...]
        y = cb_ref[...] + (cw[0:1] * sconv_ref[0] + cw[1:2] * sconv_ref[1] + cw[2:3] * z)
        hb = (proj_s[COL_BG] * y).astype(BF16)
        acc = acc + jax.nn.sigmoid(proj_s[COL_GATE + 1]) * _dot(hb, wpb)

        xc = proj_s[COL_XC]
        xc_ref[...] = xc
        psc = psc_ref[...]
        hc = []
        for g, (w, sp_ref) in enumerate(zip(POOL_WINDOWS, (sp0_ref, sp1_ref, sp2_ref, sp3_ref))):
            cs = slice(g * G_C, (g + 1) * G_C)
            tot = xc[:, cs]
            for j in range(w - 1):
                tot = tot + sp_ref[w - 2 - j]
            d = tot * (1.0 / w) - xc[:, cs]
            hc.append((_dot(d.astype(BF16), wpool[g]) * psc[:, cs]).astype(BF16))
        hc = jnp.concatenate(hc, axis=-1)
        acc = acc + jax.nn.sigmoid(proj_s[COL_GATE + 2]) * _dot(hc, wpc)

        o = _dot(acc.astype(BF16), wo)
        h_ref[...] = _layer_norm(ALPHA * x + o, l1g_ref[...], l1b_ref[...])


def _ffn_rows(h, hb, w1_ref, w2_ref, g_ref, b_ref):
    acc = None
    for j in range(D_FF // D):
        f = _dot(hb, w1_ref[:, j * D:(j + 1) * D])
        a = jnp.square(jnp.maximum(f, 0.0)).astype(BF16)
        c = _dot(a, w2_ref[j * D:(j + 1) * D, :])
        acc = c if acc is None else acc + c
    return _layer_norm(ALPHA * h + acc, g_ref[...], b_ref[...])


def _ffn_kernel(h_ref, hs_ref, w1_ref, w2_ref, g_ref, b_ref, o_ref, os_ref, hb_s, *, sub):
    for r0 in range(0, h_ref.shape[0], sub):
        rs = slice(r0, r0 + sub)
        hb_s[rs, :] = h_ref[rs, :].astype(BF16)
        o_ref[rs, :] = _ffn_rows(h_ref[rs, :], hb_s[rs, :], w1_ref, w2_ref, g_ref, b_ref)

    @pl.when(pl.program_id(0) == pl.num_programs(0) - 1)
    def _():
        hs = hs_ref[...]
        os_ref[...] = _ffn_rows(hs, hs.astype(BF16), w1_ref, w2_ref, g_ref, b_ref)


def _resident(shape):
    nd = len(shape)
    return pl.BlockSpec(shape, lambda *_: (0,) * nd, pipeline_mode=pl.Buffered(1))


def _layer_resident(a, l):
    return pl.BlockSpec((None,) + a.shape[1:], lambda *_: (l,) + (0,) * (a.ndim - 1), pipeline_mode=pl.Buffered(1))


MIXER_MATS = ('w_in', 'w_proj_a', 'w_proj_b', 'w_pool', 'w_proj_c', 'w_o')


def _mixer_prompt(x2d, batch, seq, tm, mats, p, l):
    nt = seq // tm
    nsteps = batch * nt
    row_block = pl.BlockSpec((tm, D), lambda b, t: (b * nt + t, 0))
    r1, r2 = D // nsteps, D_FF // nsteps
    lay = lambda k: (p[k], _layer_resident(p[k], l))
    mat = lambda k: (mats[k], _resident(mats[k].shape))
    operands = [(x2d, row_block), mat('w_in'), lay('lnv_g'), lay('lnv_b'), lay('w_spatial'), lay('b_spatial_t'),
                mat('w_proj_a'), lay('conv_w'), lay('conv_b'), mat('w_proj_b'), mat('w_pool'), lay('pool_scale'),
                mat('w_proj_c'), mat('w_o'), lay('ln1_g'), lay('ln1_b'),
                (p['w_ff1'], pl.BlockSpec((None, r1, D_FF), lambda b, t: (l, b * nt + t, 0))),
                (p['w_ff2'], pl.BlockSpec((None, r2, D), lambda b, t: (l, b * nt + t, 0)))]
    return pl.pallas_call(
        functools.partial(_mixer_prompt_kernel, tm=tm, sub=min(tm, SUB_MIXER)),
        grid=(batch, nt),
        in_specs=[spec for _, spec in operands],
        out_specs=[row_block,
                   pl.BlockSpec((None, Z_HEAD, D), lambda b, t: (b, 0, 0)),
                   pl.BlockSpec((None, 16, D), lambda b, t: (b, 0, 0)),
                   pl.BlockSpec((r1, D_FF), lambda b, t: (b * nt + t, 0)),
                   pl.BlockSpec((r2, D), lambda b, t: (b * nt + t, 0))],
        out_shape=[jax.ShapeDtypeStruct((batch * seq, D), F32),
                   jax.ShapeDtypeStruct((batch, Z_HEAD, D), F32),
                   jax.ShapeDtypeStruct((batch, 16, D), F32),
                   jax.ShapeDtypeStruct((D, D_FF), BF16),
                   jax.ShapeDtypeStruct((D_FF, D), BF16)],
        scratch_shapes=[pltpu.VMEM((tm, D), BF16),
                        pltpu.VMEM((tm, D), BF16),
                        pltpu.VMEM((tm, D), F32),
                        pltpu.VMEM((tm, D), BF16),
                        pltpu.VMEM((tm, D), BF16),
                        pltpu.VMEM((tm, D), BF16),
                        pltpu.VMEM((tm, D), F32),
                        pltpu.VMEM((Z_HEAD + tm, D), F32),
                        pltpu.VMEM((P_HEAD + tm, D), F32),
                        pltpu.VMEM((P_HEAD + tm, D), F32),
                        pltpu.VMEM((P_HEAD + tm, D - G_C), F32),
                        pltpu.VMEM((P_HEAD + tm, D - 2 * G_C), F32)],
        compiler_params=pltpu.CompilerParams(dimension_semantics=("arbitrary", "arbitrary"),
                                             vmem_limit_bytes=VMEM_LIMIT_MIXER),
        name="mixer_prompt",
    )(*[a for a, _ in operands])


def _sample_mixer(x2d, p, l):
    n = x2d.shape[0]
    lay = lambda k: (p[k], _layer_resident(p[k], l))
    col_block = lambda k: (0, jnp.minimum(k, N_COLS - 1))
    operands = [(x2d, _resident(x2d.shape)), lay('state_conv_t'), lay('state_pool_t0'), lay('state_pool_t1'),
                lay('state_pool_t2'), lay('state_pool_t3'),
                (p['w_in'], pl.BlockSpec((None, D, D), lambda k: (l,) + col_block(k))),
                lay('lnv_g'), lay('lnv_b'), lay('w_spatial_d'), lay('b_spatial_d'), lay('w_proj_a'), lay('conv_w'),
                lay('conv_b'), lay('w_proj_b'), lay('w_pool'), lay('pool_scale'), lay('w_proj_c'), lay('w_o'),
                lay('ln1_g'), lay('ln1_b')]
    once = lambda shape: pl.BlockSpec(shape, lambda k: (0,) * len(shape), pipeline_mode=pl.Buffered(1))
    outs = pl.pallas_call(
        _sample_mixer_kernel,
        grid=(N_COLS + 1,),
        in_specs=[spec for _, spec in operands],
        out_specs=[once((n, D))] * 4 + [pl.BlockSpec((D, D), col_block), once((D, D)), once((D, D)),
                                        once(p['w_pool'].shape[1:]), once((D, D)), once((D, D))],
        out_shape=[jax.ShapeDtypeStruct((n, D), F32)] * 4
                  + [jax.ShapeDtypeStruct(p[k].shape[1:], BF16) for k in MIXER_MATS],
        scratch_shapes=[pltpu.VMEM((n, D), BF16),
                        pltpu.VMEM((N_COLS, n, D), F32)],
        compiler_params=pltpu.CompilerParams(dimension_semantics=("arbitrary",),
                                             vmem_limit_bytes=VMEM_LIMIT_MIXER),
        name="sample_mixer",
    )(*[a for a, _ in operands])
    return outs[:4], dict(zip(MIXER_MATS, outs[4:]))


def _ffn(h2d, hs2d, tm, w1, w2, p, l):
    n = h2d.shape[0]
    row_block = pl.BlockSpec((tm, D), lambda i: (i, 0))
    lay = lambda k: (p[k], _layer_resident(p[k], l))
    operands = [(h2d, row_block), (hs2d, _resident(hs2d.shape)), (w1, _resident(w1.shape)), (w2, _resident(w2.shape)),
                lay('ln2_g'), lay('ln2_b')]
    return pl.pallas_call(
        functools.partial(_ffn_kernel, sub=min(tm, SUB_FFN)),
        grid=(n // tm,),
        in_specs=[spec for _, spec in operands],
        out_specs=[row_block, pl.BlockSpec(hs2d.shape, lambda i: (0, 0))],
        out_shape=[jax.ShapeDtypeStruct((n, D), F32), jax.ShapeDtypeStruct(hs2d.shape, F32)],
        scratch_shapes=[pltpu.VMEM((tm, D), BF16)],
        compiler_params=pltpu.CompilerParams(dimension_semantics=("arbitrary",),
                                             vmem_limit_bytes=VMEM_LIMIT_FFN),
        name="ffn",
    )(*[a for a, _ in operands])


TM_MIXER = 512
TM_FFN = 1024
SUB_FFN = 256
SUB_MIXER = 256


def kernel(x_prompt, x_sample, state_conv, state_pool, w_in, lnv_g, lnv_b, w_spatial, b_spatial, w_proj_a, conv_w, conv_b, w_proj_b, w_pool, pool_scale, w_proj_c, w_o, ln1_g, ln1_b, w_ff1, w_ff2, ln2_g, ln2_b):
    bp, seq, _ = x_prompt.shape
    bs = x_sample.shape[0]
    xp = x_prompt.reshape(bp * seq, D)
    xs = x_sample.reshape(bs, D)
    conv_p, pool_p, conv_s, pool_s, chunk_v_s = [], [], [], [], []
    rows = lambda a: a[:, None, :]
    p = {
        'w_in': w_in, 'lnv_g': rows(lnv_g), 'lnv_b': rows(lnv_b),
        'w_spatial': w_spatial, 'b_spatial_t': jnp.swapaxes(b_spatial, 1, 2),
        'w_spatial_d': rows(jnp.repeat(w_spatial[:, :, 0, 0], G_A, axis=1)),
        'b_spatial_d': rows(jnp.repeat(b_spatial[:, :, 0], G_A, axis=1)),
        'w_proj_a': w_proj_a, 'conv_w': conv_w, 'conv_b': rows(conv_b),
        'w_proj_b': w_proj_b, 'w_pool': w_pool, 'pool_scale': rows(pool_scale),
        'w_proj_c': w_proj_c, 'w_o': w_o, 'ln1_g': rows(ln1_g), 'ln1_b': rows(ln1_b),
        'w_ff1': w_ff1, 'w_ff2': w_ff2, 'ln2_g': rows(ln2_g), 'ln2_b': rows(ln2_b),
        'state_conv_t': jnp.swapaxes(state_conv, 1, 2),
    }
    for g, w in enumerate(POOL_WINDOWS):
        p['state_pool_t%d' % g] = jnp.swapaxes(state_pool[:, :, POOL_BUF - (w - 1):, g * G_C:(g + 1) * G_C], 1, 2)
    for l in range(DEPTH):
        (hs, z_new, xc_new, vn), mats = _sample_mixer(xs, p, l)
        conv_s.append(jnp.concatenate([state_conv[l][:, 1:], z_new[:, None, :]], axis=1))
        pool_s.append(jnp.concatenate([state_pool[l][:, 1:], xc_new[:, None, :]], axis=1))
        chunk_v_s.append(vn[:, None, :])
        hp, nconv, npool, w1, w2 = _mixer_prompt(xp, bp, seq, TM_MIXER, mats, p, l)
        conv_p.append(nconv[:, Z_HEAD - (CONV_W - 1):, :])
        pool_p.append(npool[:, 16 - POOL_BUF:, :])
        xp, xs = _ffn(hp, hs, TM_FFN, w1, w2, p, l)
    return (xp.reshape(bp, seq, D), xs.reshape(bs, 1, D), jnp.stack(conv_p), jnp.stack(pool_p),
            jnp.stack(conv_s), jnp.stack(pool_s), jnp.stack(chunk_v_s))
```

```python
import functools

import jax
import jax.numpy as jnp
from jax import lax
from jax.experimental import pallas as pl
from jax.experimental.pallas import tpu as pltpu

D = 1024
N_GROUPS_A = 4
CHUNK = 128
G_A = D // N_GROUPS_A
CONV_W = 3
POOL_WINDOWS = (2, 4, 8, 16)
G_C = D // len(POOL_WINDOWS)
POOL_BUF = max(POOL_WINDOWS) - 1
D_FF = 4 * D
DEPTH = 2
ALPHA = float((2 * DEPTH) ** 0.25)
LN_EPS = 1e-5
PAST_LEN = 16384

COL_U, COL_V, COL_BG, COL_CG, COL_XB, COL_XC, COL_GATE = 0, 1, 2, 3, 4, 5, 6
N_COLS = 9

SUBLANES = 8
LANES = 128
N_LT = D // LANES
LT_G = G_C // LANES
Z_HEAD = SUBLANES
P_HEAD = 32
P_LO = 16

VMEM_LIMIT_MIXER = 56 * 1024 * 1024
VMEM_LIMIT_FFN = 48 * 1024 * 1024

F32 = jnp.float32
BF16 = jnp.bfloat16


def _dot(a, b):
    return jnp.dot(a, b, preferred_element_type=F32)


def _layer_norm(x, g, b):
    mu = jnp.mean(x, axis=-1, keepdims=True)
    xc = x - mu
    var = jnp.mean(xc * xc, axis=-1, keepdims=True)
    return xc * lax.rsqrt(var + LN_EPS) * g + b


def _mixer_prompt_kernel(x_ref, win_ref, lnvg_ref, lnvb_ref, wsp_ref, bsp_ref, wpa_ref, cw_ref, cb_ref,
                         wpb_ref, wpool_ref, psc_ref, wpc_ref, wo_ref, l1g_ref, l1b_ref, wf1_ref, wf2_ref,
                         h_ref, nconv_ref, npool_ref, wf1o_ref, wf2o_ref,
                         xb_s, vn_s, u_s, ha_s, hb_s, hc_s, acc_s, z_s, p_s, la_s, lb_s, lc_s, *, tm, sub):
    t = pl.program_id(1)

    @pl.when(t == 0)
    def _():
        z_s[:, 0:Z_HEAD, :] = jnp.zeros((N_LT, Z_HEAD, LANES), F32)
        p_s[:, 0:P_HEAD, :] = jnp.zeros((N_LT, P_HEAD, LANES), F32)
        la_s[:, 0:P_LO, :] = jnp.zeros((N_LT, P_LO, LANES), F32)
        lb_s[:, 0:P_LO, :] = jnp.zeros((N_LT - LT_G, P_LO, LANES), F32)
        lc_s[:, 0:P_LO, :] = jnp.zeros((N_LT - 2 * LT_G, P_LO, LANES), F32)

    row = lax.broadcasted_iota(jnp.int32, (CHUNK, CHUNK), 0)
    col = lax.broadcasted_iota(jnp.int32, (CHUNK, CHUNK), 1)
    causal = col <= row
    w_spatial = [jnp.where(causal, wsp_ref[g], 0.0).astype(BF16) for g in range(N_GROUPS_A)]
    cw = cw_ref[...]
    psc = psc_ref[...]

    for r0 in range(0, tm, sub):
        rs = slice(r0, r0 + sub)
        xb_s[rs, :] = x_ref[rs, :].astype(BF16)

        def proj(k):
            return _dot(xb_s[rs, :], win_ref[:, k * D:(k + 1) * D])

        xc = proj(COL_XC)
        p0, p1 = P_HEAD + r0, P_HEAD + r0 + sub
        for j in range(N_LT):
            p_s[j, p0:p1, :] = xc[:, j * LANES:(j + 1) * LANES]
        lo = P_LO if r0 == 0 else p0
        for j in range(N_LT):
            la_s[j, lo:p1, :] = p_s[j, lo:p1, :] + p_s[j, lo - 1:p1 - 1, :]
        for j in range(N_LT - LT_G):
            lb_s[j, lo:p1, :] = la_s[j + LT_G, lo:p1, :] + la_s[j + LT_G, lo - 2:p1 - 2, :]
        for j in range(N_LT - 2 * LT_G):
            lc_s[j, lo:p1, :] = lb_s[j + LT_G, lo:p1, :] + lb_s[j + LT_G, lo - 4:p1 - 4, :]
        sum16 = [lc_s[j + LT_G, p0:p1, :] + lc_s[j + LT_G, p0 - 8:p1 - 8, :] for j in range(LT_G)]
        sums = tuple(jnp.concatenate(tiles, axis=-1) for tiles in (
            [la_s[j, p0:p1, :] for j in range(LT_G)], [lb_s[j, p0:p1, :] for j in range(LT_G)],
            [lc_s[j, p0:p1, :] for j in range(LT_G)], sum16))
        pos1 = t * tm + r0 + lax.broadcasted_iota(jnp.int32, (sub, 1), 0) + 1
        for g, w in enumerate(POOL_WINDOWS):
            cs = slice(g * G_C, (g + 1) * G_C)
            inv_cnt = 1.0 / jnp.minimum(pos1, w).astype(F32)
            d = sums[g] * inv_cnt - xc[:, cs]
            hc_s[rs, cs] = (_dot(d.astype(BF16), wpool_ref[g]) * psc[:, cs]).astype(BF16)
        acc_s[rs, :] = jax.nn.sigmoid(proj(COL_GATE + 2)) * _dot(hc_s[rs, :], wpc_ref[...])

        vn_s[rs, :] = _layer_norm(proj(COL_V), lnvg_ref[...], lnvb_ref[...]).astype(BF16)
        u_s[rs, :] = proj(COL_U)
        for g in range(N_GROUPS_A):
            bias = bsp_ref[:, g:g + 1]
            cs = slice(g * G_A, (g + 1) * G_A)
            for c0 in range(r0, r0 + sub, CHUNK):
                ch = slice(c0, c0 + CHUNK)
                s = _dot(w_spatial[g], vn_s[ch, cs]) + bias
                ha_s[ch, cs] = (u_s[ch, cs] * s).astype(BF16)
        acc_s[rs, :] += jax.nn.sigmoid(proj(COL_GATE + 0)) * _dot(ha_s[rs, :], wpa_ref[...])

        z = proj(COL_CG) * proj(COL_XB)
        z0 = Z_HEAD + r0
        for j in range(N_LT):
            z_s[j, z0:z0 + sub, :] = z[:, j * LANES:(j + 1) * LANES]
        z_m2 = jnp.concatenate([z_s[j, z0 - 2:z0 - 2 + sub, :] for j in range(N_LT)], axis=-1)
        z_m1 = jnp.concatenate([z_s[j, z0 - 1:z0 - 1 + sub, :] for j in range(N_LT)], axis=-1)
        y = cb_ref[...] + (cw[0:1] * z_m2 + cw[1:2] * z_m1 + cw[2:3] * z)
        hb_s[rs, :] = (proj(COL_BG) * y).astype(BF16)
        acc_s[rs, :] += jax.nn.sigmoid(proj(COL_GATE + 1)) * _dot(hb_s[rs, :], wpb_ref[...])

        o = _dot(acc_s[rs, :].astype(BF16), wo_ref[...])
        h_ref[rs, :] = _layer_norm(ALPHA * x_ref[rs, :] + o, l1g_ref[...], l1b_ref[...])

    tail = z_s[:, tm:tm + Z_HEAD, :]
    z_s[:, 0:Z_HEAD, :] = tail
    hist = p_s[:, tm + P_HEAD - 16:tm + P_HEAD, :]
    p_s[:, P_HEAD - 16:P_HEAD, :] = hist
    for j in range(N_LT):
        nconv_ref[:, j * LANES:(j + 1) * LANES] = tail[j]
        npool_ref[:, j * LANES:(j + 1) * LANES] = hist[j]

    wf1o_ref[...] = wf1_ref[...].astype(BF16)
    wf2o_ref[...] = wf2_ref[...].astype(BF16)


def _sample_mixer_kernel(x_ref, sconv_ref, sp0_ref, sp1_ref, sp2_ref, sp3_ref, win_ref, lnvg_ref, lnvb_ref, wsd_ref,
                         bsd_ref, wpa_ref, cw_ref, cb_ref, wpb_ref, wpool_ref, psc_ref, wpc_ref, wo_ref, l1g_ref,
                         l1b_ref,
                         h_ref, z_ref, xc_ref, vn_ref, wino_ref, wpao_ref, wpbo_ref, wpoolo_ref, wpco_ref, woo_ref,
                         xb_s, proj_s):
    k = pl.program_id(0)

    @pl.when(k == 0)
    def _():
        xb_s[...] = x_ref[...].astype(BF16)

    @pl.when(k < N_COLS)
    def _():
        w = win_ref[...].astype(BF16)
        wino_ref[...] = w
        proj_s[k] = _dot(xb_s[...], w)

    @pl.when(k == N_COLS)
    def _():
        x = x_ref[...]
        wpa = wpa_ref[...].astype(BF16)
        wpb = wpb_ref[...].astype(BF16)
        wpc = wpc_ref[...].astype(BF16)
        wo = wo_ref[...].astype(BF16)
        wpool = wpool_ref[...].astype(BF16)
        wpao_ref[...] = wpa
        wpbo_ref[...] = wpb
        wpco_ref[...] = wpc
        woo_ref[...] = wo
        wpoolo_ref[...] = wpool

        vn = _layer_norm(proj_s[COL_V], lnvg_ref[...], lnvb_ref[...])
        vn_ref[...] = vn
        s = wsd_ref[...] * vn + bsd_ref[...]
        ha = (proj_s[COL_U] * s).astype(BF16)
        acc = jax.nn.sigmoid(proj_s[COL_GATE + 0]) * _dot(ha, wpa)

        z = proj_s[COL_CG] * proj_s[COL_XB]
        z_ref[...] = z
        cw = cw_ref[...]
        y = cb_ref[...] + (cw[0:1] * sconv_ref[0] + cw[1:2] * sconv_ref[1] + cw[2:3] * z)
        hb = (proj_s[COL_BG] * y).astype(BF16)
        acc = acc + jax.nn.sigmoid(proj_s[COL_GATE + 1]) * _dot(hb, wpb)

        xc = proj_s[COL_XC]
        xc_ref[...] = xc
        psc = psc_ref[...]
        hc = []
        for g, (w, sp_ref) in enumerate(zip(POOL_WINDOWS, (sp0_ref, sp1_ref, sp2_ref, sp3_ref))):
            cs = slice(g * G_C, (g + 1) * G_C)
            tot = xc[:, cs]
            for j in range(w - 1):
                tot = tot + sp_ref[w - 2 - j]
            d = tot * (1.0 / w) - xc[:, cs]
            hc.append((_dot(d.astype(BF16), wpool[g]) * psc[:, cs]).astype(BF16))
        hc = jnp.concatenate(hc, axis=-1)
        acc = acc + jax.nn.sigmoid(proj_s[COL_GATE + 2]) * _dot(hc, wpc)

        o = _dot(acc.astype(BF16), wo)
        h_ref[...] = _layer_norm(ALPHA * x + o, l1g_ref[...], l1b_ref[...])


def _ffn_rows(h, hb, w1_ref, w2_ref, g_ref, b_ref):
    acc = None
    for j in range(D_FF // D):
        f = _dot(hb, w1_ref[:, j * D:(j + 1) * D])
        a = jnp.square(jnp.maximum(f, 0.0)).astype(BF16)
        c = _dot(a, w2_ref[j * D:(j + 1) * D, :])
        acc = c if acc is None else acc + c
    return _layer_norm(ALPHA * h + acc, g_ref[...], b_ref[...])


def _ffn_kernel(h_ref, hs_ref, w1_ref, w2_ref, g_ref, b_ref, o_ref, os_ref, hb_s, *, sub):
    for r0 in range(0, h_ref.shape[0], sub):
        rs = slice(r0, r0 + sub)
        hb_s[rs, :] = h_ref[rs, :].astype(BF16)
        o_ref[rs, :] = _ffn_rows(h_ref[rs, :], hb_s[rs, :], w1_ref, w2_ref, g_ref, b_ref)

    @pl.when(pl.program_id(0) == pl.num_programs(0) - 1)
    def _():
        hs = hs_ref[...]
        os_ref[...] = _ffn_rows(hs, hs.astype(BF16), w1_ref, w2_ref, g_ref, b_ref)


def _resident(shape):
    nd = len(shape)
    return pl.BlockSpec(shape, lambda *_: (0,) * nd, pipeline_mode=pl.Buffered(1))


def _layer_resident(a, l):
    return pl.BlockSpec((None,) + a.shape[1:], lambda *_: (l,) + (0,) * (a.ndim - 1), pipeline_mode=pl.Buffered(1))


MIXER_MATS = ('w_in', 'w_proj_a', 'w_proj_b', 'w_pool', 'w_proj_c', 'w_o')


def _mixer_prompt(x2d, batch, seq, tm, mats, p, l):
    nt = seq // tm
    nsteps = batch * nt
    row_block = pl.BlockSpec((tm, D), lambda b, t: (b * nt + t, 0))
    r1, r2 = D // nsteps, D_FF // nsteps
    lay = lambda k: (p[k], _layer_resident(p[k], l))
    mat = lambda k: (mats[k], _resident(mats[k].shape))
    operands = [(x2d, row_block), mat('w_in'), lay('lnv_g'), lay('lnv_b'), lay('w_spatial'), lay('b_spatial_t'),
                mat('w_proj_a'), lay('conv_w'), lay('conv_b'), mat('w_proj_b'), mat('w_pool'), lay('pool_scale'),
                mat('w_proj_c'), mat('w_o'), lay('ln1_g'), lay('ln1_b'),
                (p['w_ff1'], pl.BlockSpec((None, r1, D_FF), lambda b, t: (l, b * nt + t, 0))),
                (p['w_ff2'], pl.BlockSpec((None, r2, D), lambda b, t: (l, b * nt + t, 0)))]
    return pl.pallas_call(
        functools.partial(_mixer_prompt_kernel, tm=tm, sub=min(tm, SUB_MIXER)),
        grid=(batch, nt),
        in_specs=[spec for _, spec in operands],
        out_specs=[row_block,
                   pl.BlockSpec((None, Z_HEAD, D), lambda b, t: (b, 0, 0)),
                   pl.BlockSpec((None, 16, D), lambda b, t: (b, 0, 0)),
                   pl.BlockSpec((r1, D_FF), lambda b, t: (b * nt + t, 0)),
                   pl.BlockSpec((r2, D), lambda b, t: (b * nt + t, 0))],
        out_shape=[jax.ShapeDtypeStruct((batch * seq, D), F32),
                   jax.ShapeDtypeStruct((batch, Z_HEAD, D), F32),
                   jax.ShapeDtypeStruct((batch, 16, D), F32),
                   jax.ShapeDtypeStruct((D, D_FF), BF16),
                   jax.ShapeDtypeStruct((D_FF, D), BF16)],
        scratch_shapes=[pltpu.VMEM((tm, D), BF16),
                        pltpu.VMEM((tm, D), BF16),
                        pltpu.VMEM((tm, D), F32),
                        pltpu.VMEM((tm, D), BF16),
                        pltpu.VMEM((tm, D), BF16),
                        pltpu.VMEM((tm, D), BF16),
                        pltpu.VMEM((tm, D), F32),
                        pltpu.VMEM((N_LT, Z_HEAD + tm, LANES), F32),
                        pltpu.VMEM((N_LT, P_HEAD + tm, LANES), F32),
                        pltpu.VMEM((N_LT, P_HEAD + tm, LANES), F32),
                        pltpu.VMEM((N_LT - LT_G, P_HEAD + tm, LANES), F32),
                        pltpu.VMEM((N_LT - 2 * LT_G, P_HEAD + tm, LANES), F32)],
        compiler_params=pltpu.CompilerParams(dimension_semantics=("arbitrary", "arbitrary"),
                                             vmem_limit_bytes=VMEM_LIMIT_MIXER),
        name="mixer_prompt",
    )(*[a for a, _ in operands])


def _sample_mixer(x2d, p, l):
    n = x2d.shape[0]
    lay = lambda k: (p[k], _layer_resident(p[k], l))
    col_block = lambda k: (0, jnp.minimum(k, N_COLS - 1))
    operands = [(x2d, _resident(x2d.shape)), lay('state_conv_t'), lay('state_pool_t0'), lay('state_pool_t1'),
                lay('state_pool_t2'), lay('state_pool_t3'),
                (p['w_in'], pl.BlockSpec((None, D, D), lambda k: (l,) + col_block(k))),
                lay('lnv_g'), lay('lnv_b'), lay('w_spatial_d'), lay('b_spatial_d'), lay('w_proj_a'), lay('conv_w'),
                lay('conv_b'), lay('w_proj_b'), lay('w_pool'), lay('pool_scale'), lay('w_proj_c'), lay('w_o'),
                lay('ln1_g'), lay('ln1_b')]
    once = lambda shape: pl.BlockSpec(shape, lambda k: (0,) * len(shape), pipeline_mode=pl.Buffered(1))
    outs = pl.pallas_call(
        _sample_mixer_kernel,
        grid=(N_COLS + 1,),
        in_specs=[spec for _, spec in operands],
        out_specs=[once((n, D))] * 4 + [pl.BlockSpec((D, D), col_block), once((D, D)), once((D, D)),
                                        once(p['w_pool'].shape[1:]), once((D, D)), once((D, D))],
        out_shape=[jax.ShapeDtypeStruct((n, D), F32)] * 4
                  + [jax.ShapeDtypeStruct(p[k].shape[1:], BF16) for k in MIXER_MATS],
        scratch_shapes=[pltpu.VMEM((n, D), BF16),
                        pltpu.VMEM((N_COLS, n, D), F32)],
        compiler_params=pltpu.CompilerParams(dimension_semantics=("arbitrary",),
                                             vmem_limit_bytes=VMEM_LIMIT_MIXER),
        name="sample_mixer",
    )(*[a for a, _ in operands])
    return outs[:4], dict(zip(MIXER_MATS, outs[4:]))


def _ffn(h2d, hs2d, tm, w1, w2, p, l):
    n = h2d.shape[0]
    row_block = pl.BlockSpec((tm, D), lambda i: (i, 0))
    lay = lambda k: (p[k], _layer_resident(p[k], l))
    operands = [(h2d, row_block), (hs2d, _resident(hs2d.shape)), (w1, _resident(w1.shape)), (w2, _resident(w2.shape)),
                lay('ln2_g'), lay('ln2_b')]
    return pl.pallas_call(
        functools.partial(_ffn_kernel, sub=min(tm, SUB_FFN)),
        grid=(n // tm,),
        in_specs=[spec for _, spec in operands],
        out_specs=[row_block, pl.BlockSpec(hs2d.shape, lambda i: (0, 0))],
        out_shape=[jax.ShapeDtypeStruct((n, D), F32), jax.ShapeDtypeStruct(hs2d.shape, F32)],
        scratch_shapes=[pltpu.VMEM((tm, D), BF16)],
        compiler_params=pltpu.CompilerParams(dimension_semantics=("arbitrary",),
                                             vmem_limit_bytes=VMEM_LIMIT_FFN),
        name="ffn",
    )(*[a for a, _ in operands])


TM_MIXER = 512
TM_FFN = 1024
SUB_FFN = 256
SUB_MIXER = 256


def kernel(x_prompt, x_sample, state_conv, state_pool, w_in, lnv_g, lnv_b, w_spatial, b_spatial, w_proj_a, conv_w, conv_b, w_proj_b, w_pool, pool_scale, w_proj_c, w_o, ln1_g, ln1_b, w_ff1, w_ff2, ln2_g, ln2_b):
    bp, seq, _ = x_prompt.shape
    bs = x_sample.shape[0]
    xp = x_prompt.reshape(bp * seq, D)
    xs = x_sample.reshape(bs, D)
    conv_p, pool_p, conv_s, pool_s, chunk_v_s = [], [], [], [], []
    rows = lambda a: a[:, None, :]
    p = {
        'w_in': w_in, 'lnv_g': rows(lnv_g), 'lnv_b': rows(lnv_b),
        'w_spatial': w_spatial, 'b_spatial_t': jnp.swapaxes(b_spatial, 1, 2),
        'w_spatial_d': rows(jnp.repeat(w_spatial[:, :, 0, 0], G_A, axis=1)),
        'b_spatial_d': rows(jnp.repeat(b_spatial[:, :, 0], G_A, axis=1)),
        'w_proj_a': w_proj_a, 'conv_w': conv_w, 'conv_b': rows(conv_b),
        'w_proj_b': w_proj_b, 'w_pool': w_pool, 'pool_scale': rows(pool_scale),
        'w_proj_c': w_proj_c, 'w_o': w_o, 'ln1_g': rows(ln1_g), 'ln1_b': rows(ln1_b),
        'w_ff1': w_ff1, 'w_ff2': w_ff2, 'ln2_g': rows(ln2_g), 'ln2_b': rows(ln2_b),
        'state_conv_t': jnp.swapaxes(state_conv, 1, 2),
    }
    for g, w in enumerate(POOL_WINDOWS):
        p['state_pool_t%d' % g] = jnp.swapaxes(state_pool[:, :, POOL_BUF - (w - 1):, g * G_C:(g + 1) * G_C], 1, 2)
    for l in range(DEPTH):
        (hs, z_new, xc_new, vn), mats = _sample_mixer(xs, p, l)
        conv_s.append(jnp.concatenate([state_conv[l][:, 1:], z_new[:, None, :]], axis=1))
        pool_s.append(jnp.concatenate([state_pool[l][:, 1:], xc_new[:, None, :]], axis=1))
        chunk_v_s.append(vn[:, None, :])
        hp, nconv, npool, w1, w2 = _mixer_prompt(xp, bp, seq, TM_MIXER, mats, p, l)
        conv_p.append(nconv[:, Z_HEAD - (CONV_W - 1):, :])
        pool_p.append(npool[:, 16 - POOL_BUF:, :])
        xp, xs = _ffn(hp, hs, TM_FFN, w1, w2, p, l)
    return (xp.reshape(bp, seq, D), xs.reshape(bs, 1, D), jnp.stack(conv_p), jnp.stack(pool_p),
            jnp.stack(conv_s), jnp.stack(pool_s), jnp.stack(chunk_v_s))
```

```python
import functools

import jax
import jax.numpy as jnp
from jax import lax
from jax.experimental import pallas as pl
from jax.experimental.pallas import tpu as pltpu

D = 1024
N_GROUPS_A = 4
CHUNK = 128
G_A = D // N_GROUPS_A
CONV_W = 3
POOL_WINDOWS = (2, 4, 8, 16)
G_C = D // len(POOL_WINDOWS)
POOL_BUF = max(POOL_WINDOWS) - 1
D_FF = 4 * D
DEPTH = 2
ALPHA = float((2 * DEPTH) ** 0.25)
LN_EPS = 1e-5
PAST_LEN = 16384

COL_U, COL_V, COL_BG, COL_CG, COL_XB, COL_XC, COL_GATE = 0, 1, 2, 3, 4, 5, 6
N_COLS = 9
N_ROW_PIECES = 8
ROW_PIECE = D // N_ROW_PIECES

SUBLANES = 8
LANES = 128
N_LT = D // LANES
LT_G = G_C // LANES
Z_HEAD = SUBLANES
P_HEAD = 32
P_LO = 16

VMEM_LIMIT_MIXER = 56 * 1024 * 1024
VMEM_LIMIT_FFN = 48 * 1024 * 1024

F32 = jnp.float32
BF16 = jnp.bfloat16


def _dot(a, b):
    return jnp.dot(a, b, preferred_element_type=F32)


def _layer_norm(x, g, b):
    mu = jnp.mean(x, axis=-1, keepdims=True)
    xc = x - mu
    var = jnp.mean(xc * xc, axis=-1, keepdims=True)
    return xc * lax.rsqrt(var + LN_EPS) * g + b


def _mixer_prompt_kernel(x_ref, win_ref, lnvg_ref, lnvb_ref, wsp_ref, bsp_ref, wpa_ref, cw_ref, cb_ref,
                         wpb_ref, wpool_ref, psc_ref, wpc_ref, wo_ref, l1g_ref, l1b_ref, wf1_ref, wf2_ref,
                         h_ref, nconv_ref, npool_ref, wf1o_ref, wf2o_ref,
                         xb_s, vn_s, u_s, ha_s, hb_s, hc_s, acc_s, z_s, p_s, la_s, lb_s, lc_s, *, tm, sub):
    t = pl.program_id(1)

    @pl.when(t == 0)
    def _():
        z_s[:, 0:Z_HEAD, :] = jnp.zeros((N_LT, Z_HEAD, LANES), F32)
        p_s[:, 0:P_HEAD, :] = jnp.zeros((N_LT, P_HEAD, LANES), F32)
        la_s[:, 0:P_LO, :] = jnp.zeros((N_LT, P_LO, LANES), F32)
        lb_s[:, 0:P_LO, :] = jnp.zeros((N_LT - LT_G, P_LO, LANES), F32)
        lc_s[:, 0:P_LO, :] = jnp.zeros((N_LT - 2 * LT_G, P_LO, LANES), F32)

    row = lax.broadcasted_iota(jnp.int32, (CHUNK, CHUNK), 0)
    col = lax.broadcasted_iota(jnp.int32, (CHUNK, CHUNK), 1)
    causal = col <= row
    w_spatial = [jnp.where(causal, wsp_ref[g], 0.0).astype(BF16) for g in range(N_GROUPS_A)]
    cw = cw_ref[...]
    psc = psc_ref[...]

    for r0 in range(0, tm, sub):
        rs = slice(r0, r0 + sub)
        xb_s[rs, :] = x_ref[rs, :].astype(BF16)

        def proj(k):
            return _dot(xb_s[rs, :], win_ref[:, k * D:(k + 1) * D])

        xc = proj(COL_XC)
        p0, p1 = P_HEAD + r0, P_HEAD + r0 + sub
        for j in range(N_LT):
            p_s[j, p0:p1, :] = xc[:, j * LANES:(j + 1) * LANES]
        lo = P_LO if r0 == 0 else p0
        for j in range(N_LT):
            la_s[j, lo:p1, :] = p_s[j, lo:p1, :] + p_s[j, lo - 1:p1 - 1, :]
        for j in range(N_LT - LT_G):
            lb_s[j, lo:p1, :] = la_s[j + LT_G, lo:p1, :] + la_s[j + LT_G, lo - 2:p1 - 2, :]
        for j in range(N_LT - 2 * LT_G):
            lc_s[j, lo:p1, :] = lb_s[j + LT_G, lo:p1, :] + lb_s[j + LT_G, lo - 4:p1 - 4, :]
        sum16 = [lc_s[j + LT_G, p0:p1, :] + lc_s[j + LT_G, p0 - 8:p1 - 8, :] for j in range(LT_G)]
        sums = tuple(jnp.concatenate(tiles, axis=-1) for tiles in (
            [la_s[j, p0:p1, :] for j in range(LT_G)], [lb_s[j, p0:p1, :] for j in range(LT_G)],
            [lc_s[j, p0:p1, :] for j in range(LT_G)], sum16))
        pos1 = t * tm + r0 + lax.broadcasted_iota(jnp.int32, (sub, 1), 0) + 1
        for g, w in enumerate(POOL_WINDOWS):
            cs = slice(g * G_C, (g + 1) * G_C)
            inv_cnt = 1.0 / jnp.minimum(pos1, w).astype(F32)
            d = sums[g] * inv_cnt - xc[:, cs]
            hc_s[rs, cs] = (_dot(d.astype(BF16), wpool_ref[g]) * psc[:, cs]).astype(BF16)
        acc_s[rs, :] = jax.nn.sigmoid(proj(COL_GATE + 2)) * _dot(hc_s[rs, :], wpc_ref[...])

        vn_s[rs, :] = _layer_norm(proj(COL_V), lnvg_ref[...], lnvb_ref[...]).astype(BF16)
        u_s[rs, :] = proj(COL_U)
        for g in range(N_GROUPS_A):
            bias = bsp_ref[:, g:g + 1]
            cs = slice(g * G_A, (g + 1) * G_A)
            for c0 in range(r0, r0 + sub, CHUNK):
                ch = slice(c0, c0 + CHUNK)
                s = _dot(w_spatial[g], vn_s[ch, cs]) + bias
                ha_s[ch, cs] = (u_s[ch, cs] * s).astype(BF16)
        acc_s[rs, :] += jax.nn.sigmoid(proj(COL_GATE + 0)) * _dot(ha_s[rs, :], wpa_ref[...])

        z = proj(COL_CG) * proj(COL_XB)
        z0 = Z_HEAD + r0
        for j in range(N_LT):
            z_s[j, z0:z0 + sub, :] = z[:, j * LANES:(j + 1) * LANES]
        z_m2 = jnp.concatenate([z_s[j, z0 - 2:z0 - 2 + sub, :] for j in range(N_LT)], axis=-1)
        z_m1 = jnp.concatenate([z_s[j, z0 - 1:z0 - 1 + sub, :] for j in range(N_LT)], axis=-1)
        y = cb_ref[...] + (cw[0:1] * z_m2 + cw[1:2] * z_m1 + cw[2:3] * z)
        hb_s[rs, :] = (proj(COL_BG) * y).astype(BF16)
        acc_s[rs, :] += jax.nn.sigmoid(proj(COL_GATE + 1)) * _dot(hb_s[rs, :], wpb_ref[...])

        o = _dot(acc_s[rs, :].astype(BF16), wo_ref[...])
        h_ref[rs, :] = _layer_norm(ALPHA * x_ref[rs, :] + o, l1g_ref[...], l1b_ref[...])

    tail = z_s[:, tm:tm + Z_HEAD, :]
    z_s[:, 0:Z_HEAD, :] = tail
    hist = p_s[:, tm + P_HEAD - 16:tm + P_HEAD, :]
    p_s[:, P_HEAD - 16:P_HEAD, :] = hist
    for j in range(N_LT):
        nconv_ref[:, j * LANES:(j + 1) * LANES] = z_s[j, tm + Z_HEAD - (CONV_W - 1):tm + Z_HEAD, :]
        npool_ref[:, j * LANES:(j + 1) * LANES] = p_s[j, tm + P_HEAD - POOL_BUF:tm + P_HEAD, :]

    wf1o_ref[...] = wf1_ref[...].astype(BF16)
    wf2o_ref[...] = wf2_ref[...].astype(BF16)


def _sample_mixer_kernel(x_ref, sconv_ref, sp0_ref, sp1_ref, sp2_ref, sp3_ref, win_ref, lnvg_ref, lnvb_ref, wsd_ref,
                         bsd_ref, wpa_ref, cw_ref, cb_ref, wpb_ref, wpool_ref, psc_ref, wpc_ref, wo_ref, l1g_ref,
                         l1b_ref,
                         h_ref, z_ref, xc_ref, vn_ref, wino_ref, wpao_ref, wpbo_ref, wpoolo_ref, wpco_ref, woo_ref,
                         xb_s, proj_s, wpa_s, wpb_s, wpc_s, wo_s):
    k = pl.program_id(0)

    @pl.when(k == 0)
    def _():
        xb_s[...] = x_ref[...].astype(BF16)

    @pl.when(k < N_COLS)
    def _():
        w = win_ref[...].astype(BF16)
        wino_ref[...] = w
        proj_s[k] = _dot(xb_s[...], w)

    @pl.when(k < N_ROW_PIECES)
    def _():
        rows = pl.ds(pl.multiple_of(k * ROW_PIECE, ROW_PIECE), ROW_PIECE)
        for src, dst, keep in ((wpa_ref, wpao_ref, wpa_s), (wpb_ref, wpbo_ref, wpb_s), (wpc_ref, wpco_ref, wpc_s),
                               (wo_ref, woo_ref, wo_s)):
            piece = src[...].astype(BF16)
            dst[...] = piece
            keep[rows, :] = piece

    @pl.when(k == N_COLS)
    def _():
        x = x_ref[...]
        wpa, wpb, wpc, wo = wpa_s[...], wpb_s[...], wpc_s[...], wo_s[...]
        wpool = wpool_ref[...].astype(BF16)
        wpoolo_ref[...] = wpool

        vn = _layer_norm(proj_s[COL_V], lnvg_ref[...], lnvb_ref[...])
        vn_ref[...] = vn
        s = wsd_ref[...] * vn + bsd_ref[...]
        ha = (proj_s[COL_U] * s).astype(BF16)
        acc = jax.nn.sigmoid(proj_s[COL_GATE + 0]) * _dot(ha, wpa)

        z = proj_s[COL_CG] * proj_s[COL_XB]
        z_ref[...] = z
        cw = cw_ref[...]
        y = cb_ref[...] + (cw[0:1] * sconv_ref[0] + cw[1:2] * sconv_ref[1] + cw[2:3] * z)
        hb = (proj_s[COL_BG] * y).astype(BF16)
        acc = acc + jax.nn.sigmoid(proj_s[COL_GATE + 1]) * _dot(hb, wpb)

        xc = proj_s[COL_XC]
        xc_ref[...] = xc
        psc = psc_ref[...]
        hc = []
        for g, (w, sp_ref) in enumerate(zip(POOL_WINDOWS, (sp0_ref, sp1_ref, sp2_ref, sp3_ref))):
            cs = slice(g * G_C, (g + 1) * G_C)
            tot = xc[:, cs]
            for j in range(w - 1):
                tot = tot + sp_ref[w - 2 - j]
            d = tot * (1.0 / w) - xc[:, cs]
            hc.append((_dot(d.astype(BF16), wpool[g]) * psc[:, cs]).astype(BF16))
        hc = jnp.concatenate(hc, axis=-1)
        acc = acc + jax.nn.sigmoid(proj_s[COL_GATE + 2]) * _dot(hc, wpc)

        o = _dot(acc.astype(BF16), wo)
        h_ref[...] = _layer_norm(ALPHA * x + o, l1g_ref[...], l1b_ref[...])


def _ffn_rows(h, hb, w1_ref, w2_ref, g_ref, b_ref):
    acc = None
    for j in range(D_FF // D):
        f = _dot(hb, w1_ref[:, j * D:(j + 1) * D])
        a = jnp.square(jnp.maximum(f, 0.0)).astype(BF16)
        c = _dot(a, w2_ref[j * D:(j + 1) * D, :])
        acc = c if acc is None else acc + c
    return _layer_norm(ALPHA * h + acc, g_ref[...], b_ref[...])


def _ffn_kernel(h_ref, hs_ref, w1_ref, w2_ref, g_ref, b_ref, o_ref, os_ref, hb_s, *, sub):
    for r0 in range(0, h_ref.shape[0], sub):
        rs = slice(r0, r0 + sub)
        hb_s[rs, :] = h_ref[rs, :].astype(BF16)
        o_ref[rs, :] = _ffn_rows(h_ref[rs, :], hb_s[rs, :], w1_ref, w2_ref, g_ref, b_ref)

    @pl.when(pl.program_id(0) == pl.num_programs(0) - 1)
    def _():
        hs = hs_ref[...]
        os_ref[...] = _ffn_rows(hs, hs.astype(BF16), w1_ref, w2_ref, g_ref, b_ref)


def _resident(shape):
    nd = len(shape)
    return pl.BlockSpec(shape, lambda *_: (0,) * nd, pipeline_mode=pl.Buffered(1))


def _layer_resident(a, l):
    return pl.BlockSpec((None,) + a.shape[1:], lambda *_: (l,) + (0,) * (a.ndim - 1), pipeline_mode=pl.Buffered(1))


MIXER_MATS = ('w_in', 'w_proj_a', 'w_proj_b', 'w_pool', 'w_proj_c', 'w_o')


def _mixer_prompt(x2d, batch, seq, tm, mats, p, l):
    nt = seq // tm
    nsteps = batch * nt
    row_block = pl.BlockSpec((tm, D), lambda b, t: (b * nt + t, 0))
    r1, r2 = D // nsteps, D_FF // nsteps
    lay = lambda k: (p[k], _layer_resident(p[k], l))
    mat = lambda k: (mats[k], _resident(mats[k].shape))
    operands = [(x2d, row_block), mat('w_in'), lay('lnv_g'), lay('lnv_b'), lay('w_spatial'), lay('b_spatial_t'),
                mat('w_proj_a'), lay('conv_w'), lay('conv_b'), mat('w_proj_b'), mat('w_pool'), lay('pool_scale'),
                mat('w_proj_c'), mat('w_o'), lay('ln1_g'), lay('ln1_b'),
                (p['w_ff1'], pl.BlockSpec((None, r1, D_FF), lambda b, t: (l, b * nt + t, 0))),
                (p['w_ff2'], pl.BlockSpec((None, r2, D), lambda b, t: (l, b * nt + t, 0)))]
    return pl.pallas_call(
        functools.partial(_mixer_prompt_kernel, tm=tm, sub=min(tm, SUB_MIXER)),
        grid=(batch, nt),
        in_specs=[spec for _, spec in operands],
        out_specs=[row_block,
                   pl.BlockSpec((None, CONV_W - 1, D), lambda b, t: (b, 0, 0)),
                   pl.BlockSpec((None, POOL_BUF, D), lambda b, t: (b, 0, 0)),
                   pl.BlockSpec((r1, D_FF), lambda b, t: (b * nt + t, 0)),
                   pl.BlockSpec((r2, D), lambda b, t: (b * nt + t, 0))],
        out_shape=[jax.ShapeDtypeStruct((batch * seq, D), F32),
                   jax.ShapeDtypeStruct((batch, CONV_W - 1, D), F32),
                   jax.ShapeDtypeStruct((batch, POOL_BUF, D), F32),
                   jax.ShapeDtypeStruct((D, D_FF), BF16),
                   jax.ShapeDtypeStruct((D_FF, D), BF16)],
        scratch_shapes=[pltpu.VMEM((tm, D), BF16),
                        pltpu.VMEM((tm, D), BF16),
                        pltpu.VMEM((tm, D), F32),
                        pltpu.VMEM((tm, D), BF16),
                        pltpu.VMEM((tm, D), BF16),
                        pltpu.VMEM((tm, D), BF16),
                        pltpu.VMEM((tm, D), F32),
                        pltpu.VMEM((N_LT, Z_HEAD + tm, LANES), F32),
                        pltpu.VMEM((N_LT, P_HEAD + tm, LANES), F32),
                        pltpu.VMEM((N_LT, P_HEAD + tm, LANES), F32),
                        pltpu.VMEM((N_LT - LT_G, P_HEAD + tm, LANES), F32),
                        pltpu.VMEM((N_LT - 2 * LT_G, P_HEAD + tm, LANES), F32)],
        compiler_params=pltpu.CompilerParams(dimension_semantics=("arbitrary", "arbitrary"),
                                             vmem_limit_bytes=VMEM_LIMIT_MIXER),
        name="mixer_prompt",
    )(*[a for a, _ in operands])


def _sample_mixer(x2d, p, l):
    n = x2d.shape[0]
    lay = lambda k: (p[k], _layer_resident(p[k], l))
    col_block = lambda k: (0, jnp.minimum(k, N_COLS - 1))
    row_piece = lambda k: (jnp.minimum(k, N_ROW_PIECES - 1), 0)
    sq = lambda name: (p[name], pl.BlockSpec((None, ROW_PIECE, D), lambda k: (l,) + row_piece(k)))
    operands = [(x2d, _resident(x2d.shape)), lay('state_conv_t'), lay('state_pool_t0'), lay('state_pool_t1'),
                lay('state_pool_t2'), lay('state_pool_t3'),
                (p['w_in'], pl.BlockSpec((None, D, D), lambda k: (l,) + col_block(k))),
                lay('lnv_g'), lay('lnv_b'), lay('w_spatial_d'), lay('b_spatial_d'), sq('w_proj_a'), lay('conv_w'),
                lay('conv_b'), sq('w_proj_b'), lay('w_pool'), lay('pool_scale'), sq('w_proj_c'), sq('w_o'),
                lay('ln1_g'), lay('ln1_b')]
    once = lambda shape: pl.BlockSpec(shape, lambda k: (0,) * len(shape), pipeline_mode=pl.Buffered(1))
    sq_out = pl.BlockSpec((ROW_PIECE, D), row_piece)
    outs = pl.pallas_call(
        _sample_mixer_kernel,
        grid=(N_COLS + 1,),
        in_specs=[spec for _, spec in operands],
        out_specs=[once((n, D))] * 4 + [pl.BlockSpec((D, D), col_block), sq_out, sq_out,
                                        once(p['w_pool'].shape[1:]), sq_out, sq_out],
        out_shape=[jax.ShapeDtypeStruct((n, D), F32)] * 4
                  + [jax.ShapeDtypeStruct(p[k].shape[1:], BF16) for k in MIXER_MATS],
        scratch_shapes=[pltpu.VMEM((n, D), BF16),
                        pltpu.VMEM((N_COLS, n, D), F32)]
                       + [pltpu.VMEM((D, D), BF16)] * 4,
        compiler_params=pltpu.CompilerParams(dimension_semantics=("arbitrary",),
                                             vmem_limit_bytes=VMEM_LIMIT_MIXER),
        name="sample_mixer",
    )(*[a for a, _ in operands])
    return outs[:4], dict(zip(MIXER_MATS, outs[4:]))


def _ffn(h2d, hs2d, tm, w1, w2, p, l):
    n = h2d.shape[0]
    row_block = pl.BlockSpec((tm, D), lambda i: (i, 0))
    lay = lambda k: (p[k], _layer_resident(p[k], l))
    operands = [(h2d, row_block), (hs2d, _resident(hs2d.shape)), (w1, _resident(w1.shape)), (w2, _resident(w2.shape)),
                lay('ln2_g'), lay('ln2_b')]
    return pl.pallas_call(
        functools.partial(_ffn_kernel, sub=min(tm, SUB_FFN)),
        grid=(n // tm,),
        in_specs=[spec for _, spec in operands],
        out_specs=[row_block, pl.BlockSpec(hs2d.shape, lambda i: (0, 0))],
        out_shape=[jax.ShapeDtypeStruct((n, D), F32), jax.ShapeDtypeStruct(hs2d.shape, F32)],
        scratch_shapes=[pltpu.VMEM((tm, D), BF16)],
        compiler_params=pltpu.CompilerParams(dimension_semantics=("arbitrary",),
                                             vmem_limit_bytes=VMEM_LIMIT_FFN),
        name="ffn",
    )(*[a for a, _ in operands])


TM_MIXER = 512
TM_FFN = 1024
SUB_FFN = 256
SUB_MIXER = 256


def kernel(x_prompt, x_sample, state_conv, state_pool, w_in, lnv_g, lnv_b, w_spatial, b_spatial, w_proj_a, conv_w, conv_b, w_proj_b, w_pool, pool_scale, w_proj_c, w_o, ln1_g, ln1_b, w_ff1, w_ff2, ln2_g, ln2_b):
    bp, seq, _ = x_prompt.shape
    bs = x_sample.shape[0]
    xp = x_prompt.reshape(bp * seq, D)
    xs = x_sample.reshape(bs, D)
    conv_p, pool_p, conv_s, pool_s, chunk_v_s = [], [], [], [], []
    rows = lambda a: a[:, None, :]
    p = {
        'w_in': w_in, 'lnv_g': rows(lnv_g), 'lnv_b': rows(lnv_b),
        'w_spatial': w_spatial, 'b_spatial_t': jnp.swapaxes(b_spatial, 1, 2),
        'w_spatial_d': rows(jnp.repeat(w_spatial[:, :, 0, 0], G_A, axis=1)),
        'b_spatial_d': rows(jnp.repeat(b_spatial[:, :, 0], G_A, axis=1)),
        'w_proj_a': w_proj_a, 'conv_w': conv_w, 'conv_b': rows(conv_b),
        'w_proj_b': w_proj_b, 'w_pool': w_pool, 'pool_scale': rows(pool_scale),
        'w_proj_c': w_proj_c, 'w_o': w_o, 'ln1_g': rows(ln1_g), 'ln1_b': rows(ln1_b),
        'w_ff1': w_ff1, 'w_ff2': w_ff2, 'ln2_g': rows(ln2_g), 'ln2_b': rows(ln2_b),
        'state_conv_t': jnp.swapaxes(state_conv, 1, 2),
    }
    for g, w in enumerate(POOL_WINDOWS):
        p['state_pool_t%d' % g] = jnp.swapaxes(state_pool[:, :, POOL_BUF - (w - 1):, g * G_C:(g + 1) * G_C], 1, 2)
    for l in range(DEPTH):
        (hs, z_new, xc_new, vn), mats = _sample_mixer(xs, p, l)
        conv_s.append(jnp.concatenate([state_conv[l][:, 1:], z_new[:, None, :]], axis=1))
        pool_s.append(jnp.concatenate([state_pool[l][:, 1:], xc_new[:, None, :]], axis=1))
        chunk_v_s.append(vn[:, None, :])
        hp, nconv, npool, w1, w2 = _mixer_prompt(xp, bp, seq, TM_MIXER, mats, p, l)
        conv_p.append(nconv)
        pool_p.append(npool)
        xp, xs = _ffn(hp, hs, TM_FFN, w1, w2, p, l)
    return (xp.reshape(bp, seq, D), xs.reshape(bs, 1, D), jnp.stack(conv_p), jnp.stack(pool_p),
            jnp.stack(conv_s), jnp.stack(pool_s), jnp.stack(chunk_v_s))
```

```python
import functools

import jax
import jax.numpy as jnp
from jax import lax
from jax.experimental import pallas as pl
from jax.experimental.pallas import tpu as pltpu

D = 1024
N_GROUPS_A = 4
CHUNK = 128
G_A = D // N_GROUPS_A
CONV_W = 3
POOL_WINDOWS = (2, 4, 8, 16)
G_C = D // len(POOL_WINDOWS)
POOL_BUF = max(POOL_WINDOWS) - 1
D_FF = 4 * D
DEPTH = 2
ALPHA = float((2 * DEPTH) ** 0.25)
LN_EPS = 1e-5
PAST_LEN = 16384

COL_U, COL_V, COL_BG, COL_CG, COL_XB, COL_XC, COL_GATE = 0, 1, 2, 3, 4, 5, 6
N_COLS = 9
N_ROW_PIECES = 8
ROW_PIECE = D // N_ROW_PIECES

SUBLANES = 8
LANES = 128
N_LT = D // LANES
LT_G = G_C // LANES
Z_HEAD = SUBLANES
P_HEAD = 32
P_LO = 16

VMEM_LIMIT_MIXER = 56 * 1024 * 1024
VMEM_LIMIT_FFN = 48 * 1024 * 1024

F32 = jnp.float32
BF16 = jnp.bfloat16


def _dot(a, b):
    return jnp.dot(a, b, preferred_element_type=F32)


def _layer_norm(x, g, b):
    mu = jnp.mean(x, axis=-1, keepdims=True)
    xc = x - mu
    var = jnp.mean(xc * xc, axis=-1, keepdims=True)
    return xc * lax.rsqrt(var + LN_EPS) * g + b


def _mixer_prompt_kernel(x_ref, win_ref, lnvg_ref, lnvb_ref, wsp_ref, bsp_ref, wpa_ref, cw_ref, cb_ref,
                         wpb_ref, wpool_ref, psc_ref, wpc_ref, wo_ref, l1g_ref, l1b_ref, wf1_ref, wf2_ref,
                         h_ref, nconv_ref, npool_ref, wf1o_ref, wf2o_ref,
                         xb_s, vn_s, u_s, ha_s, hb_s, hc_s, acc_s, z_s, p_s, la_s, lb_s, lc_s, *, tm, sub):
    t = pl.program_id(1)

    @pl.when(t == 0)
    def _():
        z_s[:, 0:Z_HEAD, :] = jnp.zeros((N_LT, Z_HEAD, LANES), F32)
        p_s[:, 0:P_HEAD, :] = jnp.zeros((N_LT, P_HEAD, LANES), F32)
        la_s[:, 0:P_LO, :] = jnp.zeros((N_LT, P_LO, LANES), F32)
        lb_s[:, 0:P_LO, :] = jnp.zeros((N_LT - LT_G, P_LO, LANES), F32)
        lc_s[:, 0:P_LO, :] = jnp.zeros((N_LT - 2 * LT_G, P_LO, LANES), F32)

    row = lax.broadcasted_iota(jnp.int32, (CHUNK, CHUNK), 0)
    col = lax.broadcasted_iota(jnp.int32, (CHUNK, CHUNK), 1)
    causal = col <= row
    w_spatial = [jnp.where(causal, wsp_ref[g], 0.0).astype(BF16) for g in range(N_GROUPS_A)]
    cw = cw_ref[...]
    psc = psc_ref[...]

    for r0 in range(0, tm, sub):
        rs = slice(r0, r0 + sub)
        xb_s[rs, :] = x_ref[rs, :].astype(BF16)

        def proj(k):
            return _dot(xb_s[rs, :], win_ref[:, k * D:(k + 1) * D])

        xc = proj(COL_XC)
        p0, p1 = P_HEAD + r0, P_HEAD + r0 + sub
        for j in range(N_LT):
            p_s[j, p0:p1, :] = xc[:, j * LANES:(j + 1) * LANES]
        lo = P_LO if r0 == 0 else p0
        for j in range(N_LT):
            la_s[j, lo:p1, :] = p_s[j, lo:p1, :] + p_s[j, lo - 1:p1 - 1, :]
        for j in range(N_LT - LT_G):
            lb_s[j, lo:p1, :] = la_s[j + LT_G, lo:p1, :] + la_s[j + LT_G, lo - 2:p1 - 2, :]
        for j in range(N_LT - 2 * LT_G):
            lc_s[j, lo:p1, :] = lb_s[j + LT_G, lo:p1, :] + lb_s[j + LT_G, lo - 4:p1 - 4, :]
        sum16 = [lc_s[j + LT_G, p0:p1, :] + lc_s[j + LT_G, p0 - 8:p1 - 8, :] for j in range(LT_G)]
        sums = tuple(jnp.concatenate(tiles, axis=-1) for tiles in (
            [la_s[j, p0:p1, :] for j in range(LT_G)], [lb_s[j, p0:p1, :] for j in range(LT_G)],
            [lc_s[j, p0:p1, :] for j in range(LT_G)], sum16))
        pos1 = t * tm + r0 + lax.broadcasted_iota(jnp.int32, (sub, 1), 0) + 1
        for g, w in enumerate(POOL_WINDOWS):
            cs = slice(g * G_C, (g + 1) * G_C)
            inv_cnt = 1.0 / jnp.minimum(pos1, w).astype(F32)
            d = sums[g] * inv_cnt - xc[:, cs]
            hc_s[rs, cs] = (_dot(d.astype(BF16), wpool_ref[g]) * psc[:, cs]).astype(BF16)
        acc_s[rs, :] = jax.nn.sigmoid(proj(COL_GATE + 2)) * _dot(hc_s[rs, :], wpc_ref[...])

        vn_s[rs, :] = _layer_norm(proj(COL_V), lnvg_ref[...], lnvb_ref[...]).astype(BF16)
        u_s[rs, :] = proj(COL_U)
        for g in range(N_GROUPS_A):
            bias = bsp_ref[:, g:g + 1]
            cs = slice(g * G_A, (g + 1) * G_A)
            for c0 in range(r0, r0 + sub, CHUNK):
                ch = slice(c0, c0 + CHUNK)
                s = _dot(w_spatial[g], vn_s[ch, cs]) + bias
                ha_s[ch, cs] = (u_s[ch, cs] * s).astype(BF16)
        acc_s[rs, :] += jax.nn.sigmoid(proj(COL_GATE + 0)) * _dot(ha_s[rs, :], wpa_ref[...])

        z = proj(COL_CG) * proj(COL_XB)
        z0 = Z_HEAD + r0
        for j in range(N_LT):
            z_s[j, z0:z0 + sub, :] = z[:, j * LANES:(j + 1) * LANES]
        z_m2 = jnp.concatenate([z_s[j, z0 - 2:z0 - 2 + sub, :] for j in range(N_LT)], axis=-1)
        z_m1 = jnp.concatenate([z_s[j, z0 - 1:z0 - 1 + sub, :] for j in range(N_LT)], axis=-1)
        y = cb_ref[...] + (cw[0:1] * z_m2 + cw[1:2] * z_m1 + cw[2:3] * z)
        hb_s[rs, :] = (proj(COL_BG) * y).astype(BF16)
        acc_s[rs, :] += jax.nn.sigmoid(proj(COL_GATE + 1)) * _dot(hb_s[rs, :], wpb_ref[...])

        o = _dot(acc_s[rs, :].astype(BF16), wo_ref[...])
        h_ref[rs, :] = _layer_norm(ALPHA * x_ref[rs, :] + o, l1g_ref[...], l1b_ref[...])

    tail = z_s[:, tm:tm + Z_HEAD, :]
    z_s[:, 0:Z_HEAD, :] = tail
    hist = p_s[:, tm + P_HEAD - 16:tm + P_HEAD, :]
    p_s[:, P_HEAD - 16:P_HEAD, :] = hist
    for j in range(N_LT):
        nconv_ref[:, j * LANES:(j + 1) * LANES] = z_s[j, tm + Z_HEAD - (CONV_W - 1):tm + Z_HEAD, :]
        npool_ref[:, j * LANES:(j + 1) * LANES] = p_s[j, tm + P_HEAD - POOL_BUF:tm + P_HEAD, :]

    wf1o_ref[...] = wf1_ref[...].astype(BF16)
    wf2o_ref[...] = wf2_ref[...].astype(BF16)


def _sample_mixer_kernel(x_ref, sconv_ref, sp0_ref, sp1_ref, sp2_ref, sp3_ref, win_ref, lnvg_ref, lnvb_ref, wsd_ref,
                         bsd_ref, wpa_ref, cw_ref, cb_ref, wpb_ref, wpool_ref, psc_ref, wpc_ref, wo_ref, l1g_ref,
                         l1b_ref, h_ref, z_ref, xc_ref, vn_ref, *rest, emit_bf16):
    if emit_bf16:
        wino_ref, wpao_ref, wpbo_ref, wpoolo_ref, wpco_ref, woo_ref = rest[:len(MIXER_MATS)]
    else:
        wino_ref = wpao_ref = wpbo_ref = wpoolo_ref = wpco_ref = woo_ref = None
    xb_s, proj_s, wpa_s, wpb_s, wpc_s, wo_s = rest[len(MIXER_MATS) if emit_bf16 else 0:]
    k = pl.program_id(0)

    @pl.when(k == 0)
    def _():
        xb_s[...] = x_ref[...].astype(BF16)

    @pl.when(k < N_COLS)
    def _():
        w = win_ref[...].astype(BF16)
        if emit_bf16:
            wino_ref[...] = w
        proj_s[k] = _dot(xb_s[...], w)

    @pl.when(k < N_ROW_PIECES)
    def _():
        rows = pl.ds(pl.multiple_of(k * ROW_PIECE, ROW_PIECE), ROW_PIECE)
        for src, dst, keep in ((wpa_ref, wpao_ref, wpa_s), (wpb_ref, wpbo_ref, wpb_s), (wpc_ref, wpco_ref, wpc_s),
                               (wo_ref, woo_ref, wo_s)):
            piece = src[...].astype(BF16)
            if emit_bf16:
                dst[...] = piece
            keep[rows, :] = piece

    @pl.when(k == N_COLS)
    def _():
        x = x_ref[...]
        wpa, wpb, wpc, wo = wpa_s[...], wpb_s[...], wpc_s[...], wo_s[...]
        wpool = wpool_ref[...].astype(BF16)
        if emit_bf16:
            wpoolo_ref[...] = wpool

        vn = _layer_norm(proj_s[COL_V], lnvg_ref[...], lnvb_ref[...])
        vn_ref[...] = vn
        s = wsd_ref[...] * vn + bsd_ref[...]
        ha = (proj_s[COL_U] * s).astype(BF16)
        acc = jax.nn.sigmoid(proj_s[COL_GATE + 0]) * _dot(ha, wpa)

        z = proj_s[COL_CG] * proj_s[COL_XB]
        z_ref[...] = z
        cw = cw_ref[...]
        y = cb_ref[...] + (cw[0:1] * sconv_ref[0] + cw[1:2] * sconv_ref[1] + cw[2:3] * z)
        hb = (proj_s[COL_BG] * y).astype(BF16)
        acc = acc + jax.nn.sigmoid(proj_s[COL_GATE + 1]) * _dot(hb, wpb)

        xc = proj_s[COL_XC]
        xc_ref[...] = xc
        psc = psc_ref[...]
        hc = []
        for g, (w, sp_ref) in enumerate(zip(POOL_WINDOWS, (sp0_ref, sp1_ref, sp2_ref, sp3_ref))):
            cs = slice(g * G_C, (g + 1) * G_C)
            tot = xc[:, cs]
            for j in range(w - 1):
                tot = tot + sp_ref[w - 2 - j]
            d = tot * (1.0 / w) - xc[:, cs]
            hc.append((_dot(d.astype(BF16), wpool[g]) * psc[:, cs]).astype(BF16))
        hc = jnp.concatenate(hc, axis=-1)
        acc = acc + jax.nn.sigmoid(proj_s[COL_GATE + 2]) * _dot(hc, wpc)

        o = _dot(acc.astype(BF16), wo)
        h_ref[...] = _layer_norm(ALPHA * x + o, l1g_ref[...], l1b_ref[...])


def _ffn_rows(h, hb, w1_ref, w2_ref, g_ref, b_ref):
    acc = None
    for j in range(D_FF // D):
        f = _dot(hb, w1_ref[:, j * D:(j + 1) * D])
        a = jnp.square(jnp.maximum(f, 0.0)).astype(BF16)
        c = _dot(a, w2_ref[j * D:(j + 1) * D, :])
        acc = c if acc is None else acc + c
    return _layer_norm(ALPHA * h + acc, g_ref[...], b_ref[...])


def _ffn_kernel(*refs, sub, n_cvt):
    h_ref, hs_ref, w1_ref, w2_ref, g_ref, b_ref = refs[:6]
    cvt_in = refs[6:6 + n_cvt]
    o_ref, os_ref = refs[6 + n_cvt:8 + n_cvt]
    cvt_out = refs[8 + n_cvt:8 + 2 * n_cvt]
    (hb_s,) = refs[8 + 2 * n_cvt:]

    for r0 in range(0, h_ref.shape[0], sub):
        rs = slice(r0, r0 + sub)
        hb_s[rs, :] = h_ref[rs, :].astype(BF16)
        o_ref[rs, :] = _ffn_rows(h_ref[rs, :], hb_s[rs, :], w1_ref, w2_ref, g_ref, b_ref)

    @pl.when(pl.program_id(0) == pl.num_programs(0) - 1)
    def _():
        hs = hs_ref[...]
        os_ref[...] = _ffn_rows(hs, hs.astype(BF16), w1_ref, w2_ref, g_ref, b_ref)

    for src, dst in zip(cvt_in, cvt_out):
        dst[...] = src[...].astype(BF16)


def _resident(shape):
    nd = len(shape)
    return pl.BlockSpec(shape, lambda *_: (0,) * nd, pipeline_mode=pl.Buffered(1))


def _layer_resident(a, l):
    return pl.BlockSpec((None,) + a.shape[1:], lambda *_: (l,) + (0,) * (a.ndim - 1), pipeline_mode=pl.Buffered(1))


MIXER_MATS = ('w_in', 'w_proj_a', 'w_proj_b', 'w_pool', 'w_proj_c', 'w_o')


def _mixer_prompt(x2d, batch, seq, tm, mats, p, l):
    nt = seq // tm
    nsteps = batch * nt
    row_block = pl.BlockSpec((tm, D), lambda b, t: (b * nt + t, 0))
    r1, r2 = D // nsteps, D_FF // nsteps
    lay = lambda k: (p[k], _layer_resident(p[k], l))
    mat = lambda k: (mats[k], _resident(mats[k].shape))
    operands = [(x2d, row_block), mat('w_in'), lay('lnv_g'), lay('lnv_b'), lay('w_spatial'), lay('b_spatial_t'),
                mat('w_proj_a'), lay('conv_w'), lay('conv_b'), mat('w_proj_b'), mat('w_pool'), lay('pool_scale'),
                mat('w_proj_c'), mat('w_o'), lay('ln1_g'), lay('ln1_b'),
                (p['w_ff1'], pl.BlockSpec((None, r1, D_FF), lambda b, t: (l, b * nt + t, 0))),
                (p['w_ff2'], pl.BlockSpec((None, r2, D), lambda b, t: (l, b * nt + t, 0)))]
    return pl.pallas_call(
        functools.partial(_mixer_prompt_kernel, tm=tm, sub=min(tm, SUB_MIXER)),
        grid=(batch, nt),
        in_specs=[spec for _, spec in operands],
        out_specs=[row_block,
                   pl.BlockSpec((None, CONV_W - 1, D), lambda b, t: (b, 0, 0)),
                   pl.BlockSpec((None, POOL_BUF, D), lambda b, t: (b, 0, 0)),
                   pl.BlockSpec((r1, D_FF), lambda b, t: (b * nt + t, 0)),
                   pl.BlockSpec((r2, D), lambda b, t: (b * nt + t, 0))],
        out_shape=[jax.ShapeDtypeStruct((batch * seq, D), F32),
                   jax.ShapeDtypeStruct((batch, CONV_W - 1, D), F32),
                   jax.ShapeDtypeStruct((batch, POOL_BUF, D), F32),
                   jax.ShapeDtypeStruct((D, D_FF), BF16),
                   jax.ShapeDtypeStruct((D_FF, D), BF16)],
        scratch_shapes=[pltpu.VMEM((tm, D), BF16),
                        pltpu.VMEM((tm, D), BF16),
                        pltpu.VMEM((tm, D), F32),
                        pltpu.VMEM((tm, D), BF16),
                        pltpu.VMEM((tm, D), BF16),
                        pltpu.VMEM((tm, D), BF16),
                        pltpu.VMEM((tm, D), F32),
                        pltpu.VMEM((N_LT, Z_HEAD + tm, LANES), F32),
                        pltpu.VMEM((N_LT, P_HEAD + tm, LANES), F32),
                        pltpu.VMEM((N_LT, P_HEAD + tm, LANES), F32),
                        pltpu.VMEM((N_LT - LT_G, P_HEAD + tm, LANES), F32),
                        pltpu.VMEM((N_LT - 2 * LT_G, P_HEAD + tm, LANES), F32)],
        compiler_params=pltpu.CompilerParams(dimension_semantics=("arbitrary", "arbitrary"),
                                             vmem_limit_bytes=VMEM_LIMIT_MIXER),
        name="mixer_prompt",
    )(*[a for a, _ in operands])


def _sample_mixer(x2d, mats, p, l):
    n = x2d.shape[0]
    emit_bf16 = mats is None
    lay = lambda k: (p[k], _layer_resident(p[k], l))
    col_block = lambda k: (0, jnp.minimum(k, N_COLS - 1))
    row_piece = lambda k: (jnp.minimum(k, N_ROW_PIECES - 1), 0)
    if emit_bf16:
        w_in = (p['w_in'], pl.BlockSpec((None, D, D), lambda k: (l,) + col_block(k)))
        sq = lambda name: (p[name], pl.BlockSpec((None, ROW_PIECE, D), lambda k: (l,) + row_piece(k)))
        w_pool = lay('w_pool')
    else:
        w_in = (mats['w_in'], pl.BlockSpec((D, D), col_block))
        sq = lambda name: (mats[name], pl.BlockSpec((ROW_PIECE, D), row_piece))
        w_pool = (mats['w_pool'], _resident(mats['w_pool'].shape))
    operands = [(x2d, _resident(x2d.shape)), lay('state_conv_t'), lay('state_pool_t0'), lay('state_pool_t1'),
                lay('state_pool_t2'), lay('state_pool_t3'), w_in,
                lay('lnv_g'), lay('lnv_b'), lay('w_spatial_d'), lay('b_spatial_d'), sq('w_proj_a'), lay('conv_w'),
                lay('conv_b'), sq('w_proj_b'), w_pool, lay('pool_scale'), sq('w_proj_c'), sq('w_o'),
                lay('ln1_g'), lay('ln1_b')]
    once = lambda shape: pl.BlockSpec(shape, lambda k: (0,) * len(shape), pipeline_mode=pl.Buffered(1))
    out_specs = [once((n, D))] * 4
    out_shape = [jax.ShapeDtypeStruct((n, D), F32)] * 4
    if emit_bf16:
        sq_out = pl.BlockSpec((ROW_PIECE, D), row_piece)
        out_specs += [pl.BlockSpec((D, D), col_block), sq_out, sq_out, once(p['w_pool'].shape[1:]), sq_out, sq_out]
        out_shape += [jax.ShapeDtypeStruct(p[k].shape[1:], BF16) for k in MIXER_MATS]
    outs = pl.pallas_call(
        functools.partial(_sample_mixer_kernel, emit_bf16=emit_bf16),
        grid=(N_COLS + 1,),
        in_specs=[spec for _, spec in operands],
        out_specs=out_specs,
        out_shape=out_shape,
        scratch_shapes=[pltpu.VMEM((n, D), BF16),
                        pltpu.VMEM((N_COLS, n, D), F32)]
                       + [pltpu.VMEM((D, D), BF16)] * 4,
        compiler_params=pltpu.CompilerParams(dimension_semantics=("arbitrary",),
                                             vmem_limit_bytes=VMEM_LIMIT_MIXER),
        name="sample_mixer",
    )(*[a for a, _ in operands])
    return outs[:4], (dict(zip(MIXER_MATS, outs[4:])) if emit_bf16 else mats)


def _ffn(h2d, hs2d, tm, w1, w2, p, l, convert_next):
    n = h2d.shape[0]
    nsteps = n // tm
    row_block = pl.BlockSpec((tm, D), lambda i: (i, 0))
    lay = lambda k: (p[k], _layer_resident(p[k], l))
    operands = [(h2d, row_block), (hs2d, _resident(hs2d.shape)), (w1, _resident(w1.shape)), (w2, _resident(w2.shape)),
                lay('ln2_g'), lay('ln2_b')]
    out_specs = [row_block, pl.BlockSpec(hs2d.shape, lambda i: (0, 0))]
    out_shape = [jax.ShapeDtypeStruct((n, D), F32), jax.ShapeDtypeStruct(hs2d.shape, F32)]
    n_cvt = len(MIXER_MATS) if convert_next else 0
    if convert_next:
        for k in MIXER_MATS:
            a = p[k]
            r = a.shape[-2] // nsteps
            lead = a.ndim - 3
            blk = a.shape[1:-2] + (r, a.shape[-1])
            operands.append((a, pl.BlockSpec((None,) + blk, lambda i, lead=lead: (l + 1,) + (0,) * lead + (i, 0))))
            out_specs.append(pl.BlockSpec(blk, lambda i, lead=lead: (0,) * lead + (i, 0)))
            out_shape.append(jax.ShapeDtypeStruct(a.shape[1:], BF16))
    outs = pl.pallas_call(
        functools.partial(_ffn_kernel, sub=min(tm, SUB_FFN), n_cvt=n_cvt),
        grid=(nsteps,),
        in_specs=[spec for _, spec in operands],
        out_specs=out_specs,
        out_shape=out_shape,
        scratch_shapes=[pltpu.VMEM((tm, D), BF16)],
        compiler_params=pltpu.CompilerParams(dimension_semantics=("arbitrary",),
                                             vmem_limit_bytes=VMEM_LIMIT_FFN),
        name="ffn",
    )(*[a for a, _ in operands])
    return outs[0], outs[1], (dict(zip(MIXER_MATS, outs[2:])) if convert_next else None)


TM_MIXER = 512
TM_FFN = 1024
SUB_FFN = 256
SUB_MIXER = 256


def kernel(x_prompt, x_sample, state_conv, state_pool, w_in, lnv_g, lnv_b, w_spatial, b_spatial, w_proj_a, conv_w, conv_b, w_proj_b, w_pool, pool_scale, w_proj_c, w_o, ln1_g, ln1_b, w_ff1, w_ff2, ln2_g, ln2_b):
    bp, seq, _ = x_prompt.shape
    bs = x_sample.shape[0]
    xp = x_prompt.reshape(bp * seq, D)
    xs = x_sample.reshape(bs, D)
    conv_p, pool_p, conv_s, pool_s, chunk_v_s = [], [], [], [], []
    rows = lambda a: a[:, None, :]
    p = {
        'w_in': w_in, 'lnv_g': rows(lnv_g), 'lnv_b': rows(lnv_b),
        'w_spatial': w_spatial, 'b_spatial_t': jnp.swapaxes(b_spatial, 1, 2),
        'w_spatial_d': rows(jnp.repeat(w_spatial[:, :, 0, 0], G_A, axis=1)),
        'b_spatial_d': rows(jnp.repeat(b_spatial[:, :, 0], G_A, axis=1)),
        'w_proj_a': w_proj_a, 'conv_w': conv_w, 'conv_b': rows(conv_b),
        'w_proj_b': w_proj_b, 'w_pool': w_pool, 'pool_scale': rows(pool_scale),
        'w_proj_c': w_proj_c, 'w_o': w_o, 'ln1_g': rows(ln1_g), 'ln1_b': rows(ln1_b),
        'w_ff1': w_ff1, 'w_ff2': w_ff2, 'ln2_g': rows(ln2_g), 'ln2_b': rows(ln2_b),
        'state_conv_t': jnp.swapaxes(state_conv, 1, 2),
    }
    for g, w in enumerate(POOL_WINDOWS):
        p['state_pool_t%d' % g] = jnp.swapaxes(state_pool[:, :, POOL_BUF - (w - 1):, g * G_C:(g + 1) * G_C], 1, 2)
    mats = None
    for l in range(DEPTH):
        (hs, z_new, xc_new, vn), mats = _sample_mixer(xs, mats, p, l)
        conv_s.append(jnp.concatenate([state_conv[l][:, 1:], z_new[:, None, :]], axis=1))
        pool_s.append(jnp.concatenate([state_pool[l][:, 1:], xc_new[:, None, :]], axis=1))
        chunk_v_s.append(vn[:, None, :])
        hp, nconv, npool, w1, w2 = _mixer_prompt(xp, bp, seq, TM_MIXER, mats, p, l)
        conv_p.append(nconv)
        pool_p.append(npool)
        xp, xs, mats = _ffn(hp, hs, TM_FFN, w1, w2, p, l, convert_next=l + 1 < DEPTH)
    return (xp.reshape(bp, seq, D), xs.reshape(bs, 1, D), jnp.stack(conv_p), jnp.stack(pool_p),
            jnp.stack(conv_s), jnp.stack(pool_s), jnp.stack(chunk_v_s))
```

```python
import functools

import jax
import jax.numpy as jnp
from jax import lax
from jax.experimental import pallas as pl
from jax.experimental.pallas import tpu as pltpu

D = 1024
N_GROUPS_A = 4
CHUNK = 128
G_A = D // N_GROUPS_A
CONV_W = 3
POOL_WINDOWS = (2, 4, 8, 16)
G_C = D // len(POOL_WINDOWS)
POOL_BUF = max(POOL_WINDOWS) - 1
D_FF = 4 * D
DEPTH = 2
ALPHA = float((2 * DEPTH) ** 0.25)
LN_EPS = 1e-5
PAST_LEN = 16384

COL_U, COL_V, COL_BG, COL_CG, COL_XB, COL_XC, COL_GATE = 0, 1, 2, 3, 4, 5, 6
N_COLS = 9
N_ROW_PIECES = 8
ROW_PIECE = D // N_ROW_PIECES

SUBLANES = 8
LANES = 128
N_LT = D // LANES
LT_G = G_C // LANES
Z_HEAD = SUBLANES
P_HEAD = 32
P_LO = 16

VMEM_LIMIT_MIXER = 56 * 1024 * 1024
VMEM_LIMIT_SAMPLE = 62 * 1024 * 1024
VMEM_LIMIT_FFN = 48 * 1024 * 1024

F32 = jnp.float32
BF16 = jnp.bfloat16


def _dot(a, b):
    return jnp.dot(a, b, preferred_element_type=F32)


def _layer_norm(x, g, b):
    mu = jnp.mean(x, axis=-1, keepdims=True)
    xc = x - mu
    var = jnp.mean(xc * xc, axis=-1, keepdims=True)
    return xc * lax.rsqrt(var + LN_EPS) * g + b


def _mixer_prompt_kernel(x_ref, win_ref, lnvg_ref, lnvb_ref, wsp_ref, bsp_ref, wpa_ref, cw_ref, cb_ref,
                         wpb_ref, wpool_ref, psc_ref, wpc_ref, wo_ref, l1g_ref, l1b_ref, wf1_ref, wf2_ref,
                         h_ref, nconv_ref, npool_ref, wf1o_ref, wf2o_ref,
                         xb_s, vn_s, u_s, ha_s, hb_s, hc_s, acc_s, z_s, p_s, la_s, lb_s, lc_s, *, tm, sub):
    t = pl.program_id(1)

    @pl.when(t == 0)
    def _():
        z_s[:, 0:Z_HEAD, :] = jnp.zeros((N_LT, Z_HEAD, LANES), F32)
        p_s[:, 0:P_HEAD, :] = jnp.zeros((N_LT, P_HEAD, LANES), F32)
        la_s[:, 0:P_LO, :] = jnp.zeros((N_LT, P_LO, LANES), F32)
        lb_s[:, 0:P_LO, :] = jnp.zeros((N_LT - LT_G, P_LO, LANES), F32)
        lc_s[:, 0:P_LO, :] = jnp.zeros((N_LT - 2 * LT_G, P_LO, LANES), F32)

    row = lax.broadcasted_iota(jnp.int32, (CHUNK, CHUNK), 0)
    col = lax.broadcasted_iota(jnp.int32, (CHUNK, CHUNK), 1)
    causal = col <= row
    w_spatial = [jnp.where(causal, wsp_ref[g], 0.0).astype(BF16) for g in range(N_GROUPS_A)]
    cw = cw_ref[...]
    psc = psc_ref[...]

    for r0 in range(0, tm, sub):
        rs = slice(r0, r0 + sub)
        xb_s[rs, :] = x_ref[rs, :].astype(BF16)

        def proj(k):
            return _dot(xb_s[rs, :], win_ref[:, k * D:(k + 1) * D])

        xc = proj(COL_XC)
        p0, p1 = P_HEAD + r0, P_HEAD + r0 + sub
        for j in range(N_LT):
            p_s[j, p0:p1, :] = xc[:, j * LANES:(j + 1) * LANES]
        lo = P_LO if r0 == 0 else p0
        for j in range(N_LT):
            la_s[j, lo:p1, :] = p_s[j, lo:p1, :] + p_s[j, lo - 1:p1 - 1, :]
        for j in range(N_LT - LT_G):
            lb_s[j, lo:p1, :] = la_s[j + LT_G, lo:p1, :] + la_s[j + LT_G, lo - 2:p1 - 2, :]
        for j in range(N_LT - 2 * LT_G):
            lc_s[j, lo:p1, :] = lb_s[j + LT_G, lo:p1, :] + lb_s[j + LT_G, lo - 4:p1 - 4, :]
        sum16 = [lc_s[j + LT_G, p0:p1, :] + lc_s[j + LT_G, p0 - 8:p1 - 8, :] for j in range(LT_G)]
        sums = tuple(jnp.concatenate(tiles, axis=-1) for tiles in (
            [la_s[j, p0:p1, :] for j in range(LT_G)], [lb_s[j, p0:p1, :] for j in range(LT_G)],
            [lc_s[j, p0:p1, :] for j in range(LT_G)], sum16))
        pos1 = t * tm + r0 + lax.broadcasted_iota(jnp.int32, (sub, 1), 0) + 1
        for g, w in enumerate(POOL_WINDOWS):
            cs = slice(g * G_C, (g + 1) * G_C)
            inv_cnt = 1.0 / jnp.minimum(pos1, w).astype(F32)
            d = sums[g] * inv_cnt - xc[:, cs]
            hc_s[rs, cs] = (_dot(d.astype(BF16), wpool_ref[g]) * psc[:, cs]).astype(BF16)
        acc_s[rs, :] = jax.nn.sigmoid(proj(COL_GATE + 2)) * _dot(hc_s[rs, :], wpc_ref[...])

        vn_s[rs, :] = _layer_norm(proj(COL_V), lnvg_ref[...], lnvb_ref[...]).astype(BF16)
        u_s[rs, :] = proj(COL_U)
        for g in range(N_GROUPS_A):
            bias = bsp_ref[:, g:g + 1]
            cs = slice(g * G_A, (g + 1) * G_A)
            for c0 in range(r0, r0 + sub, CHUNK):
                ch = slice(c0, c0 + CHUNK)
                s = _dot(w_spatial[g], vn_s[ch, cs]) + bias
                ha_s[ch, cs] = (u_s[ch, cs] * s).astype(BF16)
        acc_s[rs, :] += jax.nn.sigmoid(proj(COL_GATE + 0)) * _dot(ha_s[rs, :], wpa_ref[...])

        z = proj(COL_CG) * proj(COL_XB)
        z0 = Z_HEAD + r0
        for j in range(N_LT):
            z_s[j, z0:z0 + sub, :] = z[:, j * LANES:(j + 1) * LANES]
        z_m2 = jnp.concatenate([z_s[j, z0 - 2:z0 - 2 + sub, :] for j in range(N_LT)], axis=-1)
        z_m1 = jnp.concatenate([z_s[j, z0 - 1:z0 - 1 + sub, :] for j in range(N_LT)], axis=-1)
        y = cb_ref[...] + (cw[0:1] * z_m2 + cw[1:2] * z_m1 + cw[2:3] * z)
        hb_s[rs, :] = (proj(COL_BG) * y).astype(BF16)
        acc_s[rs, :] += jax.nn.sigmoid(proj(COL_GATE + 1)) * _dot(hb_s[rs, :], wpb_ref[...])

        o = _dot(acc_s[rs, :].astype(BF16), wo_ref[...])
        h_ref[rs, :] = _layer_norm(ALPHA * x_ref[rs, :] + o, l1g_ref[...], l1b_ref[...])

    tail = z_s[:, tm:tm + Z_HEAD, :]
    z_s[:, 0:Z_HEAD, :] = tail
    hist = p_s[:, tm + P_HEAD - 16:tm + P_HEAD, :]
    p_s[:, P_HEAD - 16:P_HEAD, :] = hist
    for j in range(N_LT):
        nconv_ref[:, j * LANES:(j + 1) * LANES] = z_s[j, tm + Z_HEAD - (CONV_W - 1):tm + Z_HEAD, :]
        npool_ref[:, j * LANES:(j + 1) * LANES] = p_s[j, tm + P_HEAD - POOL_BUF:tm + P_HEAD, :]

    wf1o_ref[...] = wf1_ref[...].astype(BF16)
    wf2o_ref[...] = wf2_ref[...].astype(BF16)


def _sample_mixer_kernel(x_ref, sconv_ref, sp0_ref, sp1_ref, sp2_ref, sp3_ref, win_ref, lnvg_ref, lnvb_ref, wsd_ref,
                         bsd_ref, wpa_ref, cw_ref, cb_ref, wpb_ref, wpool_ref, psc_ref, wpc_ref, wo_ref, l1g_ref,
                         l1b_ref, h_ref, z_ref, npool_ref, vn_ref, *rest, emit_bf16):
    if emit_bf16:
        wino_ref, wpao_ref, wpbo_ref, wpoolo_ref, wpco_ref, woo_ref = rest[:len(MIXER_MATS)]
    else:
        wino_ref = wpao_ref = wpbo_ref = wpoolo_ref = wpco_ref = woo_ref = None
    xb_s, proj_s, wpa_s, wpb_s, wpc_s, wo_s = rest[len(MIXER_MATS) if emit_bf16 else 0:]
    k = pl.program_id(0)

    @pl.when(k == 0)
    def _():
        xb_s[...] = x_ref[...].astype(BF16)

    @pl.when(k < N_COLS)
    def _():
        w = win_ref[...].astype(BF16)
        if emit_bf16:
            wino_ref[...] = w
        proj_s[k] = _dot(xb_s[...], w)

    @pl.when(k < N_ROW_PIECES)
    def _():
        rows = pl.ds(pl.multiple_of(k * ROW_PIECE, ROW_PIECE), ROW_PIECE)
        for src, dst, keep in ((wpa_ref, wpao_ref, wpa_s), (wpb_ref, wpbo_ref, wpb_s), (wpc_ref, wpco_ref, wpc_s),
                               (wo_ref, woo_ref, wo_s)):
            piece = src[...].astype(BF16)
            if emit_bf16:
                dst[...] = piece
            keep[rows, :] = piece

    @pl.when(k == N_COLS)
    def _():
        x = x_ref[...]
        wpa, wpb, wpc, wo = wpa_s[...], wpb_s[...], wpc_s[...], wo_s[...]
        wpool = wpool_ref[...].astype(BF16)
        if emit_bf16:
            wpoolo_ref[...] = wpool

        vn = _layer_norm(proj_s[COL_V], lnvg_ref[...], lnvb_ref[...])
        vn_ref[...] = vn
        s = wsd_ref[...] * vn + bsd_ref[...]
        ha = (proj_s[COL_U] * s).astype(BF16)
        acc = jax.nn.sigmoid(proj_s[COL_GATE + 0]) * _dot(ha, wpa)

        z = proj_s[COL_CG] * proj_s[COL_XB]
        z_ref[...] = z
        cw = cw_ref[...]
        y = cb_ref[...] + (cw[0:1] * sconv_ref[0] + cw[1:2] * sconv_ref[1] + cw[2:3] * z)
        hb = (proj_s[COL_BG] * y).astype(BF16)
        acc = acc + jax.nn.sigmoid(proj_s[COL_GATE + 1]) * _dot(hb, wpb)

        xc = proj_s[COL_XC]
        psc = psc_ref[...]
        hc = []
        for g, (w, sp_ref) in enumerate(zip(POOL_WINDOWS, (sp0_ref, sp1_ref, sp2_ref, sp3_ref))):
            cs = slice(g * G_C, (g + 1) * G_C)
            npool_ref[:, 0:POOL_BUF - 1, cs] = sp_ref[:, 1:, :]
            npool_ref[:, POOL_BUF - 1, cs] = xc[:, cs]
            tot = xc[:, cs] + jnp.sum(sp_ref[:, POOL_BUF - (w - 1):, :], axis=1)
            d = tot * (1.0 / w) - xc[:, cs]
            hc.append((_dot(d.astype(BF16), wpool[g]) * psc[:, cs]).astype(BF16))
        hc = jnp.concatenate(hc, axis=-1)
        acc = acc + jax.nn.sigmoid(proj_s[COL_GATE + 2]) * _dot(hc, wpc)

        o = _dot(acc.astype(BF16), wo)
        h_ref[...] = _layer_norm(ALPHA * x + o, l1g_ref[...], l1b_ref[...])


def _ffn_rows(h, hb, w1_ref, w2_ref, g_ref, b_ref):
    acc = None
    for j in range(D_FF // D):
        f = _dot(hb, w1_ref[:, j * D:(j + 1) * D])
        a = jnp.square(jnp.maximum(f, 0.0)).astype(BF16)
        c = _dot(a, w2_ref[j * D:(j + 1) * D, :])
        acc = c if acc is None else acc + c
    return _layer_norm(ALPHA * h + acc, g_ref[...], b_ref[...])


def _ffn_kernel(*refs, sub, n_cvt):
    h_ref, hs_ref, w1_ref, w2_ref, g_ref, b_ref = refs[:6]
    cvt_in = refs[6:6 + n_cvt]
    o_ref, os_ref = refs[6 + n_cvt:8 + n_cvt]
    cvt_out = refs[8 + n_cvt:8 + 2 * n_cvt]
    (hb_s,) = refs[8 + 2 * n_cvt:]

    for r0 in range(0, h_ref.shape[0], sub):
        rs = slice(r0, r0 + sub)
        hb_s[rs, :] = h_ref[rs, :].astype(BF16)
        o_ref[rs, :] = _ffn_rows(h_ref[rs, :], hb_s[rs, :], w1_ref, w2_ref, g_ref, b_ref)

    @pl.when(pl.program_id(0) == pl.num_programs(0) - 1)
    def _():
        hs = hs_ref[...]
        os_ref[...] = _ffn_rows(hs, hs.astype(BF16), w1_ref, w2_ref, g_ref, b_ref)

    for src, dst in zip(cvt_in, cvt_out):
        dst[...] = src[...].astype(BF16)


def _resident(shape):
    nd = len(shape)
    return pl.BlockSpec(shape, lambda *_: (0,) * nd, pipeline_mode=pl.Buffered(1))


def _layer_resident(a, l):
    return pl.BlockSpec((None,) + a.shape[1:], lambda *_: (l,) + (0,) * (a.ndim - 1), pipeline_mode=pl.Buffered(1))


MIXER_MATS = ('w_in', 'w_proj_a', 'w_proj_b', 'w_pool', 'w_proj_c', 'w_o')


def _mixer_prompt(x2d, batch, seq, tm, mats, p, l):
    nt = seq // tm
    nsteps = batch * nt
    row_block = pl.BlockSpec((tm, D), lambda b, t: (b * nt + t, 0))
    r1, r2 = D // nsteps, D_FF // nsteps
    lay = lambda k: (p[k], _layer_resident(p[k], l))
    mat = lambda k: (mats[k], _resident(mats[k].shape))
    operands = [(x2d, row_block), mat('w_in'), lay('lnv_g'), lay('lnv_b'), lay('w_spatial'), lay('b_spatial_t'),
                mat('w_proj_a'), lay('conv_w'), lay('conv_b'), mat('w_proj_b'), mat('w_pool'), lay('pool_scale'),
                mat('w_proj_c'), mat('w_o'), lay('ln1_g'), lay('ln1_b'),
                (p['w_ff1'], pl.BlockSpec((None, r1, D_FF), lambda b, t: (l, b * nt + t, 0))),
                (p['w_ff2'], pl.BlockSpec((None, r2, D), lambda b, t: (l, b * nt + t, 0)))]
    return pl.pallas_call(
        functools.partial(_mixer_prompt_kernel, tm=tm, sub=min(tm, SUB_MIXER)),
        grid=(batch, nt),
        in_specs=[spec for _, spec in operands],
        out_specs=[row_block,
                   pl.BlockSpec((None, CONV_W - 1, D), lambda b, t: (b, 0, 0)),
                   pl.BlockSpec((None, POOL_BUF, D), lambda b, t: (b, 0, 0)),
                   pl.BlockSpec((r1, D_FF), lambda b, t: (b * nt + t, 0)),
                   pl.BlockSpec((r2, D), lambda b, t: (b * nt + t, 0))],
        out_shape=[jax.ShapeDtypeStruct((batch * seq, D), F32),
                   jax.ShapeDtypeStruct((batch, CONV_W - 1, D), F32),
                   jax.ShapeDtypeStruct((batch, POOL_BUF, D), F32),
                   jax.ShapeDtypeStruct((D, D_FF), BF16),
                   jax.ShapeDtypeStruct((D_FF, D), BF16)],
        scratch_shapes=[pltpu.VMEM((tm, D), BF16),
                        pltpu.VMEM((tm, D), BF16),
                        pltpu.VMEM((tm, D), F32),
                        pltpu.VMEM((tm, D), BF16),
                        pltpu.VMEM((tm, D), BF16),
                        pltpu.VMEM((tm, D), BF16),
                        pltpu.VMEM((tm, D), F32),
                        pltpu.VMEM((N_LT, Z_HEAD + tm, LANES), F32),
                        pltpu.VMEM((N_LT, P_HEAD + tm, LANES), F32),
                        pltpu.VMEM((N_LT, P_HEAD + tm, LANES), F32),
                        pltpu.VMEM((N_LT - LT_G, P_HEAD + tm, LANES), F32),
                        pltpu.VMEM((N_LT - 2 * LT_G, P_HEAD + tm, LANES), F32)],
        compiler_params=pltpu.CompilerParams(dimension_semantics=("arbitrary", "arbitrary"),
                                             vmem_limit_bytes=VMEM_LIMIT_MIXER),
        name="mixer_prompt",
    )(*[a for a, _ in operands])


def _sample_mixer(x2d, mats, p, l):
    n = x2d.shape[0]
    emit_bf16 = mats is None
    lay = lambda k: (p[k], _layer_resident(p[k], l))
    col_block = lambda k: (0, jnp.minimum(k, N_COLS - 1))
    row_piece = lambda k: (jnp.minimum(k, N_ROW_PIECES - 1), 0)
    if emit_bf16:
        w_in = (p['w_in'], pl.BlockSpec((None, D, D), lambda k: (l,) + col_block(k)))
        sq = lambda name: (p[name], pl.BlockSpec((None, ROW_PIECE, D), lambda k: (l,) + row_piece(k)))
        w_pool = lay('w_pool')
    else:
        w_in = (mats['w_in'], pl.BlockSpec((D, D), col_block))
        sq = lambda name: (mats[name], pl.BlockSpec((ROW_PIECE, D), row_piece))
        w_pool = (mats['w_pool'], _resident(mats['w_pool'].shape))
    pool_g = lambda g: (p['state_pool'], pl.BlockSpec((None, n, POOL_BUF, G_C), lambda k: (l, 0, 0, g),
                                                      pipeline_mode=pl.Buffered(1)))
    operands = [(x2d, _resident(x2d.shape)), lay('state_conv_t'), pool_g(0), pool_g(1), pool_g(2), pool_g(3), w_in,
                lay('lnv_g'), lay('lnv_b'), lay('w_spatial_d'), lay('b_spatial_d'), sq('w_proj_a'), lay('conv_w'),
                lay('conv_b'), sq('w_proj_b'), w_pool, lay('pool_scale'), sq('w_proj_c'), sq('w_o'),
                lay('ln1_g'), lay('ln1_b')]
    once = lambda shape: pl.BlockSpec(shape, lambda k: (0,) * len(shape), pipeline_mode=pl.Buffered(1))
    out_specs = [once((n, D)), once((n, D)), once((n, POOL_BUF, D)), once((n, D))]
    out_shape = [jax.ShapeDtypeStruct(spec.block_shape, F32) for spec in out_specs]
    if emit_bf16:
        sq_out = pl.BlockSpec((ROW_PIECE, D), row_piece)
        out_specs += [pl.BlockSpec((D, D), col_block), sq_out, sq_out, once(p['w_pool'].shape[1:]), sq_out, sq_out]
        out_shape += [jax.ShapeDtypeStruct(p[k].shape[1:], BF16) for k in MIXER_MATS]
    outs = pl.pallas_call(
        functools.partial(_sample_mixer_kernel, emit_bf16=emit_bf16),
        grid=(N_COLS + 1,),
        in_specs=[spec for _, spec in operands],
        out_specs=out_specs,
        out_shape=out_shape,
        scratch_shapes=[pltpu.VMEM((n, D), BF16),
                        pltpu.VMEM((N_COLS, n, D), F32)]
                       + [pltpu.VMEM((D, D), BF16)] * 4,
        compiler_params=pltpu.CompilerParams(dimension_semantics=("arbitrary",),
                                             vmem_limit_bytes=VMEM_LIMIT_SAMPLE),
        name="sample_mixer",
    )(*[a for a, _ in operands])
    return outs[:4], (dict(zip(MIXER_MATS, outs[4:])) if emit_bf16 else mats)


def _ffn(h2d, hs2d, tm, w1, w2, p, l, convert_next):
    n = h2d.shape[0]
    nsteps = n // tm
    row_block = pl.BlockSpec((tm, D), lambda i: (i, 0))
    lay = lambda k: (p[k], _layer_resident(p[k], l))
    operands = [(h2d, row_block), (hs2d, _resident(hs2d.shape)), (w1, _resident(w1.shape)), (w2, _resident(w2.shape)),
                lay('ln2_g'), lay('ln2_b')]
    out_specs = [row_block, pl.BlockSpec(hs2d.shape, lambda i: (0, 0))]
    out_shape = [jax.ShapeDtypeStruct((n, D), F32), jax.ShapeDtypeStruct(hs2d.shape, F32)]
    n_cvt = len(MIXER_MATS) if convert_next else 0
    if convert_next:
        for k in MIXER_MATS:
            a = p[k]
            r = a.shape[-2] // nsteps
            lead = a.ndim - 3
            blk = a.shape[1:-2] + (r, a.shape[-1])
            operands.append((a, pl.BlockSpec((None,) + blk, lambda i, lead=lead: (l + 1,) + (0,) * lead + (i, 0))))
            out_specs.append(pl.BlockSpec(blk, lambda i, lead=lead: (0,) * lead + (i, 0)))
            out_shape.append(jax.ShapeDtypeStruct(a.shape[1:], BF16))
    outs = pl.pallas_call(
        functools.partial(_ffn_kernel, sub=min(tm, SUB_FFN), n_cvt=n_cvt),
        grid=(nsteps,),
        in_specs=[spec for _, spec in operands],
        out_specs=out_specs,
        out_shape=out_shape,
        scratch_shapes=[pltpu.VMEM((tm, D), BF16)],
        compiler_params=pltpu.CompilerParams(dimension_semantics=("arbitrary",),
                                             vmem_limit_bytes=VMEM_LIMIT_FFN),
        name="ffn",
    )(*[a for a, _ in operands])
    return outs[0], outs[1], (dict(zip(MIXER_MATS, outs[2:])) if convert_next else None)


TM_MIXER = 512
TM_FFN = 1024
SUB_FFN = 256
SUB_MIXER = 256


def kernel(x_prompt, x_sample, state_conv, state_pool, w_in, lnv_g, lnv_b, w_spatial, b_spatial, w_proj_a, conv_w, conv_b, w_proj_b, w_pool, pool_scale, w_proj_c, w_o, ln1_g, ln1_b, w_ff1, w_ff2, ln2_g, ln2_b):
    bp, seq, _ = x_prompt.shape
    bs = x_sample.shape[0]
    xp = x_prompt.reshape(bp * seq, D)
    xs = x_sample.reshape(bs, D)
    conv_p, pool_p, conv_s, pool_s, chunk_v_s = [], [], [], [], []
    rows = lambda a: a[:, None, :]
    p = {
        'w_in': w_in, 'lnv_g': rows(lnv_g), 'lnv_b': rows(lnv_b),
        'w_spatial': w_spatial, 'b_spatial_t': jnp.swapaxes(b_spatial, 1, 2),
        'w_spatial_d': rows(jnp.repeat(w_spatial[:, :, 0, 0], G_A, axis=1)),
        'b_spatial_d': rows(jnp.repeat(b_spatial[:, :, 0], G_A, axis=1)),
        'w_proj_a': w_proj_a, 'conv_w': conv_w, 'conv_b': rows(conv_b),
        'w_proj_b': w_proj_b, 'w_pool': w_pool, 'pool_scale': rows(pool_scale),
        'w_proj_c': w_proj_c, 'w_o': w_o, 'ln1_g': rows(ln1_g), 'ln1_b': rows(ln1_b),
        'w_ff1': w_ff1, 'w_ff2': w_ff2, 'ln2_g': rows(ln2_g), 'ln2_b': rows(ln2_b),
        'state_conv_t': jnp.swapaxes(state_conv, 1, 2),
        'state_pool': state_pool,
    }
    mats = None
    for l in range(DEPTH):
        (hs, z_new, npool_s, vn), mats = _sample_mixer(xs, mats, p, l)
        conv_s.append(jnp.concatenate([state_conv[l][:, 1:], z_new[:, None, :]], axis=1))
        pool_s.append(npool_s)
        chunk_v_s.append(vn[:, None, :])
        hp, nconv, npool, w1, w2 = _mixer_prompt(xp, bp, seq, TM_MIXER, mats, p, l)
        conv_p.append(nconv)
        pool_p.append(npool)
        xp, xs, mats = _ffn(hp, hs, TM_FFN, w1, w2, p, l, convert_next=l + 1 < DEPTH)
    return (xp.reshape(bp, seq, D), xs.reshape(bs, 1, D), jnp.stack(conv_p), jnp.stack(pool_p),
            jnp.stack(conv_s), jnp.stack(pool_s), jnp.stack(chunk_v_s))
```

```python
import functools

import jax
import jax.numpy as jnp
from jax import lax
from jax.experimental import pallas as pl
from jax.experimental.pallas import tpu as pltpu

D = 1024
N_GROUPS_A = 4
CHUNK = 128
G_A = D // N_GROUPS_A
CONV_W = 3
POOL_WINDOWS = (2, 4, 8, 16)
G_C = D // len(POOL_WINDOWS)
POOL_BUF = max(POOL_WINDOWS) - 1
D_FF = 4 * D
DEPTH = 2
ALPHA = float((2 * DEPTH) ** 0.25)
LN_EPS = 1e-5
PAST_LEN = 16384

COL_U, COL_V, COL_BG, COL_CG, COL_XB, COL_XC, COL_GATE = 0, 1, 2, 3, 4, 5, 6
N_COLS = 9
N_ROW_PIECES = 8
ROW_PIECE = D // N_ROW_PIECES

SUBLANES = 8
LANES = 128
N_LT = D // LANES
LT_G = G_C // LANES
Z_HEAD = SUBLANES
P_HEAD = 32
P_LO = 16
P_CARRY = 16
assert CONV_W - 1 <= Z_HEAD and POOL_BUF <= P_CARRY <= P_HEAD - P_LO and P_CARRY % SUBLANES == 0
assert PAST_LEN % CHUNK == 0 and PAST_LEN >= max(POOL_WINDOWS)

TM_MIXER = 512
TM_FFN = 1024
SUB_MIXER = 256
SUB_FFN = 256
VMEM_LIMIT_MIXER = 56 * 1024 * 1024
VMEM_LIMIT_FFN = 48 * 1024 * 1024

F32 = jnp.float32
BF16 = jnp.bfloat16


def _dot(a, b):
    return jnp.dot(a, b, preferred_element_type=F32)


def _layer_norm(x, g, b):
    mu = jnp.mean(x, axis=-1, keepdims=True)
    xc = x - mu
    var = jnp.mean(xc * xc, axis=-1, keepdims=True)
    return xc * lax.rsqrt(var + LN_EPS) * g + b


def _mixer_prompt_kernel(x_ref, win_ref, lnvg_ref, lnvb_ref, wsp_ref, bsp_ref, wpa_ref, cw_ref, cb_ref,
                         wpb_ref, wpool_ref, psc_ref, wpc_ref, wo_ref, l1g_ref, l1b_ref, wf1_ref, wf2_ref,
                         h_ref, nconv_ref, npool_ref, wf1o_ref, wf2o_ref,
                         xb_s, vn_s, u_s, ha_s, hb_s, hc_s, acc_s, z_s, p_s, la_s, lb_s, lc_s, *, tm, sub):
    t = pl.program_id(1)

    @pl.when(t == 0)
    def _():
        z_s[:, 0:Z_HEAD, :] = jnp.zeros((N_LT, Z_HEAD, LANES), F32)
        p_s[:, 0:P_HEAD, :] = jnp.zeros((N_LT, P_HEAD, LANES), F32)
        la_s[:, 0:P_LO, :] = jnp.zeros((N_LT, P_LO, LANES), F32)
        lb_s[:, 0:P_LO, :] = jnp.zeros((N_LT - LT_G, P_LO, LANES), F32)
        lc_s[:, 0:P_LO, :] = jnp.zeros((N_LT - 2 * LT_G, P_LO, LANES), F32)

    row = lax.broadcasted_iota(jnp.int32, (CHUNK, CHUNK), 0)
    col = lax.broadcasted_iota(jnp.int32, (CHUNK, CHUNK), 1)
    causal = col <= row
    w_spatial = [jnp.where(causal, wsp_ref[g], 0.0).astype(BF16) for g in range(N_GROUPS_A)]
    cw = cw_ref[...]
    psc = psc_ref[...]

    for r0 in range(0, tm, sub):
        rs = slice(r0, r0 + sub)
        xb_s[rs, :] = x_ref[rs, :].astype(BF16)

        def proj(k):
            return _dot(xb_s[rs, :], win_ref[:, k * D:(k + 1) * D])

        xc = proj(COL_XC)
        p0, p1 = P_HEAD + r0, P_HEAD + r0 + sub
        for j in range(N_LT):
            p_s[j, p0:p1, :] = xc[:, j * LANES:(j + 1) * LANES]
        lo = P_LO if r0 == 0 else p0
        for j in range(N_LT):
            la_s[j, lo:p1, :] = p_s[j, lo:p1, :] + p_s[j, lo - 1:p1 - 1, :]
        for j in range(N_LT - LT_G):
            lb_s[j, lo:p1, :] = la_s[j + LT_G, lo:p1, :] + la_s[j + LT_G, lo - 2:p1 - 2, :]
        for j in range(N_LT - 2 * LT_G):
            lc_s[j, lo:p1, :] = lb_s[j + LT_G, lo:p1, :] + lb_s[j + LT_G, lo - 4:p1 - 4, :]
        sum16 = [lc_s[j + LT_G, p0:p1, :] + lc_s[j + LT_G, p0 - 8:p1 - 8, :] for j in range(LT_G)]
        sums = tuple(jnp.concatenate(tiles, axis=-1) for tiles in (
            [la_s[j, p0:p1, :] for j in range(LT_G)], [lb_s[j, p0:p1, :] for j in range(LT_G)],
            [lc_s[j, p0:p1, :] for j in range(LT_G)], sum16))
        pos1 = t * tm + r0 + lax.broadcasted_iota(jnp.int32, (sub, 1), 0) + 1
        for g, w in enumerate(POOL_WINDOWS):
            cs = slice(g * G_C, (g + 1) * G_C)
            inv_cnt = 1.0 / jnp.minimum(pos1, w).astype(F32)
            d = sums[g] * inv_cnt - xc[:, cs]
            hc_s[rs, cs] = (_dot(d.astype(BF16), wpool_ref[g]) * psc[:, cs]).astype(BF16)
        acc_s[rs, :] = jax.nn.sigmoid(proj(COL_GATE + 2)) * _dot(hc_s[rs, :], wpc_ref[...])

        vn_s[rs, :] = _layer_norm(proj(COL_V), lnvg_ref[...], lnvb_ref[...]).astype(BF16)
        u_s[rs, :] = proj(COL_U)
        for g in range(N_GROUPS_A):
            bias = bsp_ref[:, g:g + 1]
            cs = slice(g * G_A, (g + 1) * G_A)
            for c0 in range(r0, r0 + sub, CHUNK):
                ch = slice(c0, c0 + CHUNK)
                s = _dot(w_spatial[g], vn_s[ch, cs]) + bias
                ha_s[ch, cs] = (u_s[ch, cs] * s).astype(BF16)
        acc_s[rs, :] += jax.nn.sigmoid(proj(COL_GATE + 0)) * _dot(ha_s[rs, :], wpa_ref[...])

        z = proj(COL_CG) * proj(COL_XB)
        z0 = Z_HEAD + r0
        for j in range(N_LT):
            z_s[j, z0:z0 + sub, :] = z[:, j * LANES:(j + 1) * LANES]
        z_m2 = jnp.concatenate([z_s[j, z0 - 2:z0 - 2 + sub, :] for j in range(N_LT)], axis=-1)
        z_m1 = jnp.concatenate([z_s[j, z0 - 1:z0 - 1 + sub, :] for j in range(N_LT)], axis=-1)
        y = cb_ref[...] + (cw[0:1] * z_m2 + cw[1:2] * z_m1 + cw[2:3] * z)
        hb_s[rs, :] = (proj(COL_BG) * y).astype(BF16)
        acc_s[rs, :] += jax.nn.sigmoid(proj(COL_GATE + 1)) * _dot(hb_s[rs, :], wpb_ref[...])

        o = _dot(acc_s[rs, :].astype(BF16), wo_ref[...])
        h_ref[rs, :] = _layer_norm(ALPHA * x_ref[rs, :] + o, l1g_ref[...], l1b_ref[...])

    tail = z_s[:, tm:tm + Z_HEAD, :]
    z_s[:, 0:Z_HEAD, :] = tail
    hist = p_s[:, tm + P_HEAD - P_CARRY:tm + P_HEAD, :]
    p_s[:, P_HEAD - P_CARRY:P_HEAD, :] = hist
    for j in range(N_LT):
        nconv_ref[:, j * LANES:(j + 1) * LANES] = z_s[j, tm + Z_HEAD - (CONV_W - 1):tm + Z_HEAD, :]
        npool_ref[:, j * LANES:(j + 1) * LANES] = p_s[j, tm + P_HEAD - POOL_BUF:tm + P_HEAD, :]

    wf1o_ref[...] = wf1_ref[...].astype(BF16)
    wf2o_ref[...] = wf2_ref[...].astype(BF16)


def _sample_mixer_kernel(x_ref, sconv_ref, sp0_ref, sp1_ref, sp2_ref, sp3_ref, win_ref, lnvg_ref, lnvb_ref, wsd_ref,
                         bsd_ref, wpa_ref, cw_ref, cb_ref, wpb_ref, wpool_ref, psc_ref, wpc_ref, wo_ref, l1g_ref,
                         l1b_ref, h_ref, z_ref, xc_ref, vn_ref, *rest, emit_bf16):
    if emit_bf16:
        wino_ref, wpao_ref, wpbo_ref, wpoolo_ref, wpco_ref, woo_ref = rest[:len(MIXER_MATS)]
    else:
        wino_ref = wpao_ref = wpbo_ref = wpoolo_ref = wpco_ref = woo_ref = None
    xb_s, proj_s, wpa_s, wpb_s, wpc_s, wo_s = rest[len(MIXER_MATS) if emit_bf16 else 0:]
    k = pl.program_id(0)

    @pl.when(k == 0)
    def _():
        xb_s[...] = x_ref[...].astype(BF16)

    @pl.when(k < N_COLS)
    def _():
        w = win_ref[...].astype(BF16)
        if emit_bf16:
            wino_ref[...] = w
        proj_s[k] = _dot(xb_s[...], w)

    @pl.when(k < N_ROW_PIECES)
    def _():
        rows = pl.ds(pl.multiple_of(k * ROW_PIECE, ROW_PIECE), ROW_PIECE)
        for src, dst, keep in ((wpa_ref, wpao_ref, wpa_s), (wpb_ref, wpbo_ref, wpb_s), (wpc_ref, wpco_ref, wpc_s),
                               (wo_ref, woo_ref, wo_s)):
            piece = src[...].astype(BF16)
            if emit_bf16:
                dst[...] = piece
            keep[rows, :] = piece

    @pl.when(k == N_COLS)
    def _():
        x = x_ref[...]
        wpa, wpb, wpc, wo = wpa_s[...], wpb_s[...], wpc_s[...], wo_s[...]
        wpool = wpool_ref[...].astype(BF16)
        if emit_bf16:
            wpoolo_ref[...] = wpool

        vn = _layer_norm(proj_s[COL_V], lnvg_ref[...], lnvb_ref[...])
        vn_ref[...] = vn
        s = wsd_ref[...] * vn + bsd_ref[...]
        ha = (proj_s[COL_U] * s).astype(BF16)
        acc = jax.nn.sigmoid(proj_s[COL_GATE + 0]) * _dot(ha, wpa)

        z = proj_s[COL_CG] * proj_s[COL_XB]
        z_ref[...] = z
        cw = cw_ref[...]
        y = cb_ref[...] + (cw[0:1] * sconv_ref[0] + cw[1:2] * sconv_ref[1] + cw[2:3] * z)
        hb = (proj_s[COL_BG] * y).astype(BF16)
        acc = acc + jax.nn.sigmoid(proj_s[COL_GATE + 1]) * _dot(hb, wpb)

        xc = proj_s[COL_XC]
        xc_ref[...] = xc
        psc = psc_ref[...]
        hc = []
        for g, (w, sp_ref) in enumerate(zip(POOL_WINDOWS, (sp0_ref, sp1_ref, sp2_ref, sp3_ref))):
            cs = slice(g * G_C, (g + 1) * G_C)
            tot = xc[:, cs]
            for j in range(w - 1):
                tot = tot + sp_ref[w - 2 - j]
            d = tot * (1.0 / w) - xc[:, cs]
            hc.append((_dot(d.astype(BF16), wpool[g]) * psc[:, cs]).astype(BF16))
        hc = jnp.concatenate(hc, axis=-1)
        acc = acc + jax.nn.sigmoid(proj_s[COL_GATE + 2]) * _dot(hc, wpc)

        o = _dot(acc.astype(BF16), wo)
        h_ref[...] = _layer_norm(ALPHA * x + o, l1g_ref[...], l1b_ref[...])


def _ffn_rows(h, hb, w1_ref, w2_ref, g_ref, b_ref):
    acc = None
    for j in range(D_FF // D):
        f = _dot(hb, w1_ref[:, j * D:(j + 1) * D])
        a = jnp.square(jnp.maximum(f, 0.0)).astype(BF16)
        c = _dot(a, w2_ref[j * D:(j + 1) * D, :])
        acc = c if acc is None else acc + c
    return _layer_norm(ALPHA * h + acc, g_ref[...], b_ref[...])


def _ffn_kernel(*refs, sub, n_cvt):
    h_ref, hs_ref, w1_ref, w2_ref, g_ref, b_ref = refs[:6]
    cvt_in = refs[6:6 + n_cvt]
    o_ref, os_ref = refs[6 + n_cvt:8 + n_cvt]
    cvt_out = refs[8 + n_cvt:8 + 2 * n_cvt]
    (hb_s,) = refs[8 + 2 * n_cvt:]

    for r0 in range(0, h_ref.shape[0], sub):
        rs = slice(r0, r0 + sub)
        hb_s[rs, :] = h_ref[rs, :].astype(BF16)
        o_ref[rs, :] = _ffn_rows(h_ref[rs, :], hb_s[rs, :], w1_ref, w2_ref, g_ref, b_ref)

    @pl.when(pl.program_id(0) == pl.num_programs(0) - 1)
    def _():
        hs = hs_ref[...]
        os_ref[...] = _ffn_rows(hs, hs.astype(BF16), w1_ref, w2_ref, g_ref, b_ref)

    for src, dst in zip(cvt_in, cvt_out):
        dst[...] = src[...].astype(BF16)


def _resident(shape):
    nd = len(shape)
    return pl.BlockSpec(shape, lambda *_: (0,) * nd, pipeline_mode=pl.Buffered(1))


def _layer_resident(a, l):
    return pl.BlockSpec((None,) + a.shape[1:], lambda *_: (l,) + (0,) * (a.ndim - 1), pipeline_mode=pl.Buffered(1))


MIXER_MATS = ('w_in', 'w_proj_a', 'w_proj_b', 'w_pool', 'w_proj_c', 'w_o')


def _mixer_prompt(x2d, batch, seq, tm, mats, p, l):
    nt = seq // tm
    nsteps = batch * nt
    row_block = pl.BlockSpec((tm, D), lambda b, t: (b * nt + t, 0))
    r1, r2 = D // nsteps, D_FF // nsteps
    lay = lambda k: (p[k], _layer_resident(p[k], l))
    mat = lambda k: (mats[k], _resident(mats[k].shape))
    operands = [(x2d, row_block), mat('w_in'), lay('lnv_g'), lay('lnv_b'), lay('w_spatial'), lay('b_spatial_t'),
                mat('w_proj_a'), lay('conv_w'), lay('conv_b'), mat('w_proj_b'), mat('w_pool'), lay('pool_scale'),
                mat('w_proj_c'), mat('w_o'), lay('ln1_g'), lay('ln1_b'),
                (p['w_ff1'], pl.BlockSpec((None, r1, D_FF), lambda b, t: (l, b * nt + t, 0))),
                (p['w_ff2'], pl.BlockSpec((None, r2, D), lambda b, t: (l, b * nt + t, 0)))]
    return pl.pallas_call(
        functools.partial(_mixer_prompt_kernel, tm=tm, sub=min(tm, SUB_MIXER)),
        grid=(batch, nt),
        in_specs=[spec for _, spec in operands],
        out_specs=[row_block,
                   pl.BlockSpec((None, CONV_W - 1, D), lambda b, t: (b, 0, 0)),
                   pl.BlockSpec((None, POOL_BUF, D), lambda b, t: (b, 0, 0)),
                   pl.BlockSpec((r1, D_FF), lambda b, t: (b * nt + t, 0)),
                   pl.BlockSpec((r2, D), lambda b, t: (b * nt + t, 0))],
        out_shape=[jax.ShapeDtypeStruct((batch * seq, D), F32),
                   jax.ShapeDtypeStruct((batch, CONV_W - 1, D), F32),
                   jax.ShapeDtypeStruct((batch, POOL_BUF, D), F32),
                   jax.ShapeDtypeStruct((D, D_FF), BF16),
                   jax.ShapeDtypeStruct((D_FF, D), BF16)],
        scratch_shapes=[pltpu.VMEM((tm, D), BF16),
                        pltpu.VMEM((tm, D), BF16),
                        pltpu.VMEM((tm, D), F32),
                        pltpu.VMEM((tm, D), BF16),
                        pltpu.VMEM((tm, D), BF16),
                        pltpu.VMEM((tm, D), BF16),
                        pltpu.VMEM((tm, D), F32),
                        pltpu.VMEM((N_LT, Z_HEAD + tm, LANES), F32),
                        pltpu.VMEM((N_LT, P_HEAD + tm, LANES), F32),
                        pltpu.VMEM((N_LT, P_HEAD + tm, LANES), F32),
                        pltpu.VMEM((N_LT - LT_G, P_HEAD + tm, LANES), F32),
                        pltpu.VMEM((N_LT - 2 * LT_G, P_HEAD + tm, LANES), F32)],
        compiler_params=pltpu.CompilerParams(dimension_semantics=("arbitrary", "arbitrary"),
                                             vmem_limit_bytes=VMEM_LIMIT_MIXER),
        name="mixer_prompt",
    )(*[a for a, _ in operands])


def _sample_mixer(x2d, mats, p, l):
    n = x2d.shape[0]
    emit_bf16 = mats is None
    lay = lambda k: (p[k], _layer_resident(p[k], l))
    col_block = lambda k: (0, jnp.minimum(k, N_COLS - 1))
    row_piece = lambda k: (jnp.minimum(k, N_ROW_PIECES - 1), 0)
    if emit_bf16:
        w_in = (p['w_in'], pl.BlockSpec((None, D, D), lambda k: (l,) + col_block(k)))
        sq = lambda name: (p[name], pl.BlockSpec((None, ROW_PIECE, D), lambda k: (l,) + row_piece(k)))
        w_pool = lay('w_pool')
    else:
        w_in = (mats['w_in'], pl.BlockSpec((D, D), col_block))
        sq = lambda name: (mats[name], pl.BlockSpec((ROW_PIECE, D), row_piece))
        w_pool = (mats['w_pool'], _resident(mats['w_pool'].shape))
    operands = [(x2d, _resident(x2d.shape)), lay('state_conv_t'), lay('state_pool_t0'), lay('state_pool_t1'),
                lay('state_pool_t2'), lay('state_pool_t3'), w_in,
                lay('lnv_g'), lay('lnv_b'), lay('w_spatial_d'), lay('b_spatial_d'), sq('w_proj_a'), lay('conv_w'),
                lay('conv_b'), sq('w_proj_b'), w_pool, lay('pool_scale'), sq('w_proj_c'), sq('w_o'),
                lay('ln1_g'), lay('ln1_b')]
    once = lambda shape: pl.BlockSpec(shape, lambda k: (0,) * len(shape), pipeline_mode=pl.Buffered(1))
    out_specs = [once((n, D))] * 4
    out_shape = [jax.ShapeDtypeStruct((n, D), F32)] * 4
    if emit_bf16:
        sq_out = pl.BlockSpec((ROW_PIECE, D), row_piece)
        out_specs += [pl.BlockSpec((D, D), col_block), sq_out, sq_out, once(p['w_pool'].shape[1:]), sq_out, sq_out]
        out_shape += [jax.ShapeDtypeStruct(p[k].shape[1:], BF16) for k in MIXER_MATS]
    outs = pl.pallas_call(
        functools.partial(_sample_mixer_kernel, emit_bf16=emit_bf16),
        grid=(N_COLS + 1,),
        in_specs=[spec for _, spec in operands],
        out_specs=out_specs,
        out_shape=out_shape,
        scratch_shapes=[pltpu.VMEM((n, D), BF16),
                        pltpu.VMEM((N_COLS, n, D), F32)]
                       + [pltpu.VMEM((D, D), BF16)] * 4,
        compiler_params=pltpu.CompilerParams(dimension_semantics=("arbitrary",),
                                             vmem_limit_bytes=VMEM_LIMIT_MIXER),
        name="sample_mixer",
    )(*[a for a, _ in operands])
    return outs[:4], (dict(zip(MIXER_MATS, outs[4:])) if emit_bf16 else mats)


def _ffn(h2d, hs2d, tm, w1, w2, p, l, convert_next):
    n = h2d.shape[0]
    nsteps = n // tm
    row_block = pl.BlockSpec((tm, D), lambda i: (i, 0))
    lay = lambda k: (p[k], _layer_resident(p[k], l))
    operands = [(h2d, row_block), (hs2d, _resident(hs2d.shape)), (w1, _resident(w1.shape)), (w2, _resident(w2.shape)),
                lay('ln2_g'), lay('ln2_b')]
    out_specs = [row_block, pl.BlockSpec(hs2d.shape, lambda i: (0, 0))]
    out_shape = [jax.ShapeDtypeStruct((n, D), F32), jax.ShapeDtypeStruct(hs2d.shape, F32)]
    n_cvt = len(MIXER_MATS) if convert_next else 0
    if convert_next:
        for k in MIXER_MATS:
            a = p[k]
            r = a.shape[-2] // nsteps
            lead = a.ndim - 3
            blk = a.shape[1:-2] + (r, a.shape[-1])
            operands.append((a, pl.BlockSpec((None,) + blk, lambda i, lead=lead: (l + 1,) + (0,) * lead + (i, 0))))
            out_specs.append(pl.BlockSpec(blk, lambda i, lead=lead: (0,) * lead + (i, 0)))
            out_shape.append(jax.ShapeDtypeStruct(a.shape[1:], BF16))
    outs = pl.pallas_call(
        functools.partial(_ffn_kernel, sub=min(tm, SUB_FFN), n_cvt=n_cvt),
        grid=(nsteps,),
        in_specs=[spec for _, spec in operands],
        out_specs=out_specs,
        out_shape=out_shape,
        scratch_shapes=[pltpu.VMEM((tm, D), BF16)],
        compiler_params=pltpu.CompilerParams(dimension_semantics=("arbitrary",),
                                             vmem_limit_bytes=VMEM_LIMIT_FFN),
        name="ffn",
    )(*[a for a, _ in operands])
    return outs[0], outs[1], (dict(zip(MIXER_MATS, outs[2:])) if convert_next else None)


def kernel(x_prompt, x_sample, state_conv, state_pool, w_in, lnv_g, lnv_b, w_spatial, b_spatial, w_proj_a, conv_w, conv_b, w_proj_b, w_pool, pool_scale, w_proj_c, w_o, ln1_g, ln1_b, w_ff1, w_ff2, ln2_g, ln2_b):
    bp, seq, _ = x_prompt.shape
    bs = x_sample.shape[0]
    xp = x_prompt.reshape(bp * seq, D)
    xs = x_sample.reshape(bs, D)
    conv_p, pool_p, conv_s, pool_s, chunk_v_s = [], [], [], [], []
    rows = lambda a: a[:, None, :]
    p = {
        'w_in': w_in, 'lnv_g': rows(lnv_g), 'lnv_b': rows(lnv_b),
        'w_spatial': w_spatial, 'b_spatial_t': jnp.swapaxes(b_spatial, 1, 2),
        'w_spatial_d': rows(jnp.repeat(w_spatial[:, :, 0, 0], G_A, axis=1)),
        'b_spatial_d': rows(jnp.repeat(b_spatial[:, :, 0], G_A, axis=1)),
        'w_proj_a': w_proj_a, 'conv_w': conv_w, 'conv_b': rows(conv_b),
        'w_proj_b': w_proj_b, 'w_pool': w_pool, 'pool_scale': rows(pool_scale),
        'w_proj_c': w_proj_c, 'w_o': w_o, 'ln1_g': rows(ln1_g), 'ln1_b': rows(ln1_b),
        'w_ff1': w_ff1, 'w_ff2': w_ff2, 'ln2_g': rows(ln2_g), 'ln2_b': rows(ln2_b),
        'state_conv_t': jnp.swapaxes(state_conv, 1, 2),
    }
    for g, w in enumerate(POOL_WINDOWS):
        p['state_pool_t%d' % g] = jnp.swapaxes(state_pool[:, :, POOL_BUF - (w - 1):, g * G_C:(g + 1) * G_C], 1, 2)
    mats = None
    for l in range(DEPTH):
        (hs, z_new, xc_new, vn), mats = _sample_mixer(xs, mats, p, l)
        conv_s.append(jnp.concatenate([state_conv[l][:, 1:], z_new[:, None, :]], axis=1))
        pool_s.append(jnp.concatenate([state_pool[l][:, 1:], xc_new[:, None, :]], axis=1))
        chunk_v_s.append(vn[:, None, :])
        hp, nconv, npool, w1, w2 = _mixer_prompt(xp, bp, seq, TM_MIXER, mats, p, l)
        conv_p.append(nconv)
        pool_p.append(npool)
        xp, xs, mats = _ffn(hp, hs, TM_FFN, w1, w2, p, l, convert_next=l + 1 < DEPTH)
    return (xp.reshape(bp, seq, D), xs.reshape(bs, 1, D), jnp.stack(conv_p), jnp.stack(pool_p),
            jnp.stack(conv_s), jnp.stack(pool_s), jnp.stack(chunk_v_s))
```

```python
import functools

import jax
import jax.numpy as jnp
from jax import lax
from jax.experimental import pallas as pl
from jax.experimental.pallas import tpu as pltpu

D = 1024
N_GROUPS_A = 4
CHUNK = 128
G_A = D // N_GROUPS_A
CONV_W = 3
POOL_WINDOWS = (2, 4, 8, 16)
G_C = D // len(POOL_WINDOWS)
POOL_BUF = max(POOL_WINDOWS) - 1
D_FF = 4 * D
DEPTH = 2
ALPHA = float((2 * DEPTH) ** 0.25)
LN_EPS = 1e-5
PAST_LEN = 16384

COL_U, COL_V, COL_BG, COL_CG, COL_XB, COL_XC, COL_GATE = 0, 1, 2, 3, 4, 5, 6
N_COLS = 9
N_ROW_PIECES = 8
ROW_PIECE = D // N_ROW_PIECES

SUBLANES = 8
LANES = 128
N_LT = D // LANES
LT_G = G_C // LANES
Z_HEAD = SUBLANES
P_HEAD = 32
P_LO = 16
P_CARRY = 16
assert CONV_W - 1 <= Z_HEAD and POOL_BUF <= P_CARRY <= P_HEAD - P_LO and P_CARRY % SUBLANES == 0
assert PAST_LEN % CHUNK == 0 and PAST_LEN >= max(POOL_WINDOWS)

TM_MIXER = 512
TM_FFN = 1024
SUB_MIXER = 256
SUB_FFN = 256
VMEM_LIMIT_MIXER = 56 * 1024 * 1024
VMEM_LIMIT_FFN = 48 * 1024 * 1024

F32 = jnp.float32
BF16 = jnp.bfloat16


def _dot(a, b):
    return jnp.dot(a, b, preferred_element_type=F32)


def _layer_norm(x, g, b):
    mu = jnp.mean(x, axis=-1, keepdims=True)
    xc = x - mu
    var = jnp.mean(xc * xc, axis=-1, keepdims=True)
    return xc * lax.rsqrt(var + LN_EPS) * g + b


def _mixer_prompt_kernel(x_ref, win_ref, lnvg_ref, lnvb_ref, wsp_ref, bsp_ref, wpa_ref, cw_ref, cb_ref,
                         wpb_ref, wpool_ref, psc_ref, wpc_ref, wo_ref, l1g_ref, l1b_ref, wf1_ref, wf2_ref,
                         h_ref, nconv_ref, npool_ref, wf1o_ref, wf2o_ref,
                         xb_s, vn_s, u_s, ha_s, hb_s, hc_s, acc_s, z_s, p_s, la_s, lb_s, lc_s, *, tm, sub):
    t = pl.program_id(1)

    @pl.when(t == 0)
    def _():
        z_s[:, 0:Z_HEAD, :] = jnp.zeros((N_LT, Z_HEAD, LANES), F32)
        p_s[:, 0:P_HEAD, :] = jnp.zeros((N_LT, P_HEAD, LANES), F32)
        la_s[:, 0:P_LO, :] = jnp.zeros((N_LT, P_LO, LANES), F32)
        lb_s[:, 0:P_LO, :] = jnp.zeros((N_LT - LT_G, P_LO, LANES), F32)
        lc_s[:, 0:P_LO, :] = jnp.zeros((N_LT - 2 * LT_G, P_LO, LANES), F32)

    row = lax.broadcasted_iota(jnp.int32, (CHUNK, CHUNK), 0)
    col = lax.broadcasted_iota(jnp.int32, (CHUNK, CHUNK), 1)
    causal = col <= row
    w_spatial = [jnp.where(causal, wsp_ref[g], 0.0).astype(BF16) for g in range(N_GROUPS_A)]
    cw = cw_ref[...]
    psc = psc_ref[...]

    for r0 in range(0, tm, sub):
        rs = slice(r0, r0 + sub)
        xb_s[rs, :] = x_ref[rs, :].astype(BF16)

        def proj(k):
            return _dot(xb_s[rs, :], win_ref[:, k * D:(k + 1) * D])

        xc = proj(COL_XC)
        p0, p1 = P_HEAD + r0, P_HEAD + r0 + sub
        for j in range(N_LT):
            p_s[j, p0:p1, :] = xc[:, j * LANES:(j + 1) * LANES]
        lo = P_LO if r0 == 0 else p0
        for j in range(N_LT):
            la_s[j, lo:p1, :] = p_s[j, lo:p1, :] + p_s[j, lo - 1:p1 - 1, :]
        for j in range(N_LT - LT_G):
            lb_s[j, lo:p1, :] = la_s[j + LT_G, lo:p1, :] + la_s[j + LT_G, lo - 2:p1 - 2, :]
        for j in range(N_LT - 2 * LT_G):
            lc_s[j, lo:p1, :] = lb_s[j + LT_G, lo:p1, :] + lb_s[j + LT_G, lo - 4:p1 - 4, :]
        sum16 = [lc_s[j + LT_G, p0:p1, :] + lc_s[j + LT_G, p0 - 8:p1 - 8, :] for j in range(LT_G)]
        sums = tuple(jnp.concatenate(tiles, axis=-1) for tiles in (
            [la_s[j, p0:p1, :] for j in range(LT_G)], [lb_s[j, p0:p1, :] for j in range(LT_G)],
            [lc_s[j, p0:p1, :] for j in range(LT_G)], sum16))
        pos1 = t * tm + r0 + lax.broadcasted_iota(jnp.int32, (sub, 1), 0) + 1
        for g, w in enumerate(POOL_WINDOWS):
            cs = slice(g * G_C, (g + 1) * G_C)
            inv_cnt = 1.0 / jnp.minimum(pos1, w).astype(F32)
            d = sums[g] * inv_cnt - xc[:, cs]
            hc_s[rs, cs] = (_dot(d.astype(BF16), wpool_ref[g]) * psc[:, cs]).astype(BF16)
        acc_s[rs, :] = jax.nn.sigmoid(proj(COL_GATE + 2)) * _dot(hc_s[rs, :], wpc_ref[...])

        vn_s[rs, :] = _layer_norm(proj(COL_V), lnvg_ref[...], lnvb_ref[...]).astype(BF16)
        u_s[rs, :] = proj(COL_U)
        for g in range(N_GROUPS_A):
            bias = bsp_ref[:, g:g + 1]
            cs = slice(g * G_A, (g + 1) * G_A)
            for c0 in range(r0, r0 + sub, CHUNK):
                ch = slice(c0, c0 + CHUNK)
                s = _dot(w_spatial[g], vn_s[ch, cs]) + bias
                ha_s[ch, cs] = (u_s[ch, cs] * s).astype(BF16)
        acc_s[rs, :] += jax.nn.sigmoid(proj(COL_GATE + 0)) * _dot(ha_s[rs, :], wpa_ref[...])

        z = proj(COL_CG) * proj(COL_XB)
        z0 = Z_HEAD + r0
        for j in range(N_LT):
            z_s[j, z0:z0 + sub, :] = z[:, j * LANES:(j + 1) * LANES]
        z_m2 = jnp.concatenate([z_s[j, z0 - 2:z0 - 2 + sub, :] for j in range(N_LT)], axis=-1)
        z_m1 = jnp.concatenate([z_s[j, z0 - 1:z0 - 1 + sub, :] for j in range(N_LT)], axis=-1)
        y = cb_ref[...] + (cw[0:1] * z_m2 + cw[1:2] * z_m1 + cw[2:3] * z)
        hb_s[rs, :] = (proj(COL_BG) * y).astype(BF16)
        acc_s[rs, :] += jax.nn.sigmoid(proj(COL_GATE + 1)) * _dot(hb_s[rs, :], wpb_ref[...])

        o = _dot(acc_s[rs, :].astype(BF16), wo_ref[...])
        h_ref[rs, :] = _layer_norm(ALPHA * x_ref[rs, :] + o, l1g_ref[...], l1b_ref[...])

    tail = z_s[:, tm:tm + Z_HEAD, :]
    z_s[:, 0:Z_HEAD, :] = tail
    hist = p_s[:, tm + P_HEAD - P_CARRY:tm + P_HEAD, :]
    p_s[:, P_HEAD - P_CARRY:P_HEAD, :] = hist
    for j in range(N_LT):
        nconv_ref[:, j * LANES:(j + 1) * LANES] = z_s[j, tm + Z_HEAD - (CONV_W - 1):tm + Z_HEAD, :]
        npool_ref[:, j * LANES:(j + 1) * LANES] = p_s[j, tm + P_HEAD - POOL_BUF:tm + P_HEAD, :]

    wf1o_ref[...] = wf1_ref[...].astype(BF16)
    wf2o_ref[...] = wf2_ref[...].astype(BF16)


def _sample_mixer_kernel(x_ref, sconv_ref, sp0_ref, sp1_ref, sp2_ref, sp3_ref, win_ref, lnvg_ref, lnvb_ref, wsd_ref,
                         bsd_ref, wpa_ref, cw_ref, cb_ref, wpb_ref, wpool_ref, psc_ref, wpc_ref, wo_ref, l1g_ref,
                         l1b_ref, h_ref, z_ref, xc_ref, vn_ref, *rest, emit_bf16):
    if emit_bf16:
        wino_ref, wpao_ref, wpbo_ref, wpoolo_ref, wpco_ref, woo_ref = rest[:len(MIXER_MATS)]
    else:
        wino_ref = wpao_ref = wpbo_ref = wpoolo_ref = wpco_ref = woo_ref = None
    xb_s, proj_s, wpa_s, wpb_s, wpc_s, wo_s = rest[len(MIXER_MATS) if emit_bf16 else 0:]
    k = pl.program_id(0)

    @pl.when(k == 0)
    def _():
        xb_s[...] = x_ref[...].astype(BF16)

    @pl.when(k < N_COLS)
    def _():
        w = win_ref[...].astype(BF16)
        if emit_bf16:
            wino_ref[...] = w
        proj_s[k] = _dot(xb_s[...], w)

    @pl.when(k < N_ROW_PIECES)
    def _():
        rows = pl.ds(pl.multiple_of(k * ROW_PIECE, ROW_PIECE), ROW_PIECE)
        for src, dst, keep in ((wpa_ref, wpao_ref, wpa_s), (wpb_ref, wpbo_ref, wpb_s), (wpc_ref, wpco_ref, wpc_s),
                               (wo_ref, woo_ref, wo_s)):
            piece = src[...].astype(BF16)
            if emit_bf16:
                dst[...] = piece
            keep[rows, :] = piece

    @pl.when(k == N_COLS)
    def _():
        x = x_ref[...]
        wpa, wpb, wpc, wo = wpa_s[...], wpb_s[...], wpc_s[...], wo_s[...]
        wpool = wpool_ref[...].astype(BF16)
        if emit_bf16:
            wpoolo_ref[...] = wpool

        vn = _layer_norm(proj_s[COL_V], lnvg_ref[...], lnvb_ref[...])
        vn_ref[...] = vn
        s = wsd_ref[...] * vn + bsd_ref[...]
        ha = (proj_s[COL_U] * s).astype(BF16)
        acc = jax.nn.sigmoid(proj_s[COL_GATE + 0]) * _dot(ha, wpa)

        z = proj_s[COL_CG] * proj_s[COL_XB]
        z_ref[...] = z
        cw = cw_ref[...]
        y = cb_ref[...] + (cw[0:1] * sconv_ref[0] + cw[1:2] * sconv_ref[1] + cw[2:3] * z)
        hb = (proj_s[COL_BG] * y).astype(BF16)
        acc = acc + jax.nn.sigmoid(proj_s[COL_GATE + 1]) * _dot(hb, wpb)

        xc = proj_s[COL_XC]
        xc_ref[...] = xc
        psc = psc_ref[...]
        hc = []
        for g, (w, sp_ref) in enumerate(zip(POOL_WINDOWS, (sp0_ref, sp1_ref, sp2_ref, sp3_ref))):
            cs = slice(g * G_C, (g + 1) * G_C)
            tot = xc[:, cs]
            for j in range(w - 1):
                tot = tot + sp_ref[w - 2 - j]
            d = tot * (1.0 / w) - xc[:, cs]
            hc.append((_dot(d.astype(BF16), wpool[g]) * psc[:, cs]).astype(BF16))
        hc = jnp.concatenate(hc, axis=-1)
        acc = acc + jax.nn.sigmoid(proj_s[COL_GATE + 2]) * _dot(hc, wpc)

        o = _dot(acc.astype(BF16), wo)
        h_ref[...] = _layer_norm(ALPHA * x + o, l1g_ref[...], l1b_ref[...])


def _ffn_rows(h, hb, w1_ref, w2_ref, g_ref, b_ref):
    acc = None
    for j in range(D_FF // D):
        f = _dot(hb, w1_ref[:, j * D:(j + 1) * D])
        a = jnp.square(jnp.maximum(f, 0.0)).astype(BF16)
        c = _dot(a, w2_ref[j * D:(j + 1) * D, :])
        acc = c if acc is None else acc + c
    return _layer_norm(ALPHA * h + acc, g_ref[...], b_ref[...])


def _ffn_kernel(*refs, sub, n_cvt):
    h_ref, hs_ref, w1_ref, w2_ref, g_ref, b_ref = refs[:6]
    cvt_in = refs[6:6 + n_cvt]
    o_ref, os_ref = refs[6 + n_cvt:8 + n_cvt]
    cvt_out = refs[8 + n_cvt:8 + 2 * n_cvt]
    (hb_s,) = refs[8 + 2 * n_cvt:]

    for r0 in range(0, h_ref.shape[0], sub):
        rs = slice(r0, r0 + sub)
        hb_s[rs, :] = h_ref[rs, :].astype(BF16)
        o_ref[rs, :] = _ffn_rows(h_ref[rs, :], hb_s[rs, :], w1_ref, w2_ref, g_ref, b_ref)

    @pl.when(pl.program_id(0) == pl.num_programs(0) - 1)
    def _():
        hs = hs_ref[...]
        os_ref[...] = _ffn_rows(hs, hs.astype(BF16), w1_ref, w2_ref, g_ref, b_ref)

    for src, dst in zip(cvt_in, cvt_out):
        dst[...] = src[...].astype(BF16)


def _resident(shape):
    nd = len(shape)
    return pl.BlockSpec(shape, lambda *_: (0,) * nd, pipeline_mode=pl.Buffered(1))


def _layer_resident(a, l):
    return pl.BlockSpec((None,) + a.shape[1:], lambda *_: (l,) + (0,) * (a.ndim - 1), pipeline_mode=pl.Buffered(1))


MIXER_MATS = ('w_in', 'w_proj_a', 'w_proj_b', 'w_pool', 'w_proj_c', 'w_o')


def _mixer_prompt(x2d, batch, seq, tm, mats, p, l):
    nt = seq // tm
    nsteps = batch * nt
    row_block = pl.BlockSpec((tm, D), lambda b, t: (b * nt + t, 0))
    r1, r2 = D // nsteps, D_FF // nsteps
    lay = lambda k: (p[k], _layer_resident(p[k], l))
    mat = lambda k: (mats[k], _resident(mats[k].shape))
    operands = [(x2d, row_block), mat('w_in'), lay('lnv_g'), lay('lnv_b'), lay('w_spatial'), lay('b_spatial_t'),
                mat('w_proj_a'), lay('conv_w'), lay('conv_b'), mat('w_proj_b'), mat('w_pool'), lay('pool_scale'),
                mat('w_proj_c'), mat('w_o'), lay('ln1_g'), lay('ln1_b'),
                (p['w_ff1'], pl.BlockSpec((None, r1, D_FF), lambda b, t: (l, b * nt + t, 0))),
                (p['w_ff2'], pl.BlockSpec((None, r2, D), lambda b, t: (l, b * nt + t, 0)))]
    return pl.pallas_call(
        functools.partial(_mixer_prompt_kernel, tm=tm, sub=min(tm, SUB_MIXER)),
        grid=(batch, nt),
        in_specs=[spec for _, spec in operands],
        out_specs=[row_block,
                   pl.BlockSpec((None, CONV_W - 1, D), lambda b, t: (b, 0, 0)),
                   pl.BlockSpec((None, POOL_BUF, D), lambda b, t: (b, 0, 0)),
                   pl.BlockSpec((r1, D_FF), lambda b, t: (b * nt + t, 0)),
                   pl.BlockSpec((r2, D), lambda b, t: (b * nt + t, 0))],
        out_shape=[jax.ShapeDtypeStruct((batch * seq, D), F32),
                   jax.ShapeDtypeStruct((batch, CONV_W - 1, D), F32),
                   jax.ShapeDtypeStruct((batch, POOL_BUF, D), F32),
                   jax.ShapeDtypeStruct((D, D_FF), BF16),
                   jax.ShapeDtypeStruct((D_FF, D), BF16)],
        scratch_shapes=[pltpu.VMEM((tm, D), BF16),
                        pltpu.VMEM((tm, D), BF16),
                        pltpu.VMEM((tm, D), F32),
                        pltpu.VMEM((tm, D), BF16),
                        pltpu.VMEM((tm, D), BF16),
                        pltpu.VMEM((tm, D), BF16),
                        pltpu.VMEM((tm, D), F32),
                        pltpu.VMEM((N_LT, Z_HEAD + tm, LANES), F32),
                        pltpu.VMEM((N_LT, P_HEAD + tm, LANES), F32),
                        pltpu.VMEM((N_LT, P_HEAD + tm, LANES), F32),
                        pltpu.VMEM((N_LT - LT_G, P_HEAD + tm, LANES), F32),
                        pltpu.VMEM((N_LT - 2 * LT_G, P_HEAD + tm, LANES), F32)],
        compiler_params=pltpu.CompilerParams(dimension_semantics=("arbitrary", "arbitrary"),
                                             vmem_limit_bytes=VMEM_LIMIT_MIXER),
        name="mixer_prompt",
    )(*[a for a, _ in operands])


def _sample_mixer(x2d, mats, p, l):
    n = x2d.shape[0]
    emit_bf16 = mats is None
    lay = lambda k: (p[k], _layer_resident(p[k], l))
    col_block = lambda k: (0, jnp.minimum(k, N_COLS - 1))
    row_piece = lambda k: (jnp.minimum(k, N_ROW_PIECES - 1), 0)
    if emit_bf16:
        w_in = (p['w_in'], pl.BlockSpec((None, D, D), lambda k: (l,) + col_block(k)))
        sq = lambda name: (p[name], pl.BlockSpec((None, ROW_PIECE, D), lambda k: (l,) + row_piece(k)))
        w_pool = lay('w_pool')
    else:
        w_in = (mats['w_in'], pl.BlockSpec((D, D), col_block))
        sq = lambda name: (mats[name], pl.BlockSpec((ROW_PIECE, D), row_piece))
        w_pool = (mats['w_pool'], _resident(mats['w_pool'].shape))
    operands = [(x2d, _resident(x2d.shape)), lay('state_conv_t'), lay('state_pool_t0'), lay('state_pool_t1'),
                lay('state_pool_t2'), lay('state_pool_t3'), w_in,
                lay('lnv_g'), lay('lnv_b'), lay('w_spatial_d'), lay('b_spatial_d'), sq('w_proj_a'), lay('conv_w'),
                lay('conv_b'), sq('w_proj_b'), w_pool, lay('pool_scale'), sq('w_proj_c'), sq('w_o'),
                lay('ln1_g'), lay('ln1_b')]
    once = lambda shape: pl.BlockSpec(shape, lambda k: (0,) * len(shape), pipeline_mode=pl.Buffered(1))
    out_specs = [once((n, D))] * 4
    out_shape = [jax.ShapeDtypeStruct((n, D), F32)] * 4
    if emit_bf16:
        sq_out = pl.BlockSpec((ROW_PIECE, D), row_piece)
        out_specs += [pl.BlockSpec((D, D), col_block), sq_out, sq_out, once(p['w_pool'].shape[1:]), sq_out, sq_out]
        out_shape += [jax.ShapeDtypeStruct(p[k].shape[1:], BF16) for k in MIXER_MATS]
    outs = pl.pallas_call(
        functools.partial(_sample_mixer_kernel, emit_bf16=emit_bf16),
        grid=(N_COLS + 1,),
        in_specs=[spec for _, spec in operands],
        out_specs=out_specs,
        out_shape=out_shape,
        scratch_shapes=[pltpu.VMEM((n, D), BF16),
                        pltpu.VMEM((N_COLS, n, D), F32)]
                       + [pltpu.VMEM((D, D), BF16)] * 4,
        compiler_params=pltpu.CompilerParams(dimension_semantics=("arbitrary",),
                                             vmem_limit_bytes=VMEM_LIMIT_MIXER),
        name="sample_mixer",
    )(*[a for a, _ in operands])
    return outs[:4], (dict(zip(MIXER_MATS, outs[4:])) if emit_bf16 else mats)


def _ffn(h2d, hs2d, tm, w1, w2, p, l, convert_next):
    n = h2d.shape[0]
    nsteps = n // tm
    row_block = pl.BlockSpec((tm, D), lambda i: (i, 0))
    lay = lambda k: (p[k], _layer_resident(p[k], l))
    operands = [(h2d, row_block), (hs2d, _resident(hs2d.shape)), (w1, _resident(w1.shape)), (w2, _resident(w2.shape)),
                lay('ln2_g'), lay('ln2_b')]
    out_specs = [row_block, pl.BlockSpec(hs2d.shape, lambda i: (0, 0))]
    out_shape = [jax.ShapeDtypeStruct((n, D), F32), jax.ShapeDtypeStruct(hs2d.shape, F32)]
    n_cvt = len(MIXER_MATS) if convert_next else 0
    if convert_next:
        for k in MIXER_MATS:
            a = p[k]
            r = a.shape[-2] // nsteps
            lead = a.ndim - 3
            blk = a.shape[1:-2] + (r, a.shape[-1])
            operands.append((a, pl.BlockSpec((None,) + blk, lambda i, lead=lead: (l + 1,) + (0,) * lead + (i, 0))))
            out_specs.append(pl.BlockSpec(blk, lambda i, lead=lead: (0,) * lead + (i, 0)))
            out_shape.append(jax.ShapeDtypeStruct(a.shape[1:], BF16))
    outs = pl.pallas_call(
        functools.partial(_ffn_kernel, sub=min(tm, SUB_FFN), n_cvt=n_cvt),
        grid=(nsteps,),
        in_specs=[spec for _, spec in operands],
        out_specs=out_specs,
        out_shape=out_shape,
        scratch_shapes=[pltpu.VMEM((tm, D), BF16)],
        compiler_params=pltpu.CompilerParams(dimension_semantics=("arbitrary",),
                                             vmem_limit_bytes=VMEM_LIMIT_FFN),
        name="ffn",
    )(*[a for a, _ in operands])
    return outs[0], outs[1], (dict(zip(MIXER_MATS, outs[2:])) if convert_next else None)


def kernel(x_prompt, x_sample, state_conv, state_pool, w_in, lnv_g, lnv_b, w_spatial, b_spatial, w_proj_a, conv_w, conv_b, w_proj_b, w_pool, pool_scale, w_proj_c, w_o, ln1_g, ln1_b, w_ff1, w_ff2, ln2_g, ln2_b):
    bp, seq, _ = x_prompt.shape
    bs = x_sample.shape[0]
    xp = x_prompt.reshape(bp * seq, D)
    xs = x_sample.reshape(bs, D)
    conv_p, pool_p, conv_s, pool_s, chunk_v_s = [], [], [], [], []
    rows = lambda a: a[:, None, :]
    p = {
        'w_in': w_in, 'lnv_g': rows(lnv_g), 'lnv_b': rows(lnv_b),
        'w_spatial': w_spatial, 'b_spatial_t': jnp.swapaxes(b_spatial, 1, 2),
        'w_spatial_d': rows(jnp.repeat(w_spatial[:, :, 0, 0], G_A, axis=1)),
        'b_spatial_d': rows(jnp.repeat(b_spatial[:, :, 0], G_A, axis=1)),
        'w_proj_a': w_proj_a, 'conv_w': conv_w, 'conv_b': rows(conv_b),
        'w_proj_b': w_proj_b, 'w_pool': w_pool, 'pool_scale': rows(pool_scale),
        'w_proj_c': w_proj_c, 'w_o': w_o, 'ln1_g': rows(ln1_g), 'ln1_b': rows(ln1_b),
        'w_ff1': w_ff1, 'w_ff2': w_ff2, 'ln2_g': rows(ln2_g), 'ln2_b': rows(ln2_b),
        'state_conv_t': jnp.swapaxes(state_conv, 1, 2),
    }
    for g, w in enumerate(POOL_WINDOWS):
        p['state_pool_t%d' % g] = jnp.swapaxes(state_pool[:, :, POOL_BUF - (w - 1):, g * G_C:(g + 1) * G_C], 1, 2)
    mats = None
    for l in range(DEPTH):
        (hs, z_new, xc_new, vn), mats = _sample_mixer(xs, mats, p, l)
        conv_s.append(z_new)
        pool_s.append(xc_new)
        chunk_v_s.append(vn)
        hp, nconv, npool, w1, w2 = _mixer_prompt(xp, bp, seq, TM_MIXER, mats, p, l)
        conv_p.append(nconv)
        pool_p.append(npool)
        xp, xs, mats = _ffn(hp, hs, TM_FFN, w1, w2, p, l, convert_next=l + 1 < DEPTH)
    new_rows = lambda rows_per_layer: jnp.stack(rows_per_layer)[:, :, None, :]
    new_conv_s = jnp.concatenate([state_conv[:, :, 1:], new_rows(conv_s)], axis=2)
    new_pool_s = jnp.concatenate([state_pool[:, :, 1:], new_rows(pool_s)], axis=2)
    return (xp.reshape(bp, seq, D), xs.reshape(bs, 1, D), jnp.stack(conv_p), jnp.stack(pool_p),
            new_conv_s, new_pool_s, new_rows(chunk_v_s))
```

```python
import functools

import jax
import jax.numpy as jnp
from jax import lax
from jax.experimental import pallas as pl
from jax.experimental.pallas import tpu as pltpu

D = 1024
N_GROUPS_A = 4
CHUNK = 128
G_A = D // N_GROUPS_A
CONV_W = 3
POOL_WINDOWS = (2, 4, 8, 16)
G_C = D // len(POOL_WINDOWS)
POOL_BUF = max(POOL_WINDOWS) - 1
D_FF = 4 * D
DEPTH = 2
ALPHA = float((2 * DEPTH) ** 0.25)
LN_EPS = 1e-5
PAST_LEN = 16384

COL_U, COL_V, COL_BG, COL_CG, COL_XB, COL_XC, COL_GATE = 0, 1, 2, 3, 4, 5, 6
N_COLS = 9
N_ROW_PIECES = 8
ROW_PIECE = D // N_ROW_PIECES

SUBLANES = 8
LANES = 128
N_LT = D // LANES
LT_G = G_C // LANES
Z_HEAD = SUBLANES
P_HEAD = 32
P_LO = 16
P_CARRY = 16
assert CONV_W - 1 <= Z_HEAD and POOL_BUF <= P_CARRY <= P_HEAD - P_LO and P_CARRY % SUBLANES == 0
assert PAST_LEN % CHUNK == 0 and PAST_LEN >= max(POOL_WINDOWS)

TM_MIXER = 512
TM_FFN = 1024
SUB_MIXER = 256
SUB_FFN = 256
VMEM_LIMIT_MIXER = 56 * 1024 * 1024
VMEM_LIMIT_FFN = 48 * 1024 * 1024

F32 = jnp.float32
BF16 = jnp.bfloat16


def _dot(a, b):
    return jnp.dot(a, b, preferred_element_type=F32)


def _layer_norm(x, g, b):
    mu = jnp.mean(x, axis=-1, keepdims=True)
    xc = x - mu
    var = jnp.mean(xc * xc, axis=-1, keepdims=True)
    return xc * lax.rsqrt(var + LN_EPS) * g + b


def _mixer_prompt_kernel(x_ref, win_ref, lnvg_ref, lnvb_ref, wsp_ref, bsp_ref, wpa_ref, cw_ref, cb_ref,
                         wpb_ref, wpool_ref, psc_ref, wpc_ref, wo_ref, l1g_ref, l1b_ref, wf1_ref, wf2_ref,
                         h_ref, nconv_ref, npool_ref, wf1o_ref, wf2o_ref,
                         xb_s, vn_s, u_s, ha_s, hb_s, hc_s, acc_s, z_s, p_s, la_s, lb_s, lc_s, *, tm, sub):
    t = pl.program_id(1)

    @pl.when(t == 0)
    def _():
        z_s[:, 0:Z_HEAD, :] = jnp.zeros((N_LT, Z_HEAD, LANES), F32)
        p_s[:, 0:P_HEAD, :] = jnp.zeros((N_LT, P_HEAD, LANES), F32)
        la_s[:, 0:P_LO, :] = jnp.zeros((N_LT, P_LO, LANES), F32)
        lb_s[:, 0:P_LO, :] = jnp.zeros((N_LT - LT_G, P_LO, LANES), F32)
        lc_s[:, 0:P_LO, :] = jnp.zeros((N_LT - 2 * LT_G, P_LO, LANES), F32)

    row = lax.broadcasted_iota(jnp.int32, (CHUNK, CHUNK), 0)
    col = lax.broadcasted_iota(jnp.int32, (CHUNK, CHUNK), 1)
    causal = col <= row
    w_spatial = [jnp.where(causal, wsp_ref[g], 0.0).astype(BF16) for g in range(N_GROUPS_A)]
    cw = cw_ref[...]
    psc = psc_ref[...]

    for r0 in range(0, tm, sub):
        rs = slice(r0, r0 + sub)
        xb_s[rs, :] = x_ref[rs, :].astype(BF16)

        def proj(k):
            return _dot(xb_s[rs, :], win_ref[:, k * D:(k + 1) * D])

        xc = proj(COL_XC)
        p0, p1 = P_HEAD + r0, P_HEAD + r0 + sub
        for j in range(N_LT):
            p_s[j, p0:p1, :] = xc[:, j * LANES:(j + 1) * LANES]
        lo = P_LO if r0 == 0 else p0
        for j in range(N_LT):
            la_s[j, lo:p1, :] = p_s[j, lo:p1, :] + p_s[j, lo - 1:p1 - 1, :]
        for j in range(N_LT - LT_G):
            lb_s[j, lo:p1, :] = la_s[j + LT_G, lo:p1, :] + la_s[j + LT_G, lo - 2:p1 - 2, :]
        for j in range(N_LT - 2 * LT_G):
            lc_s[j, lo:p1, :] = lb_s[j + LT_G, lo:p1, :] + lb_s[j + LT_G, lo - 4:p1 - 4, :]
        sum16 = [lc_s[j + LT_G, p0:p1, :] + lc_s[j + LT_G, p0 - 8:p1 - 8, :] for j in range(LT_G)]
        sums = tuple(jnp.concatenate(tiles, axis=-1) for tiles in (
            [la_s[j, p0:p1, :] for j in range(LT_G)], [lb_s[j, p0:p1, :] for j in range(LT_G)],
            [lc_s[j, p0:p1, :] for j in range(LT_G)], sum16))
        pos1 = t * tm + r0 + lax.broadcasted_iota(jnp.int32, (sub, 1), 0) + 1
        for g, w in enumerate(POOL_WINDOWS):
            cs = slice(g * G_C, (g + 1) * G_C)
            inv_cnt = 1.0 / jnp.minimum(pos1, w).astype(F32)
            d = sums[g] * inv_cnt - xc[:, cs]
            hc_s[rs, cs] = (_dot(d.astype(BF16), wpool_ref[g]) * psc[:, cs]).astype(BF16)
        acc_s[rs, :] = jax.nn.sigmoid(proj(COL_GATE + 2)) * _dot(hc_s[rs, :], wpc_ref[...])

        vn_s[rs, :] = _layer_norm(proj(COL_V), lnvg_ref[...], lnvb_ref[...]).astype(BF16)
        u_s[rs, :] = proj(COL_U)
        for g in range(N_GROUPS_A):
            bias = bsp_ref[:, g:g + 1]
            cs = slice(g * G_A, (g + 1) * G_A)
            for c0 in range(r0, r0 + sub, CHUNK):
                ch = slice(c0, c0 + CHUNK)
                s = _dot(w_spatial[g], vn_s[ch, cs]) + bias
                ha_s[ch, cs] = (u_s[ch, cs] * s).astype(BF16)
        acc_s[rs, :] += jax.nn.sigmoid(proj(COL_GATE + 0)) * _dot(ha_s[rs, :], wpa_ref[...])

        z = proj(COL_CG) * proj(COL_XB)
        z0 = Z_HEAD + r0
        for j in range(N_LT):
            z_s[j, z0:z0 + sub, :] = z[:, j * LANES:(j + 1) * LANES]
        z_m2 = jnp.concatenate([z_s[j, z0 - 2:z0 - 2 + sub, :] for j in range(N_LT)], axis=-1)
        z_m1 = jnp.concatenate([z_s[j, z0 - 1:z0 - 1 + sub, :] for j in range(N_LT)], axis=-1)
        y = cb_ref[...] + (cw[0:1] * z_m2 + cw[1:2] * z_m1 + cw[2:3] * z)
        hb_s[rs, :] = (proj(COL_BG) * y).astype(BF16)
        acc_s[rs, :] += jax.nn.sigmoid(proj(COL_GATE + 1)) * _dot(hb_s[rs, :], wpb_ref[...])

        o = _dot(acc_s[rs, :].astype(BF16), wo_ref[...])
        h_ref[rs, :] = _layer_norm(ALPHA * x_ref[rs, :] + o, l1g_ref[...], l1b_ref[...])

    tail = z_s[:, tm:tm + Z_HEAD, :]
    z_s[:, 0:Z_HEAD, :] = tail
    hist = p_s[:, tm + P_HEAD - P_CARRY:tm + P_HEAD, :]
    p_s[:, P_HEAD - P_CARRY:P_HEAD, :] = hist
    for j in range(N_LT):
        nconv_ref[:, j * LANES:(j + 1) * LANES] = z_s[j, tm + Z_HEAD - (CONV_W - 1):tm + Z_HEAD, :]
        npool_ref[:, j * LANES:(j + 1) * LANES] = p_s[j, tm + P_HEAD - POOL_BUF:tm + P_HEAD, :]

    wf1o_ref[...] = wf1_ref[...].astype(BF16)
    wf2o_ref[...] = wf2_ref[...].astype(BF16)


def _sample_mixer_kernel(x_ref, sconv_ref, sp0_ref, sp1_ref, sp2_ref, sp3_ref, win_ref, lnvg_ref, lnvb_ref, wsd_ref,
                         bsd_ref, wpa_ref, cw_ref, cb_ref, wpb_ref, wpool_ref, psc_ref, wpc_ref, wo_ref, l1g_ref,
                         l1b_ref, h_ref, z_ref, xc_ref, vn_ref, *rest, emit_bf16):
    if emit_bf16:
        wino_ref, wpao_ref, wpbo_ref, wpoolo_ref, wpco_ref, woo_ref = rest[:len(MIXER_MATS)]
    else:
        wino_ref = wpao_ref = wpbo_ref = wpoolo_ref = wpco_ref = woo_ref = None
    xb_s, proj_s, wpa_s, wpb_s, wpc_s, wo_s = rest[len(MIXER_MATS) if emit_bf16 else 0:]
    k = pl.program_id(0)

    @pl.when(k == 0)
    def _():
        xb_s[...] = x_ref[...].astype(BF16)

    @pl.when(k < N_COLS)
    def _():
        w = win_ref[...].astype(BF16)
        if emit_bf16:
            wino_ref[...] = w
        proj_s[k] = _dot(xb_s[...], w)

    @pl.when(k < N_ROW_PIECES)
    def _():
        rows = pl.ds(pl.multiple_of(k * ROW_PIECE, ROW_PIECE), ROW_PIECE)
        for src, dst, keep in ((wpa_ref, wpao_ref, wpa_s), (wpb_ref, wpbo_ref, wpb_s), (wpc_ref, wpco_ref, wpc_s),
                               (wo_ref, woo_ref, wo_s)):
            piece = src[...].astype(BF16)
            if emit_bf16:
                dst[...] = piece
            keep[rows, :] = piece

    @pl.when(k == N_COLS)
    def _():
        x = x_ref[...]
        wpa, wpb, wpc, wo = wpa_s[...], wpb_s[...], wpc_s[...], wo_s[...]
        wpool = wpool_ref[...].astype(BF16)
        if emit_bf16:
            wpoolo_ref[...] = wpool

        vn = _layer_norm(proj_s[COL_V], lnvg_ref[...], lnvb_ref[...])
        vn_ref[...] = vn
        s = wsd_ref[...] * vn + bsd_ref[...]
        ha = (proj_s[COL_U] * s).astype(BF16)
        acc = jax.nn.sigmoid(proj_s[COL_GATE + 0]) * _dot(ha, wpa)

        z = proj_s[COL_CG] * proj_s[COL_XB]
        z_ref[...] = z
        cw = cw_ref[...]
        y = cb_ref[...] + (cw[0:1] * sconv_ref[0] + cw[1:2] * sconv_ref[1] + cw[2:3] * z)
        hb = (proj_s[COL_BG] * y).astype(BF16)
        acc = acc + jax.nn.sigmoid(proj_s[COL_GATE + 1]) * _dot(hb, wpb)

        xc = proj_s[COL_XC]
        xc_ref[...] = xc
        psc = psc_ref[...]
        hc = []
        for g, (w, sp_ref) in enumerate(zip(POOL_WINDOWS, (sp0_ref, sp1_ref, sp2_ref, sp3_ref))):
            cs = slice(g * G_C, (g + 1) * G_C)
            tot = xc[:, cs]
            for j in range(w - 1):
                tot = tot + sp_ref[w - 2 - j]
            d = tot * (1.0 / w) - xc[:, cs]
            hc.append((_dot(d.astype(BF16), wpool[g]) * psc[:, cs]).astype(BF16))
        hc = jnp.concatenate(hc, axis=-1)
        acc = acc + jax.nn.sigmoid(proj_s[COL_GATE + 2]) * _dot(hc, wpc)

        o = _dot(acc.astype(BF16), wo)
        h_ref[...] = _layer_norm(ALPHA * x + o, l1g_ref[...], l1b_ref[...])


def _ffn_rows(h, hb, w1_ref, w2_ref, g_ref, b_ref):
    acc = None
    for j in range(D_FF // D):
        f = _dot(hb, w1_ref[:, j * D:(j + 1) * D])
        a = jnp.square(jnp.maximum(f, 0.0)).astype(BF16)
        c = _dot(a, w2_ref[j * D:(j + 1) * D, :])
        acc = c if acc is None else acc + c
    return _layer_norm(ALPHA * h + acc, g_ref[...], b_ref[...])


def _ffn_kernel(*refs, sub, n_cvt):
    h_ref, hs_ref, w1_ref, w2_ref, g_ref, b_ref = refs[:6]
    cvt_in = refs[6:6 + n_cvt]
    o_ref, os_ref = refs[6 + n_cvt:8 + n_cvt]
    cvt_out = refs[8 + n_cvt:8 + 2 * n_cvt]
    (hb_s,) = refs[8 + 2 * n_cvt:]

    for r0 in range(0, h_ref.shape[0], sub):
        rs = slice(r0, r0 + sub)
        hb_s[rs, :] = h_ref[rs, :].astype(BF16)
        o_ref[rs, :] = _ffn_rows(h_ref[rs, :], hb_s[rs, :], w1_ref, w2_ref, g_ref, b_ref)

    @pl.when(pl.program_id(0) == pl.num_programs(0) - 1)
    def _():
        hs = hs_ref[...]
        os_ref[...] = _ffn_rows(hs, hs.astype(BF16), w1_ref, w2_ref, g_ref, b_ref)

    for src, dst in zip(cvt_in, cvt_out):
        dst[...] = src[...].astype(BF16)


def _resident(shape):
    nd = len(shape)
    return pl.BlockSpec(shape, lambda *_: (0,) * nd, pipeline_mode=pl.Buffered(1))


def _layer_resident(a, l):
    return pl.BlockSpec((None,) + a.shape[1:], lambda *_: (l,) + (0,) * (a.ndim - 1), pipeline_mode=pl.Buffered(1))


MIXER_MATS = ('w_in', 'w_proj_a', 'w_proj_b', 'w_pool', 'w_proj_c', 'w_o')


def _mixer_prompt(x2d, batch, seq, tm, mats, p, l):
    nt = seq // tm
    nsteps = batch * nt
    row_block = pl.BlockSpec((tm, D), lambda b, t: (b * nt + t, 0))
    r1, r2 = D // nsteps, D_FF // nsteps
    lay = lambda k: (p[k], _layer_resident(p[k], l))
    mat = lambda k: (mats[k], _resident(mats[k].shape))
    operands = [(x2d, row_block), mat('w_in'), lay('lnv_g'), lay('lnv_b'), lay('w_spatial'), lay('b_spatial_t'),
                mat('w_proj_a'), lay('conv_w'), lay('conv_b'), mat('w_proj_b'), mat('w_pool'), lay('pool_scale'),
                mat('w_proj_c'), mat('w_o'), lay('ln1_g'), lay('ln1_b'),
                (p['w_ff1'], pl.BlockSpec((None, r1, D_FF), lambda b, t: (l, b * nt + t, 0))),
                (p['w_ff2'], pl.BlockSpec((None, r2, D), lambda b, t: (l, b * nt + t, 0)))]
    return pl.pallas_call(
        functools.partial(_mixer_prompt_kernel, tm=tm, sub=min(tm, SUB_MIXER)),
        grid=(batch, nt),
        in_specs=[spec for _, spec in operands],
        out_specs=[row_block,
                   pl.BlockSpec((None, CONV_W - 1, D), lambda b, t: (b, 0, 0)),
                   pl.BlockSpec((None, POOL_BUF, D), lambda b, t: (b, 0, 0)),
                   pl.BlockSpec((r1, D_FF), lambda b, t: (b * nt + t, 0)),
                   pl.BlockSpec((r2, D), lambda b, t: (b * nt + t, 0))],
        out_shape=[jax.ShapeDtypeStruct((batch * seq, D), F32),
                   jax.ShapeDtypeStruct((batch, CONV_W - 1, D), F32),
                   jax.ShapeDtypeStruct((batch, POOL_BUF, D), F32),
                   jax.ShapeDtypeStruct((D, D_FF), BF16),
                   jax.ShapeDtypeStruct((D_FF, D), BF16)],
        scratch_shapes=[pltpu.VMEM((tm, D), BF16),
                        pltpu.VMEM((tm, D), BF16),
                        pltpu.VMEM((tm, D), F32),
                        pltpu.VMEM((tm, D), BF16),
                        pltpu.VMEM((tm, D), BF16),
                        pltpu.VMEM((tm, D), BF16),
                        pltpu.VMEM((tm, D), F32),
                        pltpu.VMEM((N_LT, Z_HEAD + tm, LANES), F32),
                        pltpu.VMEM((N_LT, P_HEAD + tm, LANES), F32),
                        pltpu.VMEM((N_LT, P_HEAD + tm, LANES), F32),
                        pltpu.VMEM((N_LT - LT_G, P_HEAD + tm, LANES), F32),
                        pltpu.VMEM((N_LT - 2 * LT_G, P_HEAD + tm, LANES), F32)],
        compiler_params=pltpu.CompilerParams(dimension_semantics=("arbitrary", "arbitrary"),
                                             vmem_limit_bytes=VMEM_LIMIT_MIXER),
        name="mixer_prompt",
    )(*[a for a, _ in operands])


def _sample_mixer(x2d, mats, p, l):
    n = x2d.shape[0]
    emit_bf16 = mats is None
    lay = lambda k: (p[k], _layer_resident(p[k], l))
    col_block = lambda k: (0, jnp.minimum(k, N_COLS - 1))
    row_piece = lambda k: (jnp.minimum(k, N_ROW_PIECES - 1), 0)
    if emit_bf16:
        w_in = (p['w_in'], pl.BlockSpec((None, D, D), lambda k: (l,) + col_block(k)))
        sq = lambda name: (p[name], pl.BlockSpec((None, ROW_PIECE, D), lambda k: (l,) + row_piece(k)))
        w_pool = lay('w_pool')
    else:
        w_in = (mats['w_in'], pl.BlockSpec((D, D), col_block))
        sq = lambda name: (mats[name], pl.BlockSpec((ROW_PIECE, D), row_piece))
        w_pool = (mats['w_pool'], _resident(mats['w_pool'].shape))
    operands = [(x2d, _resident(x2d.shape)), lay('state_conv_t'), lay('state_pool_t0'), lay('state_pool_t1'),
                lay('state_pool_t2'), lay('state_pool_t3'), w_in,
                lay('lnv_g'), lay('lnv_b'), lay('w_spatial_d'), lay('b_spatial_d'), sq('w_proj_a'), lay('conv_w'),
                lay('conv_b'), sq('w_proj_b'), w_pool, lay('pool_scale'), sq('w_proj_c'), sq('w_o'),
                lay('ln1_g'), lay('ln1_b')]
    once = lambda shape: pl.BlockSpec(shape, lambda k: (0,) * len(shape), pipeline_mode=pl.Buffered(1))
    out_specs = [once((n, D))] * 4
    out_shape = [jax.ShapeDtypeStruct((n, D), F32)] * 4
    if emit_bf16:
        sq_out = pl.BlockSpec((ROW_PIECE, D), row_piece)
        out_specs += [pl.BlockSpec((D, D), col_block), sq_out, sq_out, once(p['w_pool'].shape[1:]), sq_out, sq_out]
        out_shape += [jax.ShapeDtypeStruct(p[k].shape[1:], BF16) for k in MIXER_MATS]
    outs = pl.pallas_call(
        functools.partial(_sample_mixer_kernel, emit_bf16=emit_bf16),
        grid=(N_COLS + 1,),
        in_specs=[spec for _, spec in operands],
        out_specs=out_specs,
        out_shape=out_shape,
        scratch_shapes=[pltpu.VMEM((n, D), BF16),
                        pltpu.VMEM((N_COLS, n, D), F32)]
                       + [pltpu.VMEM((D, D), BF16)] * 4,
        compiler_params=pltpu.CompilerParams(
            dimension_semantics=("arbitrary",), vmem_limit_bytes=VMEM_LIMIT_MIXER,
            allow_input_fusion=[1 <= i <= 5 for i in range(len(operands))]),
        name="sample_mixer",
    )(*[a for a, _ in operands])
    return outs[:4], (dict(zip(MIXER_MATS, outs[4:])) if emit_bf16 else mats)


def _ffn(h2d, hs2d, tm, w1, w2, p, l, convert_next):
    n = h2d.shape[0]
    nsteps = n // tm
    row_block = pl.BlockSpec((tm, D), lambda i: (i, 0))
    lay = lambda k: (p[k], _layer_resident(p[k], l))
    operands = [(h2d, row_block), (hs2d, _resident(hs2d.shape)), (w1, _resident(w1.shape)), (w2, _resident(w2.shape)),
                lay('ln2_g'), lay('ln2_b')]
    out_specs = [row_block, pl.BlockSpec(hs2d.shape, lambda i: (0, 0))]
    out_shape = [jax.ShapeDtypeStruct((n, D), F32), jax.ShapeDtypeStruct(hs2d.shape, F32)]
    n_cvt = len(MIXER_MATS) if convert_next else 0
    if convert_next:
        for k in MIXER_MATS:
            a = p[k]
            r = a.shape[-2] // nsteps
            lead = a.ndim - 3
            blk = a.shape[1:-2] + (r, a.shape[-1])
            operands.append((a, pl.BlockSpec((None,) + blk, lambda i, lead=lead: (l + 1,) + (0,) * lead + (i, 0))))
            out_specs.append(pl.BlockSpec(blk, lambda i, lead=lead: (0,) * lead + (i, 0)))
            out_shape.append(jax.ShapeDtypeStruct(a.shape[1:], BF16))
    outs = pl.pallas_call(
        functools.partial(_ffn_kernel, sub=min(tm, SUB_FFN), n_cvt=n_cvt),
        grid=(nsteps,),
        in_specs=[spec for _, spec in operands],
        out_specs=out_specs,
        out_shape=out_shape,
        scratch_shapes=[pltpu.VMEM((tm, D), BF16)],
        compiler_params=pltpu.CompilerParams(dimension_semantics=("arbitrary",),
                                             vmem_limit_bytes=VMEM_LIMIT_FFN),
        name="ffn",
    )(*[a for a, _ in operands])
    return outs[0], outs[1], (dict(zip(MIXER_MATS, outs[2:])) if convert_next else None)


def kernel(x_prompt, x_sample, state_conv, state_pool, w_in, lnv_g, lnv_b, w_spatial, b_spatial, w_proj_a, conv_w, conv_b, w_proj_b, w_pool, pool_scale, w_proj_c, w_o, ln1_g, ln1_b, w_ff1, w_ff2, ln2_g, ln2_b):
    bp, seq, _ = x_prompt.shape
    bs = x_sample.shape[0]
    xp = x_prompt.reshape(bp * seq, D)
    xs = x_sample.reshape(bs, D)
    conv_p, pool_p, conv_s, pool_s, chunk_v_s = [], [], [], [], []
    rows = lambda a: a[:, None, :]
    p = {
        'w_in': w_in, 'lnv_g': rows(lnv_g), 'lnv_b': rows(lnv_b),
        'w_spatial': w_spatial, 'b_spatial_t': jnp.swapaxes(b_spatial, 1, 2),
        'w_spatial_d': rows(jnp.repeat(w_spatial[:, :, 0, 0], G_A, axis=1)),
        'b_spatial_d': rows(jnp.repeat(b_spatial[:, :, 0], G_A, axis=1)),
        'w_proj_a': w_proj_a, 'conv_w': conv_w, 'conv_b': rows(conv_b),
        'w_proj_b': w_proj_b, 'w_pool': w_pool, 'pool_scale': rows(pool_scale),
        'w_proj_c': w_proj_c, 'w_o': w_o, 'ln1_g': rows(ln1_g), 'ln1_b': rows(ln1_b),
        'w_ff1': w_ff1, 'w_ff2': w_ff2, 'ln2_g': rows(ln2_g), 'ln2_b': rows(ln2_b),
        'state_conv_t': jnp.swapaxes(state_conv, 1, 2),
    }
    for g, w in enumerate(POOL_WINDOWS):
        p['state_pool_t%d' % g] = jnp.swapaxes(state_pool[:, :, POOL_BUF - (w - 1):, g * G_C:(g + 1) * G_C], 1, 2)
    mats = None
    for l in range(DEPTH):
        (hs, z_new, xc_new, vn), mats = _sample_mixer(xs, mats, p, l)
        conv_s.append(jnp.concatenate([state_conv[l][:, 1:], z_new[:, None, :]], axis=1))
        pool_s.append(jnp.concatenate([state_pool[l][:, 1:], xc_new[:, None, :]], axis=1))
        chunk_v_s.append(vn[:, None, :])
        hp, nconv, npool, w1, w2 = _mixer_prompt(xp, bp, seq, TM_MIXER, mats, p, l)
        conv_p.append(nconv)
        pool_p.append(npool)
        xp, xs, mats = _ffn(hp, hs, TM_FFN, w1, w2, p, l, convert_next=l + 1 < DEPTH)
    return (xp.reshape(bp, seq, D), xs.reshape(bs, 1, D), jnp.stack(conv_p), jnp.stack(pool_p),
            jnp.stack(conv_s), jnp.stack(pool_s), jnp.stack(chunk_v_s))
```

```python
import functools

import jax
import jax.numpy as jnp
from jax import lax
from jax.experimental import pallas as pl
from jax.experimental.pallas import tpu as pltpu

D = 1024
N_GROUPS_A = 4
CHUNK = 128
G_A = D // N_GROUPS_A
CONV_W = 3
POOL_WINDOWS = (2, 4, 8, 16)
G_C = D // len(POOL_WINDOWS)
POOL_BUF = max(POOL_WINDOWS) - 1
D_FF = 4 * D
DEPTH = 2
ALPHA = float((2 * DEPTH) ** 0.25)
LN_EPS = 1e-5
PAST_LEN = 16384

COL_U, COL_V, COL_BG, COL_CG, COL_XB, COL_XC, COL_GATE = 0, 1, 2, 3, 4, 5, 6
N_COLS = 9
N_ROW_PIECES = 8
ROW_PIECE = D // N_ROW_PIECES

SUBLANES = 8
LANES = 128
N_LT = D // LANES
LT_G = G_C // LANES
Z_HEAD = SUBLANES
P_HEAD = 32
P_LO = 16
P_CARRY = 16
assert CONV_W - 1 <= Z_HEAD and POOL_BUF <= P_CARRY <= P_HEAD - P_LO and P_CARRY % SUBLANES == 0
assert PAST_LEN % CHUNK == 0 and PAST_LEN >= max(POOL_WINDOWS)

TM_MIXER = 512
TM_FFN = 1024
SUB_MIXER = 256
SUB_FFN = 256
VMEM_LIMIT_MIXER = 56 * 1024 * 1024
VMEM_LIMIT_FFN = 48 * 1024 * 1024

F32 = jnp.float32
BF16 = jnp.bfloat16


def _dot(a, b):
    return jnp.dot(a, b, preferred_element_type=F32)


def _layer_norm(x, g, b):
    mu = jnp.mean(x, axis=-1, keepdims=True)
    xc = x - mu
    var = jnp.mean(xc * xc, axis=-1, keepdims=True)
    return xc * lax.rsqrt(var + LN_EPS) * g + b


def _mixer_prompt_kernel(x_ref, win_ref, lnvg_ref, lnvb_ref, wsp_ref, bsp_ref, wpa_ref, cw_ref, cb_ref,
                         wpb_ref, wpool_ref, psc_ref, wpc_ref, wo_ref, l1g_ref, l1b_ref, wf1_ref, wf2_ref,
                         h_ref, nconv_ref, npool_ref, wf1o_ref, wf2o_ref,
                         xb_s, vn_s, u_s, ha_s, hb_s, hc_s, acc_s, z_s, p_s, la_s, lb_s, lc_s, *, tm, sub):
    t = pl.program_id(1)

    @pl.when(t == 0)
    def _():
        z_s[:, 0:Z_HEAD, :] = jnp.zeros((N_LT, Z_HEAD, LANES), F32)
        p_s[:, 0:P_HEAD, :] = jnp.zeros((N_LT, P_HEAD, LANES), F32)
        la_s[:, 0:P_LO, :] = jnp.zeros((N_LT, P_LO, LANES), F32)
        lb_s[:, 0:P_LO, :] = jnp.zeros((N_LT - LT_G, P_LO, LANES), F32)
        lc_s[:, 0:P_LO, :] = jnp.zeros((N_LT - 2 * LT_G, P_LO, LANES), F32)

    row = lax.broadcasted_iota(jnp.int32, (CHUNK, CHUNK), 0)
    col = lax.broadcasted_iota(jnp.int32, (CHUNK, CHUNK), 1)
    causal = col <= row
    w_spatial = [jnp.where(causal, wsp_ref[g], 0.0).astype(BF16) for g in range(N_GROUPS_A)]
    cw = cw_ref[...]
    psc = psc_ref[...]

    for r0 in range(0, tm, sub):
        rs = slice(r0, r0 + sub)
        xb_s[rs, :] = x_ref[rs, :].astype(BF16)

        def proj(k):
            return _dot(xb_s[rs, :], win_ref[:, k * D:(k + 1) * D])

        xc = proj(COL_XC)
        p0, p1 = P_HEAD + r0, P_HEAD + r0 + sub
        for j in range(N_LT):
            p_s[j, p0:p1, :] = xc[:, j * LANES:(j + 1) * LANES]
        lo = P_LO if r0 == 0 else p0
        for j in range(N_LT):
            la_s[j, lo:p1, :] = p_s[j, lo:p1, :] + p_s[j, lo - 1:p1 - 1, :]
        for j in range(N_LT - LT_G):
            lb_s[j, lo:p1, :] = la_s[j + LT_G, lo:p1, :] + la_s[j + LT_G, lo - 2:p1 - 2, :]
        for j in range(N_LT - 2 * LT_G):
            lc_s[j, lo:p1, :] = lb_s[j + LT_G, lo:p1, :] + lb_s[j + LT_G, lo - 4:p1 - 4, :]
        sum16 = [lc_s[j + LT_G, p0:p1, :] + lc_s[j + LT_G, p0 - 8:p1 - 8, :] for j in range(LT_G)]
        sums = tuple(jnp.concatenate(tiles, axis=-1) for tiles in (
            [la_s[j, p0:p1, :] for j in range(LT_G)], [lb_s[j, p0:p1, :] for j in range(LT_G)],
            [lc_s[j, p0:p1, :] for j in range(LT_G)], sum16))
        pos1 = t * tm + r0 + lax.broadcasted_iota(jnp.int32, (sub, 1), 0) + 1
        for g, w in enumerate(POOL_WINDOWS):
            cs = slice(g * G_C, (g + 1) * G_C)
            inv_cnt = 1.0 / jnp.minimum(pos1, w).astype(F32)
            d = sums[g] * inv_cnt - xc[:, cs]
            hc_s[rs, cs] = (_dot(d.astype(BF16), wpool_ref[g]) * psc[:, cs]).astype(BF16)
        acc_s[rs, :] = jax.nn.sigmoid(proj(COL_GATE + 2)) * _dot(hc_s[rs, :], wpc_ref[...])

        vn_s[rs, :] = _layer_norm(proj(COL_V), lnvg_ref[...], lnvb_ref[...]).astype(BF16)
        u_s[rs, :] = proj(COL_U)
        for g in range(N_GROUPS_A):
            bias = bsp_ref[:, g:g + 1]
            cs = slice(g * G_A, (g + 1) * G_A)
            for c0 in range(r0, r0 + sub, CHUNK):
                ch = slice(c0, c0 + CHUNK)
                s = _dot(w_spatial[g], vn_s[ch, cs]) + bias
                ha_s[ch, cs] = (u_s[ch, cs] * s).astype(BF16)
        acc_s[rs, :] += jax.nn.sigmoid(proj(COL_GATE + 0)) * _dot(ha_s[rs, :], wpa_ref[...])

        z = proj(COL_CG) * proj(COL_XB)
        z0 = Z_HEAD + r0
        for j in range(N_LT):
            z_s[j, z0:z0 + sub, :] = z[:, j * LANES:(j + 1) * LANES]
        z_m2 = jnp.concatenate([z_s[j, z0 - 2:z0 - 2 + sub, :] for j in range(N_LT)], axis=-1)
        z_m1 = jnp.concatenate([z_s[j, z0 - 1:z0 - 1 + sub, :] for j in range(N_LT)], axis=-1)
        y = cb_ref[...] + (cw[0:1] * z_m2 + cw[1:2] * z_m1 + cw[2:3] * z)
        hb_s[rs, :] = (proj(COL_BG) * y).astype(BF16)
        acc_s[rs, :] += jax.nn.sigmoid(proj(COL_GATE + 1)) * _dot(hb_s[rs, :], wpb_ref[...])

        o = _dot(acc_s[rs, :].astype(BF16), wo_ref[...])
        h_ref[rs, :] = _layer_norm(ALPHA * x_ref[rs, :] + o, l1g_ref[...], l1b_ref[...])

    tail = z_s[:, tm:tm + Z_HEAD, :]
    z_s[:, 0:Z_HEAD, :] = tail
    hist = p_s[:, tm + P_HEAD - P_CARRY:tm + P_HEAD, :]
    p_s[:, P_HEAD - P_CARRY:P_HEAD, :] = hist
    for j in range(N_LT):
        nconv_ref[:, j * LANES:(j + 1) * LANES] = z_s[j, tm + Z_HEAD - (CONV_W - 1):tm + Z_HEAD, :]
        npool_ref[:, j * LANES:(j + 1) * LANES] = p_s[j, tm + P_HEAD - POOL_BUF:tm + P_HEAD, :]

    wf1o_ref[...] = wf1_ref[...].astype(BF16)
    wf2o_ref[...] = wf2_ref[...].astype(BF16)


def _sample_mixer_kernel(x_ref, sconv_ref, sp0_ref, sp1_ref, sp2_ref, sp3_ref, win_ref, lnvg_ref, lnvb_ref, wsd_ref,
                         bsd_ref, wpa_ref, cw_ref, cb_ref, wpb_ref, wpool_ref, psc_ref, wpc_ref, wo_ref, l1g_ref,
                         l1b_ref, h_ref, z_ref, xc_ref, vn_ref, *rest, emit_bf16):
    if emit_bf16:
        wino_ref, wpao_ref, wpbo_ref, wpoolo_ref, wpco_ref, woo_ref = rest[:len(MIXER_MATS)]
    else:
        wino_ref = wpao_ref = wpbo_ref = wpoolo_ref = wpco_ref = woo_ref = None
    xb_s, proj_s, wpa_s, wpb_s, wpc_s, wo_s = rest[len(MIXER_MATS) if emit_bf16 else 0:]
    k = pl.program_id(0)

    @pl.when(k == 0)
    def _():
        xb_s[...] = x_ref[...].astype(BF16)

    @pl.when(k < N_COLS)
    def _():
        w = win_ref[...].astype(BF16)
        if emit_bf16:
            wino_ref[...] = w
        proj_s[k] = _dot(xb_s[...], w)

    @pl.when(k < N_ROW_PIECES)
    def _():
        rows = pl.ds(pl.multiple_of(k * ROW_PIECE, ROW_PIECE), ROW_PIECE)
        for src, dst, keep in ((wpa_ref, wpao_ref, wpa_s), (wpb_ref, wpbo_ref, wpb_s), (wpc_ref, wpco_ref, wpc_s),
                               (wo_ref, woo_ref, wo_s)):
            piece = src[...].astype(BF16)
            if emit_bf16:
                dst[...] = piece
            keep[rows, :] = piece

    @pl.when(k == N_COLS)
    def _():
        x = x_ref[...]
        wpa, wpb, wpc, wo = wpa_s[...], wpb_s[...], wpc_s[...], wo_s[...]
        wpool = wpool_ref[...].astype(BF16)
        if emit_bf16:
            wpoolo_ref[...] = wpool

        vn = _layer_norm(proj_s[COL_V], lnvg_ref[...], lnvb_ref[...])
        vn_ref[...] = vn
        s = wsd_ref[...] * vn + bsd_ref[...]
        ha = (proj_s[COL_U] * s).astype(BF16)
        acc = jax.nn.sigmoid(proj_s[COL_GATE + 0]) * _dot(ha, wpa)

        z = proj_s[COL_CG] * proj_s[COL_XB]
        z_ref[...] = z
        cw = cw_ref[...]
        y = cb_ref[...] + (cw[0:1] * sconv_ref[0] + cw[1:2] * sconv_ref[1] + cw[2:3] * z)
        hb = (proj_s[COL_BG] * y).astype(BF16)
        acc = acc + jax.nn.sigmoid(proj_s[COL_GATE + 1]) * _dot(hb, wpb)

        xc = proj_s[COL_XC]
        xc_ref[...] = xc
        psc = psc_ref[...]
        hc = []
        for g, (w, sp_ref) in enumerate(zip(POOL_WINDOWS, (sp0_ref, sp1_ref, sp2_ref, sp3_ref))):
            cs = slice(g * G_C, (g + 1) * G_C)
            tot = xc[:, cs]
            for j in range(w - 1):
                tot = tot + sp_ref[w - 2 - j]
            d = tot * (1.0 / w) - xc[:, cs]
            hc.append((_dot(d.astype(BF16), wpool[g]) * psc[:, cs]).astype(BF16))
        hc = jnp.concatenate(hc, axis=-1)
        acc = acc + jax.nn.sigmoid(proj_s[COL_GATE + 2]) * _dot(hc, wpc)

        o = _dot(acc.astype(BF16), wo)
        h_ref[...] = _layer_norm(ALPHA * x + o, l1g_ref[...], l1b_ref[...])


def _ffn_rows(h, hb, w1_ref, w2_ref, g_ref, b_ref):
    acc = None
    for j in range(D_FF // D):
        f = _dot(hb, w1_ref[:, j * D:(j + 1) * D])
        a = jnp.square(jnp.maximum(f, 0.0)).astype(BF16)
        c = _dot(a, w2_ref[j * D:(j + 1) * D, :])
        acc = c if acc is None else acc + c
    return _layer_norm(ALPHA * h + acc, g_ref[...], b_ref[...])


def _ffn_kernel(*refs, sub, n_cvt):
    h_ref, hs_ref, w1_ref, w2_ref, g_ref, b_ref = refs[:6]
    cvt_in = refs[6:6 + n_cvt]
    o_ref, os_ref = refs[6 + n_cvt:8 + n_cvt]
    cvt_out = refs[8 + n_cvt:8 + 2 * n_cvt]
    (hb_s,) = refs[8 + 2 * n_cvt:]

    for r0 in range(0, h_ref.shape[0], sub):
        rs = slice(r0, r0 + sub)
        hb_s[rs, :] = h_ref[rs, :].astype(BF16)
        o_ref[rs, :] = _ffn_rows(h_ref[rs, :], hb_s[rs, :], w1_ref, w2_ref, g_ref, b_ref)

    @pl.when(pl.program_id(0) == pl.num_programs(0) - 1)
    def _():
        hs = hs_ref[...]
        os_ref[...] = _ffn_rows(hs, hs.astype(BF16), w1_ref, w2_ref, g_ref, b_ref)

    for src, dst in zip(cvt_in, cvt_out):
        dst[...] = src[...].astype(BF16)


def _resident(shape):
    nd = len(shape)
    return pl.BlockSpec(shape, lambda *_: (0,) * nd, pipeline_mode=pl.Buffered(1))


def _layer_resident(a, l):
    return pl.BlockSpec((None,) + a.shape[1:], lambda *_: (l,) + (0,) * (a.ndim - 1), pipeline_mode=pl.Buffered(1))


MIXER_MATS = ('w_in', 'w_proj_a', 'w_proj_b', 'w_pool', 'w_proj_c', 'w_o')


def _mixer_prompt(x2d, batch, seq, tm, mats, p, l):
    nt = seq // tm
    nsteps = batch * nt
    row_block = pl.BlockSpec((tm, D), lambda b, t: (b * nt + t, 0))
    r1, r2 = D // nsteps, D_FF // nsteps
    lay = lambda k: (p[k], _layer_resident(p[k], l))
    mat = lambda k: (mats[k], _resident(mats[k].shape))
    operands = [(x2d, row_block), mat('w_in'), lay('lnv_g'), lay('lnv_b'), lay('w_spatial'), lay('b_spatial_t'),
                mat('w_proj_a'), lay('conv_w'), lay('conv_b'), mat('w_proj_b'), mat('w_pool'), lay('pool_scale'),
                mat('w_proj_c'), mat('w_o'), lay('ln1_g'), lay('ln1_b'),
                (p['w_ff1'], pl.BlockSpec((None, r1, D_FF), lambda b, t: (l, b * nt + t, 0))),
                (p['w_ff2'], pl.BlockSpec((None, r2, D), lambda b, t: (l, b * nt + t, 0)))]
    return pl.pallas_call(
        functools.partial(_mixer_prompt_kernel, tm=tm, sub=min(tm, SUB_MIXER)),
        grid=(batch, nt),
        in_specs=[spec for _, spec in operands],
        out_specs=[row_block,
                   pl.BlockSpec((None, CONV_W - 1, D), lambda b, t: (b, 0, 0)),
                   pl.BlockSpec((None, POOL_BUF, D), lambda b, t: (b, 0, 0)),
                   pl.BlockSpec((r1, D_FF), lambda b, t: (b * nt + t, 0)),
                   pl.BlockSpec((r2, D), lambda b, t: (b * nt + t, 0))],
        out_shape=[jax.ShapeDtypeStruct((batch * seq, D), F32),
                   jax.ShapeDtypeStruct((batch, CONV_W - 1, D), F32),
                   jax.ShapeDtypeStruct((batch, POOL_BUF, D), F32),
                   jax.ShapeDtypeStruct((D, D_FF), BF16),
                   jax.ShapeDtypeStruct((D_FF, D), BF16)],
        scratch_shapes=[pltpu.VMEM((tm, D), BF16),
                        pltpu.VMEM((tm, D), BF16),
                        pltpu.VMEM((tm, D), F32),
                        pltpu.VMEM((tm, D), BF16),
                        pltpu.VMEM((tm, D), BF16),
                        pltpu.VMEM((tm, D), BF16),
                        pltpu.VMEM((tm, D), F32),
                        pltpu.VMEM((N_LT, Z_HEAD + tm, LANES), F32),
                        pltpu.VMEM((N_LT, P_HEAD + tm, LANES), F32),
                        pltpu.VMEM((N_LT, P_HEAD + tm, LANES), F32),
                        pltpu.VMEM((N_LT - LT_G, P_HEAD + tm, LANES), F32),
                        pltpu.VMEM((N_LT - 2 * LT_G, P_HEAD + tm, LANES), F32)],
        compiler_params=pltpu.CompilerParams(dimension_semantics=("arbitrary", "arbitrary"),
                                             vmem_limit_bytes=VMEM_LIMIT_MIXER),
        name="mixer_prompt",
    )(*[a for a, _ in operands])


def _sample_mixer(x2d, mats, p, l):
    n = x2d.shape[0]
    emit_bf16 = mats is None
    lay = lambda k: (p[k], _layer_resident(p[k], l))
    col_block = lambda k: (0, jnp.minimum(k, N_COLS - 1))
    row_piece = lambda k: (jnp.minimum(k, N_ROW_PIECES - 1), 0)
    if emit_bf16:
        w_in = (p['w_in'], pl.BlockSpec((None, D, D), lambda k: (l,) + col_block(k)))
        sq = lambda name: (p[name], pl.BlockSpec((None, ROW_PIECE, D), lambda k: (l,) + row_piece(k)))
        w_pool = lay('w_pool')
    else:
        w_in = (mats['w_in'], pl.BlockSpec((D, D), col_block))
        sq = lambda name: (mats[name], pl.BlockSpec((ROW_PIECE, D), row_piece))
        w_pool = (mats['w_pool'], _resident(mats['w_pool'].shape))
    history = [jnp.swapaxes(p['state_conv'][l], 0, 1)]
    history += [jnp.swapaxes(p['state_pool'][l, :, POOL_BUF - (w - 1):, g * G_C:(g + 1) * G_C], 0, 1)
                for g, w in enumerate(POOL_WINDOWS)]
    operands = [(x2d, _resident(x2d.shape))] + [(a, _resident(a.shape)) for a in history] + [
                w_in,
                lay('lnv_g'), lay('lnv_b'), lay('w_spatial_d'), lay('b_spatial_d'), sq('w_proj_a'), lay('conv_w'),
                lay('conv_b'), sq('w_proj_b'), w_pool, lay('pool_scale'), sq('w_proj_c'), sq('w_o'),
                lay('ln1_g'), lay('ln1_b')]
    once = lambda shape: pl.BlockSpec(shape, lambda k: (0,) * len(shape), pipeline_mode=pl.Buffered(1))
    out_specs = [once((n, D))] * 4
    out_shape = [jax.ShapeDtypeStruct((n, D), F32)] * 4
    if emit_bf16:
        sq_out = pl.BlockSpec((ROW_PIECE, D), row_piece)
        out_specs += [pl.BlockSpec((D, D), col_block), sq_out, sq_out, once(p['w_pool'].shape[1:]), sq_out, sq_out]
        out_shape += [jax.ShapeDtypeStruct(p[k].shape[1:], BF16) for k in MIXER_MATS]
    outs = pl.pallas_call(
        functools.partial(_sample_mixer_kernel, emit_bf16=emit_bf16),
        grid=(N_COLS + 1,),
        in_specs=[spec for _, spec in operands],
        out_specs=out_specs,
        out_shape=out_shape,
        scratch_shapes=[pltpu.VMEM((n, D), BF16),
                        pltpu.VMEM((N_COLS, n, D), F32)]
                       + [pltpu.VMEM((D, D), BF16)] * 4,
        compiler_params=pltpu.CompilerParams(
            dimension_semantics=("arbitrary",), vmem_limit_bytes=VMEM_LIMIT_MIXER,
            allow_input_fusion=[1 <= i <= len(history) for i in range(len(operands))]),
        name="sample_mixer",
    )(*[a for a, _ in operands])
    return outs[:4], (dict(zip(MIXER_MATS, outs[4:])) if emit_bf16 else mats)


def _ffn(h2d, hs2d, tm, w1, w2, p, l, convert_next):
    n = h2d.shape[0]
    nsteps = n // tm
    row_block = pl.BlockSpec((tm, D), lambda i: (i, 0))
    lay = lambda k: (p[k], _layer_resident(p[k], l))
    operands = [(h2d, row_block), (hs2d, _resident(hs2d.shape)), (w1, _resident(w1.shape)), (w2, _resident(w2.shape)),
                lay('ln2_g'), lay('ln2_b')]
    out_specs = [row_block, pl.BlockSpec(hs2d.shape, lambda i: (0, 0))]
    out_shape = [jax.ShapeDtypeStruct((n, D), F32), jax.ShapeDtypeStruct(hs2d.shape, F32)]
    n_cvt = len(MIXER_MATS) if convert_next else 0
    if convert_next:
        for k in MIXER_MATS:
            a = p[k]
            r = a.shape[-2] // nsteps
            lead = a.ndim - 3
            blk = a.shape[1:-2] + (r, a.shape[-1])
            operands.append((a, pl.BlockSpec((None,) + blk, lambda i, lead=lead: (l + 1,) + (0,) * lead + (i, 0))))
            out_specs.append(pl.BlockSpec(blk, lambda i, lead=lead: (0,) * lead + (i, 0)))
            out_shape.append(jax.ShapeDtypeStruct(a.shape[1:], BF16))
    outs = pl.pallas_call(
        functools.partial(_ffn_kernel, sub=min(tm, SUB_FFN), n_cvt=n_cvt),
        grid=(nsteps,),
        in_specs=[spec for _, spec in operands],
        out_specs=out_specs,
        out_shape=out_shape,
        scratch_shapes=[pltpu.VMEM((tm, D), BF16)],
        compiler_params=pltpu.CompilerParams(dimension_semantics=("arbitrary",),
                                             vmem_limit_bytes=VMEM_LIMIT_FFN),
        name="ffn",
    )(*[a for a, _ in operands])
    return outs[0], outs[1], (dict(zip(MIXER_MATS, outs[2:])) if convert_next else None)


def kernel(x_prompt, x_sample, state_conv, state_pool, w_in, lnv_g, lnv_b, w_spatial, b_spatial, w_proj_a, conv_w, conv_b, w_proj_b, w_pool, pool_scale, w_proj_c, w_o, ln1_g, ln1_b, w_ff1, w_ff2, ln2_g, ln2_b):
    bp, seq, _ = x_prompt.shape
    bs = x_sample.shape[0]
    xp = x_prompt.reshape(bp * seq, D)
    xs = x_sample.reshape(bs, D)
    conv_p, pool_p, conv_s, pool_s, chunk_v_s = [], [], [], [], []
    rows = lambda a: a[:, None, :]
    p = {
        'w_in': w_in, 'lnv_g': rows(lnv_g), 'lnv_b': rows(lnv_b),
        'w_spatial': w_spatial, 'b_spatial_t': jnp.swapaxes(b_spatial, 1, 2),
        'w_spatial_d': rows(jnp.repeat(w_spatial[:, :, 0, 0], G_A, axis=1)),
        'b_spatial_d': rows(jnp.repeat(b_spatial[:, :, 0], G_A, axis=1)),
        'w_proj_a': w_proj_a, 'conv_w': conv_w, 'conv_b': rows(conv_b),
        'w_proj_b': w_proj_b, 'w_pool': w_pool, 'pool_scale': rows(pool_scale),
        'w_proj_c': w_proj_c, 'w_o': w_o, 'ln1_g': rows(ln1_g), 'ln1_b': rows(ln1_b),
        'w_ff1': w_ff1, 'w_ff2': w_ff2, 'ln2_g': rows(ln2_g), 'ln2_b': rows(ln2_b),
        'state_conv': state_conv, 'state_pool': state_pool,
    }
    mats = None
    for l in range(DEPTH):
        (hs, z_new, xc_new, vn), mats = _sample_mixer(xs, mats, p, l)
        conv_s.append(jnp.concatenate([state_conv[l][:, 1:], z_new[:, None, :]], axis=1))
        pool_s.append(jnp.concatenate([state_pool[l][:, 1:], xc_new[:, None, :]], axis=1))
        chunk_v_s.append(vn[:, None, :])
        hp, nconv, npool, w1, w2 = _mixer_prompt(xp, bp, seq, TM_MIXER, mats, p, l)
        conv_p.append(nconv)
        pool_p.append(npool)
        xp, xs, mats = _ffn(hp, hs, TM_FFN, w1, w2, p, l, convert_next=l + 1 < DEPTH)
    return (xp.reshape(bp, seq, D), xs.reshape(bs, 1, D), jnp.stack(conv_p), jnp.stack(pool_p),
            jnp.stack(conv_s), jnp.stack(pool_s), jnp.stack(chunk_v_s))
```

```python
import functools

import jax
import jax.numpy as jnp
from jax import lax
from jax.experimental import pallas as pl
from jax.experimental.pallas import tpu as pltpu

D = 1024
N_GROUPS_A = 4
CHUNK = 128
G_A = D // N_GROUPS_A
CONV_W = 3
POOL_WINDOWS = (2, 4, 8, 16)
G_C = D // len(POOL_WINDOWS)
POOL_BUF = max(POOL_WINDOWS) - 1
D_FF = 4 * D
DEPTH = 2
ALPHA = float((2 * DEPTH) ** 0.25)
LN_EPS = 1e-5
PAST_LEN = 16384

COL_U, COL_V, COL_BG, COL_CG, COL_XB, COL_XC, COL_GATE = 0, 1, 2, 3, 4, 5, 6
N_COLS = 9
N_ROW_PIECES = 8
ROW_PIECE = D // N_ROW_PIECES

SUBLANES = 8
LANES = 128
N_LT = D // LANES
LT_G = G_C // LANES
Z_HEAD = SUBLANES
P_HEAD = 32
P_LO = 16
P_CARRY = 16
assert CONV_W - 1 <= Z_HEAD and POOL_BUF <= P_CARRY <= P_HEAD - P_LO and P_CARRY % SUBLANES == 0
assert PAST_LEN % CHUNK == 0 and PAST_LEN >= max(POOL_WINDOWS)

TM_MIXER = 512
TM_FFN = 1024
SUB_MIXER = 256
SUB_FFN = 256
VMEM_LIMIT_MIXER = 56 * 1024 * 1024
VMEM_LIMIT_FFN = 48 * 1024 * 1024

F32 = jnp.float32
BF16 = jnp.bfloat16


def _dot(a, b):
    return jnp.dot(a, b, preferred_element_type=F32)


def _layer_norm(x, g, b):
    mu = jnp.mean(x, axis=-1, keepdims=True)
    xc = x - mu
    var = jnp.mean(xc * xc, axis=-1, keepdims=True)
    return xc * lax.rsqrt(var + LN_EPS) * g + b


def _mixer_prompt_kernel(x_ref, win_ref, lnvg_ref, lnvb_ref, wsp_ref, bsp_ref, wpa_ref, cw_ref, cb_ref,
                         wpb_ref, wpool_ref, psc_ref, wpc_ref, wo_ref, l1g_ref, l1b_ref, wf1_ref, wf2_ref,
                         h_ref, nconv_ref, npool_ref, wf1o_ref, wf2o_ref,
                         xb_s, vn_s, u_s, ha_s, hb_s, hc_s, acc_s, z_s, p_s, la_s, lb_s, lc_s, *, tm, sub):
    t = pl.program_id(1)

    @pl.when(t == 0)
    def _():
        z_s[:, 0:Z_HEAD, :] = jnp.zeros((N_LT, Z_HEAD, LANES), F32)
        p_s[:, 0:P_HEAD, :] = jnp.zeros((N_LT, P_HEAD, LANES), F32)
        la_s[:, 0:P_LO, :] = jnp.zeros((N_LT, P_LO, LANES), F32)
        lb_s[:, 0:P_LO, :] = jnp.zeros((N_LT - LT_G, P_LO, LANES), F32)
        lc_s[:, 0:P_LO, :] = jnp.zeros((N_LT - 2 * LT_G, P_LO, LANES), F32)

    row = lax.broadcasted_iota(jnp.int32, (CHUNK, CHUNK), 0)
    col = lax.broadcasted_iota(jnp.int32, (CHUNK, CHUNK), 1)
    causal = col <= row
    w_spatial = [jnp.where(causal, wsp_ref[g], 0.0).astype(BF16) for g in range(N_GROUPS_A)]
    cw = cw_ref[...]
    psc = psc_ref[...]

    for r0 in range(0, tm, sub):
        rs = slice(r0, r0 + sub)
        xb_s[rs, :] = x_ref[rs, :].astype(BF16)

        def proj(k):
            return _dot(xb_s[rs, :], win_ref[:, k * D:(k + 1) * D])

        xc = proj(COL_XC)
        p0, p1 = P_HEAD + r0, P_HEAD + r0 + sub
        for j in range(N_LT):
            p_s[j, p0:p1, :] = xc[:, j * LANES:(j + 1) * LANES]
        lo = P_LO if r0 == 0 else p0
        for j in range(N_LT):
            la_s[j, lo:p1, :] = p_s[j, lo:p1, :] + p_s[j, lo - 1:p1 - 1, :]
        for j in range(N_LT - LT_G):
            lb_s[j, lo:p1, :] = la_s[j + LT_G, lo:p1, :] + la_s[j + LT_G, lo - 2:p1 - 2, :]
        for j in range(N_LT - 2 * LT_G):
            lc_s[j, lo:p1, :] = lb_s[j + LT_G, lo:p1, :] + lb_s[j + LT_G, lo - 4:p1 - 4, :]
        sum16 = [lc_s[j + LT_G, p0:p1, :] + lc_s[j + LT_G, p0 - 8:p1 - 8, :] for j in range(LT_G)]
        sums = tuple(jnp.concatenate(tiles, axis=-1) for tiles in (
            [la_s[j, p0:p1, :] for j in range(LT_G)], [lb_s[j, p0:p1, :] for j in range(LT_G)],
            [lc_s[j, p0:p1, :] for j in range(LT_G)], sum16))
        pos1 = t * tm + r0 + lax.broadcasted_iota(jnp.int32, (sub, 1), 0) + 1
        for g, w in enumerate(POOL_WINDOWS):
            cs = slice(g * G_C, (g + 1) * G_C)
            inv_cnt = 1.0 / jnp.minimum(pos1, w).astype(F32)
            d = sums[g] * inv_cnt - xc[:, cs]
            hc_s[rs, cs] = (_dot(d.astype(BF16), wpool_ref[g]) * psc[:, cs]).astype(BF16)
        acc_s[rs, :] = jax.nn.sigmoid(proj(COL_GATE + 2)) * _dot(hc_s[rs, :], wpc_ref[...])

        vn_s[rs, :] = _layer_norm(proj(COL_V), lnvg_ref[...], lnvb_ref[...]).astype(BF16)
        u_s[rs, :] = proj(COL_U)
        for g in range(N_GROUPS_A):
            bias = bsp_ref[:, g:g + 1]
            cs = slice(g * G_A, (g + 1) * G_A)
            for c0 in range(r0, r0 + sub, CHUNK):
                ch = slice(c0, c0 + CHUNK)
                s = _dot(w_spatial[g], vn_s[ch, cs]) + bias
                ha_s[ch, cs] = (u_s[ch, cs] * s).astype(BF16)
        acc_s[rs, :] += jax.nn.sigmoid(proj(COL_GATE + 0)) * _dot(ha_s[rs, :], wpa_ref[...])

        z = proj(COL_CG) * proj(COL_XB)
        z0 = Z_HEAD + r0
        for j in range(N_LT):
            z_s[j, z0:z0 + sub, :] = z[:, j * LANES:(j + 1) * LANES]
        z_m2 = jnp.concatenate([z_s[j, z0 - 2:z0 - 2 + sub, :] for j in range(N_LT)], axis=-1)
        z_m1 = jnp.concatenate([z_s[j, z0 - 1:z0 - 1 + sub, :] for j in range(N_LT)], axis=-1)
        y = cb_ref[...] + (cw[0:1] * z_m2 + cw[1:2] * z_m1 + cw[2:3] * z)
        hb_s[rs, :] = (proj(COL_BG) * y).astype(BF16)
        acc_s[rs, :] += jax.nn.sigmoid(proj(COL_GATE + 1)) * _dot(hb_s[rs, :], wpb_ref[...])

        o = _dot(acc_s[rs, :].astype(BF16), wo_ref[...])
        h_ref[rs, :] = _layer_norm(ALPHA * x_ref[rs, :] + o, l1g_ref[...], l1b_ref[...])

    tail = z_s[:, tm:tm + Z_HEAD, :]
    z_s[:, 0:Z_HEAD, :] = tail
    hist = p_s[:, tm + P_HEAD - P_CARRY:tm + P_HEAD, :]
    p_s[:, P_HEAD - P_CARRY:P_HEAD, :] = hist
    for j in range(N_LT):
        nconv_ref[:, j * LANES:(j + 1) * LANES] = z_s[j, tm + Z_HEAD - (CONV_W - 1):tm + Z_HEAD, :]
        npool_ref[:, j * LANES:(j + 1) * LANES] = p_s[j, tm + P_HEAD - POOL_BUF:tm + P_HEAD, :]

    wf1o_ref[...] = wf1_ref[...].astype(BF16)
    wf2o_ref[...] = wf2_ref[...].astype(BF16)


def _sample_mixer_kernel(x_ref, sconv_ref, sp0_ref, sp1_ref, sp2_ref, sp3_ref, win_ref, lnvg_ref, lnvb_ref, wsd_ref,
                         bsd_ref, wpa_ref, cw_ref, cb_ref, wpb_ref, wpool_ref, psc_ref, wpc_ref, wo_ref, l1g_ref,
                         l1b_ref, h_ref, z_ref, xc_ref, vn_ref, *rest, emit_bf16):
    if emit_bf16:
        wino_ref, wpao_ref, wpbo_ref, wpoolo_ref, wpco_ref, woo_ref = rest[:len(MIXER_MATS)]
    else:
        wino_ref = wpao_ref = wpbo_ref = wpoolo_ref = wpco_ref = woo_ref = None
    xb_s, proj_s, wpa_s, wpb_s, wpc_s, wo_s = rest[len(MIXER_MATS) if emit_bf16 else 0:]
    k = pl.program_id(0)

    @pl.when(k == 0)
    def _():
        xb_s[...] = x_ref[...].astype(BF16)

    @pl.when(k < N_COLS)
    def _():
        w = win_ref[...].astype(BF16)
        if emit_bf16:
            wino_ref[...] = w
        proj_s[k] = _dot(xb_s[...], w)

    @pl.when(k < N_ROW_PIECES)
    def _():
        rows = pl.ds(pl.multiple_of(k * ROW_PIECE, ROW_PIECE), ROW_PIECE)
        for src, dst, keep in ((wpa_ref, wpao_ref, wpa_s), (wpb_ref, wpbo_ref, wpb_s), (wpc_ref, wpco_ref, wpc_s),
                               (wo_ref, woo_ref, wo_s)):
            piece = src[...].astype(BF16)
            if emit_bf16:
                dst[...] = piece
            keep[rows, :] = piece

    @pl.when(k == N_COLS)
    def _():
        x = x_ref[...]
        wpa, wpb, wpc, wo = wpa_s[...], wpb_s[...], wpc_s[...], wo_s[...]
        wpool = wpool_ref[...].astype(BF16)
        if emit_bf16:
            wpoolo_ref[...] = wpool

        vn = _layer_norm(proj_s[COL_V], lnvg_ref[...], lnvb_ref[...])
        vn_ref[...] = vn
        s = wsd_ref[...] * vn + bsd_ref[...]
        ha = (proj_s[COL_U] * s).astype(BF16)
        acc = jax.nn.sigmoid(proj_s[COL_GATE + 0]) * _dot(ha, wpa)

        z = proj_s[COL_CG] * proj_s[COL_XB]
        z_ref[...] = z
        cw = cw_ref[...]
        y = cb_ref[...] + (cw[0:1] * sconv_ref[0] + cw[1:2] * sconv_ref[1] + cw[2:3] * z)
        hb = (proj_s[COL_BG] * y).astype(BF16)
        acc = acc + jax.nn.sigmoid(proj_s[COL_GATE + 1]) * _dot(hb, wpb)

        xc = proj_s[COL_XC]
        xc_ref[...] = xc
        psc = psc_ref[...]
        hc = []
        for g, (w, sp_ref) in enumerate(zip(POOL_WINDOWS, (sp0_ref, sp1_ref, sp2_ref, sp3_ref))):
            cs = slice(g * G_C, (g + 1) * G_C)
            tot = xc[:, cs]
            for j in range(w - 1):
                tot = tot + sp_ref[w - 2 - j]
            d = tot * (1.0 / w) - xc[:, cs]
            hc.append((_dot(d.astype(BF16), wpool[g]) * psc[:, cs]).astype(BF16))
        hc = jnp.concatenate(hc, axis=-1)
        acc = acc + jax.nn.sigmoid(proj_s[COL_GATE + 2]) * _dot(hc, wpc)

        o = _dot(acc.astype(BF16), wo)
        h_ref[...] = _layer_norm(ALPHA * x + o, l1g_ref[...], l1b_ref[...])


def _ffn_rows(h, hb, w1_ref, w2_ref, g_ref, b_ref):
    acc = None
    for j in range(D_FF // D):
        f = _dot(hb, w1_ref[:, j * D:(j + 1) * D])
        a = jnp.square(jnp.maximum(f, 0.0)).astype(BF16)
        c = _dot(a, w2_ref[j * D:(j + 1) * D, :])
        acc = c if acc is None else acc + c
    return _layer_norm(ALPHA * h + acc, g_ref[...], b_ref[...])


def _ffn_kernel(*refs, sub, n_cvt):
    h_ref, hs_ref, w1_ref, w2_ref, g_ref, b_ref = refs[:6]
    cvt_in = refs[6:6 + n_cvt]
    o_ref, os_ref = refs[6 + n_cvt:8 + n_cvt]
    cvt_out = refs[8 + n_cvt:8 + 2 * n_cvt]
    (hb_s,) = refs[8 + 2 * n_cvt:]

    for r0 in range(0, h_ref.shape[0], sub):
        rs = slice(r0, r0 + sub)
        hb_s[rs, :] = h_ref[rs, :].astype(BF16)
        o_ref[rs, :] = _ffn_rows(h_ref[rs, :], hb_s[rs, :], w1_ref, w2_ref, g_ref, b_ref)

    @pl.when(pl.program_id(0) == pl.num_programs(0) - 1)
    def _():
        hs = hs_ref[...]
        os_ref[...] = _ffn_rows(hs, hs.astype(BF16), w1_ref, w2_ref, g_ref, b_ref)

    for src, dst in zip(cvt_in, cvt_out):
        dst[...] = src[...].astype(BF16)


def _resident(shape):
    nd = len(shape)
    return pl.BlockSpec(shape, lambda *_: (0,) * nd, pipeline_mode=pl.Buffered(1))


def _layer_resident(a, l):
    return pl.BlockSpec((None,) + a.shape[1:], lambda *_: (l,) + (0,) * (a.ndim - 1), pipeline_mode=pl.Buffered(1))


MIXER_MATS = ('w_in', 'w_proj_a', 'w_proj_b', 'w_pool', 'w_proj_c', 'w_o')


def _mixer_prompt(x2d, batch, seq, tm, mats, p, l):
    assert x2d.shape == (batch * seq, D) and x2d.dtype == F32 and seq % tm == 0 and tm % SUB_MIXER == 0
    assert SUB_MIXER % CHUNK == 0 and SUB_MIXER >= P_CARRY
    nt = seq // tm
    nsteps = batch * nt
    assert D % nsteps == 0 and (D // nsteps) % (2 * SUBLANES) == 0
    row_block = pl.BlockSpec((tm, D), lambda b, t: (b * nt + t, 0))
    r1, r2 = D // nsteps, D_FF // nsteps
    lay = lambda k: (p[k], _layer_resident(p[k], l))
    mat = lambda k: (mats[k], _resident(mats[k].shape))
    operands = [(x2d, row_block), mat('w_in'), lay('lnv_g'), lay('lnv_b'), lay('w_spatial'), lay('b_spatial_t'),
                mat('w_proj_a'), lay('conv_w'), lay('conv_b'), mat('w_proj_b'), mat('w_pool'), lay('pool_scale'),
                mat('w_proj_c'), mat('w_o'), lay('ln1_g'), lay('ln1_b'),
                (p['w_ff1'], pl.BlockSpec((None, r1, D_FF), lambda b, t: (l, b * nt + t, 0))),
                (p['w_ff2'], pl.BlockSpec((None, r2, D), lambda b, t: (l, b * nt + t, 0)))]
    return pl.pallas_call(
        functools.partial(_mixer_prompt_kernel, tm=tm, sub=min(tm, SUB_MIXER)),
        grid=(batch, nt),
        in_specs=[spec for _, spec in operands],
        out_specs=[row_block,
                   pl.BlockSpec((None, CONV_W - 1, D), lambda b, t: (b, 0, 0)),
                   pl.BlockSpec((None, POOL_BUF, D), lambda b, t: (b, 0, 0)),
                   pl.BlockSpec((r1, D_FF), lambda b, t: (b * nt + t, 0)),
                   pl.BlockSpec((r2, D), lambda b, t: (b * nt + t, 0))],
        out_shape=[jax.ShapeDtypeStruct((batch * seq, D), F32),
                   jax.ShapeDtypeStruct((batch, CONV_W - 1, D), F32),
                   jax.ShapeDtypeStruct((batch, POOL_BUF, D), F32),
                   jax.ShapeDtypeStruct((D, D_FF), BF16),
                   jax.ShapeDtypeStruct((D_FF, D), BF16)],
        scratch_shapes=[pltpu.VMEM((tm, D), BF16),
                        pltpu.VMEM((tm, D), BF16),
                        pltpu.VMEM((tm, D), F32),
                        pltpu.VMEM((tm, D), BF16),
                        pltpu.VMEM((tm, D), BF16),
                        pltpu.VMEM((tm, D), BF16),
                        pltpu.VMEM((tm, D), F32),
                        pltpu.VMEM((N_LT, Z_HEAD + tm, LANES), F32),
                        pltpu.VMEM((N_LT, P_HEAD + tm, LANES), F32),
                        pltpu.VMEM((N_LT, P_HEAD + tm, LANES), F32),
                        pltpu.VMEM((N_LT - LT_G, P_HEAD + tm, LANES), F32),
                        pltpu.VMEM((N_LT - 2 * LT_G, P_HEAD + tm, LANES), F32)],
        compiler_params=pltpu.CompilerParams(dimension_semantics=("arbitrary", "arbitrary"),
                                             vmem_limit_bytes=VMEM_LIMIT_MIXER),
        name="mixer_prompt",
    )(*[a for a, _ in operands])


def _sample_mixer(x2d, mats, p, l):
    n = x2d.shape[0]
    emit_bf16 = mats is None
    lay = lambda k: (p[k], _layer_resident(p[k], l))
    col_block = lambda k: (0, jnp.minimum(k, N_COLS - 1))
    row_piece = lambda k: (jnp.minimum(k, N_ROW_PIECES - 1), 0)
    if emit_bf16:
        w_in = (p['w_in'], pl.BlockSpec((None, D, D), lambda k: (l,) + col_block(k)))
        sq = lambda name: (p[name], pl.BlockSpec((None, ROW_PIECE, D), lambda k: (l,) + row_piece(k)))
        w_pool = lay('w_pool')
    else:
        w_in = (mats['w_in'], pl.BlockSpec((D, D), col_block))
        sq = lambda name: (mats[name], pl.BlockSpec((ROW_PIECE, D), row_piece))
        w_pool = (mats['w_pool'], _resident(mats['w_pool'].shape))
    history = [jnp.swapaxes(p['state_conv'][l], 0, 1)]
    history += [jnp.swapaxes(p['state_pool'][l, :, POOL_BUF - (w - 1):, g * G_C:(g + 1) * G_C], 0, 1)
                for g, w in enumerate(POOL_WINDOWS)]
    operands = [(x2d, _resident(x2d.shape))] + [(a, _resident(a.shape)) for a in history] + [
                w_in,
                lay('lnv_g'), lay('lnv_b'), lay('w_spatial_d'), lay('b_spatial_d'), sq('w_proj_a'), lay('conv_w'),
                lay('conv_b'), sq('w_proj_b'), w_pool, lay('pool_scale'), sq('w_proj_c'), sq('w_o'),
                lay('ln1_g'), lay('ln1_b')]
    once = lambda shape: pl.BlockSpec(shape, lambda k: (0,) * len(shape), pipeline_mode=pl.Buffered(1))
    out_specs = [once((n, D))] * 4
    out_shape = [jax.ShapeDtypeStruct((n, D), F32)] * 4
    if emit_bf16:
        sq_out = pl.BlockSpec((ROW_PIECE, D), row_piece)
        out_specs += [pl.BlockSpec((D, D), col_block), sq_out, sq_out, once(p['w_pool'].shape[1:]), sq_out, sq_out]
        out_shape += [jax.ShapeDtypeStruct(p[k].shape[1:], BF16) for k in MIXER_MATS]
    outs = pl.pallas_call(
        functools.partial(_sample_mixer_kernel, emit_bf16=emit_bf16),
        grid=(N_COLS + 1,),
        in_specs=[spec for _, spec in operands],
        out_specs=out_specs,
        out_shape=out_shape,
        scratch_shapes=[pltpu.VMEM((n, D), BF16),
                        pltpu.VMEM((N_COLS, n, D), F32)]
                       + [pltpu.VMEM((D, D), BF16)] * 4,
        compiler_params=pltpu.CompilerParams(
            dimension_semantics=("arbitrary",), vmem_limit_bytes=VMEM_LIMIT_MIXER,
            allow_input_fusion=[1 <= i <= len(history) for i in range(len(operands))]),
        name="sample_mixer",
    )(*[a for a, _ in operands])
    return outs[:4], (dict(zip(MIXER_MATS, outs[4:])) if emit_bf16 else mats)


def _ffn(h2d, hs2d, tm, w1, w2, p, l, convert_next):
    n = h2d.shape[0]
    assert h2d.shape == (n, D) and h2d.dtype == F32 and n % tm == 0 and tm % SUB_FFN == 0
    nsteps = n // tm
    assert not convert_next or all((p[k].shape[-2] // nsteps) % (2 * SUBLANES) == 0 for k in MIXER_MATS)
    row_block = pl.BlockSpec((tm, D), lambda i: (i, 0))
    lay = lambda k: (p[k], _layer_resident(p[k], l))
    operands = [(h2d, row_block), (hs2d, _resident(hs2d.shape)), (w1, _resident(w1.shape)), (w2, _resident(w2.shape)),
                lay('ln2_g'), lay('ln2_b')]
    out_specs = [row_block, pl.BlockSpec(hs2d.shape, lambda i: (0, 0))]
    out_shape = [jax.ShapeDtypeStruct((n, D), F32), jax.ShapeDtypeStruct(hs2d.shape, F32)]
    n_cvt = len(MIXER_MATS) if convert_next else 0
    if convert_next:
        for k in MIXER_MATS:
            a = p[k]
            r = a.shape[-2] // nsteps
            lead = a.ndim - 3
            blk = a.shape[1:-2] + (r, a.shape[-1])
            operands.append((a, pl.BlockSpec((None,) + blk, lambda i, lead=lead: (l + 1,) + (0,) * lead + (i, 0))))
            out_specs.append(pl.BlockSpec(blk, lambda i, lead=lead: (0,) * lead + (i, 0)))
            out_shape.append(jax.ShapeDtypeStruct(a.shape[1:], BF16))
    outs = pl.pallas_call(
        functools.partial(_ffn_kernel, sub=min(tm, SUB_FFN), n_cvt=n_cvt),
        grid=(nsteps,),
        in_specs=[spec for _, spec in operands],
        out_specs=out_specs,
        out_shape=out_shape,
        scratch_shapes=[pltpu.VMEM((tm, D), BF16)],
        compiler_params=pltpu.CompilerParams(dimension_semantics=("arbitrary",),
                                             vmem_limit_bytes=VMEM_LIMIT_FFN),
        name="ffn",
    )(*[a for a, _ in operands])
    return outs[0], outs[1], (dict(zip(MIXER_MATS, outs[2:])) if convert_next else None)


def kernel(x_prompt, x_sample, state_conv, state_pool, w_in, lnv_g, lnv_b, w_spatial, b_spatial, w_proj_a, conv_w, conv_b, w_proj_b, w_pool, pool_scale, w_proj_c, w_o, ln1_g, ln1_b, w_ff1, w_ff2, ln2_g, ln2_b):
    bp, seq, _ = x_prompt.shape
    bs = x_sample.shape[0]
    assert x_prompt.shape == (bp, seq, D) and x_sample.shape == (bs, 1, D) and w_in.shape == (DEPTH, D, N_COLS * D)
    assert state_conv.shape == (DEPTH, bs, CONV_W - 1, D) and state_pool.shape == (DEPTH, bs, POOL_BUF, D)
    xp = x_prompt.reshape(bp * seq, D)
    xs = x_sample.reshape(bs, D)
    conv_p, pool_p, conv_s, pool_s, chunk_v_s = [], [], [], [], []
    rows = lambda a: a[:, None, :]
    p = {
        'w_in': w_in, 'lnv_g': rows(lnv_g), 'lnv_b': rows(lnv_b),
        'w_spatial': w_spatial, 'b_spatial_t': jnp.swapaxes(b_spatial, 1, 2),
        'w_spatial_d': rows(jnp.repeat(w_spatial[:, :, 0, 0], G_A, axis=1)),
        'b_spatial_d': rows(jnp.repeat(b_spatial[:, :, 0], G_A, axis=1)),
        'w_proj_a': w_proj_a, 'conv_w': conv_w, 'conv_b': rows(conv_b),
        'w_proj_b': w_proj_b, 'w_pool': w_pool, 'pool_scale': rows(pool_scale),
        'w_proj_c': w_proj_c, 'w_o': w_o, 'ln1_g': rows(ln1_g), 'ln1_b': rows(ln1_b),
        'w_ff1': w_ff1, 'w_ff2': w_ff2, 'ln2_g': rows(ln2_g), 'ln2_b': rows(ln2_b),
        'state_conv': state_conv, 'state_pool': state_pool,
    }
    mats = None
    for l in range(DEPTH):
        (hs, z_new, xc_new, vn), mats = _sample_mixer(xs, mats, p, l)
        conv_s.append(jnp.concatenate([state_conv[l][:, 1:], z_new[:, None, :]], axis=1))
        pool_s.append(jnp.concatenate([state_pool[l][:, 1:], xc_new[:, None, :]], axis=1))
        chunk_v_s.append(vn[:, None, :])
        hp, nconv, npool, w1, w2 = _mixer_prompt(xp, bp, seq, TM_MIXER, mats, p, l)
        conv_p.append(nconv)
        pool_p.append(npool)
        xp, xs, mats = _ffn(hp, hs, TM_FFN, w1, w2, p, l, convert_next=l + 1 < DEPTH)
    return (xp.reshape(bp, seq, D), xs.reshape(bs, 1, D), jnp.stack(conv_p), jnp.stack(pool_p),
            jnp.stack(conv_s), jnp.stack(pool_s), jnp.stack(chunk_v_s))
```

```python
import functools

import jax
import jax.numpy as jnp
from jax import lax
from jax.experimental import pallas as pl
from jax.experimental.pallas import tpu as pltpu

D = 1024
N_GROUPS_A = 4
CHUNK = 128
G_A = D // N_GROUPS_A
CONV_W = 3
POOL_WINDOWS = (2, 4, 8, 16)
G_C = D // len(POOL_WINDOWS)
POOL_BUF = max(POOL_WINDOWS) - 1
D_FF = 4 * D
DEPTH = 2
ALPHA = float((2 * DEPTH) ** 0.25)
LN_EPS = 1e-5
PAST_LEN = 16384

COL_U, COL_V, COL_BG, COL_CG, COL_XB, COL_XC, COL_GATE = 0, 1, 2, 3, 4, 5, 6
N_COLS = 9
N_ROW_PIECES = 8
ROW_PIECE = D // N_ROW_PIECES

SUBLANES = 8
LANES = 128
N_LT = D // LANES
LT_G = G_C // LANES
Z_HEAD = SUBLANES
P_HEAD = 32
P_LO = 16
P_CARRY = 16
assert CONV_W - 1 <= Z_HEAD and POOL_BUF <= P_CARRY <= P_HEAD - P_LO and P_CARRY % SUBLANES == 0
assert PAST_LEN % CHUNK == 0 and PAST_LEN >= max(POOL_WINDOWS)

TM_MIXER = 512
TM_FFN = 1024
SUB_MIXER = 256
SUB_FFN = 256
VMEM_LIMIT_MIXER = 56 * 1024 * 1024
VMEM_LIMIT_FFN = 48 * 1024 * 1024

F32 = jnp.float32
BF16 = jnp.bfloat16


def _dot(a, b):
    return jnp.dot(a, b, preferred_element_type=F32)


def _layer_norm(x, g, b):
    mu = jnp.mean(x, axis=-1, keepdims=True)
    xc = x - mu
    var = jnp.mean(xc * xc, axis=-1, keepdims=True)
    return xc * lax.rsqrt(var + LN_EPS) * g + b


def _mixer_prompt_kernel(x_ref, win_ref, lnvg_ref, lnvb_ref, wsp_ref, bsp_ref, wpa_ref, cw_ref, cb_ref,
                         wpb_ref, wpool_ref, psc_ref, wpc_ref, wo_ref, l1g_ref, l1b_ref, wf1_ref, wf2_ref,
                         h_ref, nconv_ref, npool_ref, wf1o_ref, wf2o_ref,
                         xb_s, vn_s, u_s, ha_s, hb_s, hc_s, acc_s, z_s, p_s, la_s, lb_s, lc_s, *, tm, sub):
    t = pl.program_id(1)

    @pl.when(t == 0)
    def _():
        z_s[:, 0:Z_HEAD, :] = jnp.zeros((N_LT, Z_HEAD, LANES), F32)
        p_s[:, 0:P_HEAD, :] = jnp.zeros((N_LT, P_HEAD, LANES), F32)
        la_s[:, 0:P_LO, :] = jnp.zeros((N_LT, P_LO, LANES), F32)
        lb_s[:, 0:P_LO, :] = jnp.zeros((N_LT - LT_G, P_LO, LANES), F32)
        lc_s[:, 0:P_LO, :] = jnp.zeros((N_LT - 2 * LT_G, P_LO, LANES), F32)

    row = lax.broadcasted_iota(jnp.int32, (CHUNK, CHUNK), 0)
    col = lax.broadcasted_iota(jnp.int32, (CHUNK, CHUNK), 1)
    causal = col <= row
    w_spatial = [jnp.where(causal, wsp_ref[g], 0.0).astype(BF16) for g in range(N_GROUPS_A)]
    cw = cw_ref[...]
    psc = psc_ref[...]

    for r0 in range(0, tm, sub):
        rs = slice(r0, r0 + sub)
        xb_s[rs, :] = x_ref[rs, :].astype(BF16)

        def proj(k):
            return _dot(xb_s[rs, :], win_ref[:, k * D:(k + 1) * D])

        xc = proj(COL_XC)
        p0, p1 = P_HEAD + r0, P_HEAD + r0 + sub
        for j in range(N_LT):
            p_s[j, p0:p1, :] = xc[:, j * LANES:(j + 1) * LANES]
        lo = P_LO if r0 == 0 else p0
        for j in range(N_LT):
            la_s[j, lo:p1, :] = p_s[j, lo:p1, :] + p_s[j, lo - 1:p1 - 1, :]
        for j in range(N_LT - LT_G):
            lb_s[j, lo:p1, :] = la_s[j + LT_G, lo:p1, :] + la_s[j + LT_G, lo - 2:p1 - 2, :]
        for j in range(N_LT - 2 * LT_G):
            lc_s[j, lo:p1, :] = lb_s[j + LT_G, lo:p1, :] + lb_s[j + LT_G, lo - 4:p1 - 4, :]
        sum16 = [lc_s[j + LT_G, p0:p1, :] + lc_s[j + LT_G, p0 - 8:p1 - 8, :] for j in range(LT_G)]
        sums = tuple(jnp.concatenate(tiles, axis=-1) for tiles in (
            [la_s[j, p0:p1, :] for j in range(LT_G)], [lb_s[j, p0:p1, :] for j in range(LT_G)],
            [lc_s[j, p0:p1, :] for j in range(LT_G)], sum16))
        pos1 = t * tm + r0 + lax.broadcasted_iota(jnp.int32, (sub, 1), 0) + 1
        for g, w in enumerate(POOL_WINDOWS):
            cs = slice(g * G_C, (g + 1) * G_C)
            inv_cnt = 1.0 / jnp.minimum(pos1, w).astype(F32)
            d = sums[g] * inv_cnt - xc[:, cs]
            hc_s[rs, cs] = (_dot(d.astype(BF16), wpool_ref[g]) * psc[:, cs]).astype(BF16)
        acc_s[rs, :] = jax.nn.sigmoid(proj(COL_GATE + 2)) * _dot(hc_s[rs, :], wpc_ref[...])

        vn_s[rs, :] = _layer_norm(proj(COL_V), lnvg_ref[...], lnvb_ref[...]).astype(BF16)
        u_s[rs, :] = proj(COL_U)
        for g in range(N_GROUPS_A):
            bias = bsp_ref[:, g:g + 1]
            cs = slice(g * G_A, (g + 1) * G_A)
            for c0 in range(r0, r0 + sub, CHUNK):
                ch = slice(c0, c0 + CHUNK)
                s = _dot(w_spatial[g], vn_s[ch, cs]) + bias
                ha_s[ch, cs] = (u_s[ch, cs] * s).astype(BF16)
        acc_s[rs, :] += jax.nn.sigmoid(proj(COL_GATE + 0)) * _dot(ha_s[rs, :], wpa_ref[...])

        z = proj(COL_CG) * proj(COL_XB)
        z0 = Z_HEAD + r0
        for j in range(N_LT):
            z_s[j, z0:z0 + sub, :] = z[:, j * LANES:(j + 1) * LANES]
        z_m2 = jnp.concatenate([z_s[j, z0 - 2:z0 - 2 + sub, :] for j in range(N_LT)], axis=-1)
        z_m1 = jnp.concatenate([z_s[j, z0 - 1:z0 - 1 + sub, :] for j in range(N_LT)], axis=-1)
        y = cb_ref[...] + (cw[0:1] * z_m2 + cw[1:2] * z_m1 + cw[2:3] * z)
        hb_s[rs, :] = (proj(COL_BG) * y).astype(BF16)
        acc_s[rs, :] += jax.nn.sigmoid(proj(COL_GATE + 1)) * _dot(hb_s[rs, :], wpb_ref[...])

        o = _dot(acc_s[rs, :].astype(BF16), wo_ref[...])
        h_ref[rs, :] = _layer_norm(ALPHA * x_ref[rs, :] + o, l1g_ref[...], l1b_ref[...])

    tail = z_s[:, tm:tm + Z_HEAD, :]
    z_s[:, 0:Z_HEAD, :] = tail
    hist = p_s[:, tm + P_HEAD - P_CARRY:tm + P_HEAD, :]
    p_s[:, P_HEAD - P_CARRY:P_HEAD, :] = hist
    for j in range(N_LT):
        nconv_ref[:, j * LANES:(j + 1) * LANES] = z_s[j, tm + Z_HEAD - (CONV_W - 1):tm + Z_HEAD, :]
        npool_ref[:, j * LANES:(j + 1) * LANES] = p_s[j, tm + P_HEAD - POOL_BUF:tm + P_HEAD, :]

    wf1o_ref[...] = wf1_ref[...].astype(BF16)
    wf2o_ref[...] = wf2_ref[...].astype(BF16)


def _sample_mixer_kernel(x_ref, sconv_ref, sp0_ref, sp1_ref, sp2_ref, sp3_ref, win_ref, lnvg_ref, lnvb_ref, wsd_ref,
                         bsd_ref, wpa_ref, cw_ref, cb_ref, wpb_ref, wpool_ref, psc_ref, wpc_ref, wo_ref, l1g_ref,
                         l1b_ref, h_ref, nconv_ref, xc_ref, vn_ref, *rest, emit_bf16):
    if emit_bf16:
        wino_ref, wpao_ref, wpbo_ref, wpoolo_ref, wpco_ref, woo_ref = rest[:len(MIXER_MATS)]
    else:
        wino_ref = wpao_ref = wpbo_ref = wpoolo_ref = wpco_ref = woo_ref = None
    xb_s, proj_s, wpa_s, wpb_s, wpc_s, wo_s = rest[len(MIXER_MATS) if emit_bf16 else 0:]
    k = pl.program_id(0)

    @pl.when(k == 0)
    def _():
        xb_s[...] = x_ref[...].astype(BF16)

    @pl.when(k < N_COLS)
    def _():
        w = win_ref[...].astype(BF16)
        if emit_bf16:
            wino_ref[...] = w
        proj_s[k] = _dot(xb_s[...], w)

    @pl.when(k < N_ROW_PIECES)
    def _():
        rows = pl.ds(pl.multiple_of(k * ROW_PIECE, ROW_PIECE), ROW_PIECE)
        for src, dst, keep in ((wpa_ref, wpao_ref, wpa_s), (wpb_ref, wpbo_ref, wpb_s), (wpc_ref, wpco_ref, wpc_s),
                               (wo_ref, woo_ref, wo_s)):
            piece = src[...].astype(BF16)
            if emit_bf16:
                dst[...] = piece
            keep[rows, :] = piece

    @pl.when(k == N_COLS)
    def _():
        x = x_ref[...]
        wpa, wpb, wpc, wo = wpa_s[...], wpb_s[...], wpc_s[...], wo_s[...]
        wpool = wpool_ref[...].astype(BF16)
        if emit_bf16:
            wpoolo_ref[...] = wpool

        vn = _layer_norm(proj_s[COL_V], lnvg_ref[...], lnvb_ref[...])
        vn_ref[...] = vn
        s = wsd_ref[...] * vn + bsd_ref[...]
        ha = (proj_s[COL_U] * s).astype(BF16)
        acc = jax.nn.sigmoid(proj_s[COL_GATE + 0]) * _dot(ha, wpa)

        z = proj_s[COL_CG] * proj_s[COL_XB]
        for j in range(1, CONV_W - 1):
            nconv_ref[:, j - 1, :] = sconv_ref[j]
        nconv_ref[:, CONV_W - 2, :] = z
        cw = cw_ref[...]
        y = cb_ref[...] + (cw[0:1] * sconv_ref[0] + cw[1:2] * sconv_ref[1] + cw[2:3] * z)
        hb = (proj_s[COL_BG] * y).astype(BF16)
        acc = acc + jax.nn.sigmoid(proj_s[COL_GATE + 1]) * _dot(hb, wpb)

        xc = proj_s[COL_XC]
        xc_ref[...] = xc
        psc = psc_ref[...]
        hc = []
        for g, (w, sp_ref) in enumerate(zip(POOL_WINDOWS, (sp0_ref, sp1_ref, sp2_ref, sp3_ref))):
            cs = slice(g * G_C, (g + 1) * G_C)
            tot = xc[:, cs]
            for j in range(w - 1):
                tot = tot + sp_ref[w - 2 - j]
            d = tot * (1.0 / w) - xc[:, cs]
            hc.append((_dot(d.astype(BF16), wpool[g]) * psc[:, cs]).astype(BF16))
        hc = jnp.concatenate(hc, axis=-1)
        acc = acc + jax.nn.sigmoid(proj_s[COL_GATE + 2]) * _dot(hc, wpc)

        o = _dot(acc.astype(BF16), wo)
        h_ref[...] = _layer_norm(ALPHA * x + o, l1g_ref[...], l1b_ref[...])


def _ffn_rows(h, hb, w1_ref, w2_ref, g_ref, b_ref):
    acc = None
    for j in range(D_FF // D):
        f = _dot(hb, w1_ref[:, j * D:(j + 1) * D])
        a = jnp.square(jnp.maximum(f, 0.0)).astype(BF16)
        c = _dot(a, w2_ref[j * D:(j + 1) * D, :])
        acc = c if acc is None else acc + c
    return _layer_norm(ALPHA * h + acc, g_ref[...], b_ref[...])


def _ffn_kernel(*refs, sub, n_cvt):
    h_ref, hs_ref, w1_ref, w2_ref, g_ref, b_ref = refs[:6]
    cvt_in = refs[6:6 + n_cvt]
    o_ref, os_ref = refs[6 + n_cvt:8 + n_cvt]
    cvt_out = refs[8 + n_cvt:8 + 2 * n_cvt]
    (hb_s,) = refs[8 + 2 * n_cvt:]

    for r0 in range(0, h_ref.shape[0], sub):
        rs = slice(r0, r0 + sub)
        hb_s[rs, :] = h_ref[rs, :].astype(BF16)
        o_ref[rs, :] = _ffn_rows(h_ref[rs, :], hb_s[rs, :], w1_ref, w2_ref, g_ref, b_ref)

    @pl.when(pl.program_id(0) == pl.num_programs(0) - 1)
    def _():
        hs = hs_ref[...]
        os_ref[...] = _ffn_rows(hs, hs.astype(BF16), w1_ref, w2_ref, g_ref, b_ref)

    for src, dst in zip(cvt_in, cvt_out):
        dst[...] = src[...].astype(BF16)


def _resident(shape):
    nd = len(shape)
    return pl.BlockSpec(shape, lambda *_: (0,) * nd, pipeline_mode=pl.Buffered(1))


def _layer_resident(a, l):
    return pl.BlockSpec((None,) + a.shape[1:], lambda *_: (l,) + (0,) * (a.ndim - 1), pipeline_mode=pl.Buffered(1))


MIXER_MATS = ('w_in', 'w_proj_a', 'w_proj_b', 'w_pool', 'w_proj_c', 'w_o')


def _mixer_prompt(x2d, batch, seq, tm, mats, p, l):
    assert x2d.shape == (batch * seq, D) and x2d.dtype == F32 and seq % tm == 0 and tm % SUB_MIXER == 0
    assert SUB_MIXER % CHUNK == 0 and SUB_MIXER >= P_CARRY
    nt = seq // tm
    nsteps = batch * nt
    assert D % nsteps == 0 and (D // nsteps) % (2 * SUBLANES) == 0
    row_block = pl.BlockSpec((tm, D), lambda b, t: (b * nt + t, 0))
    r1, r2 = D // nsteps, D_FF // nsteps
    lay = lambda k: (p[k], _layer_resident(p[k], l))
    mat = lambda k: (mats[k], _resident(mats[k].shape))
    operands = [(x2d, row_block), mat('w_in'), lay('lnv_g'), lay('lnv_b'), lay('w_spatial'), lay('b_spatial_t'),
                mat('w_proj_a'), lay('conv_w'), lay('conv_b'), mat('w_proj_b'), mat('w_pool'), lay('pool_scale'),
                mat('w_proj_c'), mat('w_o'), lay('ln1_g'), lay('ln1_b'),
                (p['w_ff1'], pl.BlockSpec((None, r1, D_FF), lambda b, t: (l, b * nt + t, 0))),
                (p['w_ff2'], pl.BlockSpec((None, r2, D), lambda b, t: (l, b * nt + t, 0)))]
    return pl.pallas_call(
        functools.partial(_mixer_prompt_kernel, tm=tm, sub=min(tm, SUB_MIXER)),
        grid=(batch, nt),
        in_specs=[spec for _, spec in operands],
        out_specs=[row_block,
                   pl.BlockSpec((None, CONV_W - 1, D), lambda b, t: (b, 0, 0)),
                   pl.BlockSpec((None, POOL_BUF, D), lambda b, t: (b, 0, 0)),
                   pl.BlockSpec((r1, D_FF), lambda b, t: (b * nt + t, 0)),
                   pl.BlockSpec((r2, D), lambda b, t: (b * nt + t, 0))],
        out_shape=[jax.ShapeDtypeStruct((batch * seq, D), F32),
                   jax.ShapeDtypeStruct((batch, CONV_W - 1, D), F32),
                   jax.ShapeDtypeStruct((batch, POOL_BUF, D), F32),
                   jax.ShapeDtypeStruct((D, D_FF), BF16),
                   jax.ShapeDtypeStruct((D_FF, D), BF16)],
        scratch_shapes=[pltpu.VMEM((tm, D), BF16),
                        pltpu.VMEM((tm, D), BF16),
                        pltpu.VMEM((tm, D), F32),
                        pltpu.VMEM((tm, D), BF16),
                        pltpu.VMEM((tm, D), BF16),
                        pltpu.VMEM((tm, D), BF16),
                        pltpu.VMEM((tm, D), F32),
                        pltpu.VMEM((N_LT, Z_HEAD + tm, LANES), F32),
                        pltpu.VMEM((N_LT, P_HEAD + tm, LANES), F32),
                        pltpu.VMEM((N_LT, P_HEAD + tm, LANES), F32),
                        pltpu.VMEM((N_LT - LT_G, P_HEAD + tm, LANES), F32),
                        pltpu.VMEM((N_LT - 2 * LT_G, P_HEAD + tm, LANES), F32)],
        compiler_params=pltpu.CompilerParams(dimension_semantics=("arbitrary", "arbitrary"),
                                             vmem_limit_bytes=VMEM_LIMIT_MIXER),
        name="mixer_prompt",
    )(*[a for a, _ in operands])


def _sample_mixer(x2d, mats, p, l):
    n = x2d.shape[0]
    emit_bf16 = mats is None
    lay = lambda k: (p[k], _layer_resident(p[k], l))
    col_block = lambda k: (0, jnp.minimum(k, N_COLS - 1))
    row_piece = lambda k: (jnp.minimum(k, N_ROW_PIECES - 1), 0)
    if emit_bf16:
        w_in = (p['w_in'], pl.BlockSpec((None, D, D), lambda k: (l,) + col_block(k)))
        sq = lambda name: (p[name], pl.BlockSpec((None, ROW_PIECE, D), lambda k: (l,) + row_piece(k)))
        w_pool = lay('w_pool')
    else:
        w_in = (mats['w_in'], pl.BlockSpec((D, D), col_block))
        sq = lambda name: (mats[name], pl.BlockSpec((ROW_PIECE, D), row_piece))
        w_pool = (mats['w_pool'], _resident(mats['w_pool'].shape))
    history = [jnp.swapaxes(p['state_conv'][l], 0, 1)]
    history += [jnp.swapaxes(p['state_pool'][l, :, POOL_BUF - (w - 1):, g * G_C:(g + 1) * G_C], 0, 1)
                for g, w in enumerate(POOL_WINDOWS)]
    operands = [(x2d, _resident(x2d.shape))] + [(a, _resident(a.shape)) for a in history] + [
                w_in,
                lay('lnv_g'), lay('lnv_b'), lay('w_spatial_d'), lay('b_spatial_d'), sq('w_proj_a'), lay('conv_w'),
                lay('conv_b'), sq('w_proj_b'), w_pool, lay('pool_scale'), sq('w_proj_c'), sq('w_o'),
                lay('ln1_g'), lay('ln1_b')]
    once = lambda shape: pl.BlockSpec(shape, lambda k: (0,) * len(shape), pipeline_mode=pl.Buffered(1))
    out_specs = [once((n, D)), once((n, CONV_W - 1, D)), once((n, D)), once((n, D))]
    out_shape = [jax.ShapeDtypeStruct(spec.block_shape, F32) for spec in out_specs]
    if emit_bf16:
        sq_out = pl.BlockSpec((ROW_PIECE, D), row_piece)
        out_specs += [pl.BlockSpec((D, D), col_block), sq_out, sq_out, once(p['w_pool'].shape[1:]), sq_out, sq_out]
        out_shape += [jax.ShapeDtypeStruct(p[k].shape[1:], BF16) for k in MIXER_MATS]
    outs = pl.pallas_call(
        functools.partial(_sample_mixer_kernel, emit_bf16=emit_bf16),
        grid=(N_COLS + 1,),
        in_specs=[spec for _, spec in operands],
        out_specs=out_specs,
        out_shape=out_shape,
        scratch_shapes=[pltpu.VMEM((n, D), BF16),
                        pltpu.VMEM((N_COLS, n, D), F32)]
                       + [pltpu.VMEM((D, D), BF16)] * 4,
        compiler_params=pltpu.CompilerParams(
            dimension_semantics=("arbitrary",), vmem_limit_bytes=VMEM_LIMIT_MIXER,
            allow_input_fusion=[1 <= i <= len(history) for i in range(len(operands))]),
        name="sample_mixer",
    )(*[a for a, _ in operands])
    return outs[:4], (dict(zip(MIXER_MATS, outs[4:])) if emit_bf16 else mats)


def _ffn(h2d, hs2d, tm, w1, w2, p, l, convert_next):
    n = h2d.shape[0]
    assert h2d.shape == (n, D) and h2d.dtype == F32 and n % tm == 0 and tm % SUB_FFN == 0
    nsteps = n // tm
    assert not convert_next or all((p[k].shape[-2] // nsteps) % (2 * SUBLANES) == 0 for k in MIXER_MATS)
    row_block = pl.BlockSpec((tm, D), lambda i: (i, 0))
    lay = lambda k: (p[k], _layer_resident(p[k], l))
    operands = [(h2d, row_block), (hs2d, _resident(hs2d.shape)), (w1, _resident(w1.shape)), (w2, _resident(w2.shape)),
                lay('ln2_g'), lay('ln2_b')]
    out_specs = [row_block, pl.BlockSpec(hs2d.shape, lambda i: (0, 0))]
    out_shape = [jax.ShapeDtypeStruct((n, D), F32), jax.ShapeDtypeStruct(hs2d.shape, F32)]
    n_cvt = len(MIXER_MATS) if convert_next else 0
    if convert_next:
        for k in MIXER_MATS:
            a = p[k]
            r = a.shape[-2] // nsteps
            lead = a.ndim - 3
            blk = a.shape[1:-2] + (r, a.shape[-1])
            operands.append((a, pl.BlockSpec((None,) + blk, lambda i, lead=lead: (l + 1,) + (0,) * lead + (i, 0))))
            out_specs.append(pl.BlockSpec(blk, lambda i, lead=lead: (0,) * lead + (i, 0)))
            out_shape.append(jax.ShapeDtypeStruct(a.shape[1:], BF16))
    outs = pl.pallas_call(
        functools.partial(_ffn_kernel, sub=min(tm, SUB_FFN), n_cvt=n_cvt),
        grid=(nsteps,),
        in_specs=[spec for _, spec in operands],
        out_specs=out_specs,
        out_shape=out_shape,
        scratch_shapes=[pltpu.VMEM((tm, D), BF16)],
        compiler_params=pltpu.CompilerParams(dimension_semantics=("arbitrary",),
                                             vmem_limit_bytes=VMEM_LIMIT_FFN),
        name="ffn",
    )(*[a for a, _ in operands])
    return outs[0], outs[1], (dict(zip(MIXER_MATS, outs[2:])) if convert_next else None)


def kernel(x_prompt, x_sample, state_conv, state_pool, w_in, lnv_g, lnv_b, w_spatial, b_spatial, w_proj_a, conv_w, conv_b, w_proj_b, w_pool, pool_scale, w_proj_c, w_o, ln1_g, ln1_b, w_ff1, w_ff2, ln2_g, ln2_b):
    bp, seq, _ = x_prompt.shape
    bs = x_sample.shape[0]
    assert x_prompt.shape == (bp, seq, D) and x_sample.shape == (bs, 1, D) and w_in.shape == (DEPTH, D, N_COLS * D)
    assert state_conv.shape == (DEPTH, bs, CONV_W - 1, D) and state_pool.shape == (DEPTH, bs, POOL_BUF, D)
    xp = x_prompt.reshape(bp * seq, D)
    xs = x_sample.reshape(bs, D)
    conv_p, pool_p, conv_s, pool_s, chunk_v_s = [], [], [], [], []
    rows = lambda a: a[:, None, :]
    p = {
        'w_in': w_in, 'lnv_g': rows(lnv_g), 'lnv_b': rows(lnv_b),
        'w_spatial': w_spatial, 'b_spatial_t': jnp.swapaxes(b_spatial, 1, 2),
        'w_spatial_d': rows(jnp.repeat(w_spatial[:, :, 0, 0], G_A, axis=1)),
        'b_spatial_d': rows(jnp.repeat(b_spatial[:, :, 0], G_A, axis=1)),
        'w_proj_a': w_proj_a, 'conv_w': conv_w, 'conv_b': rows(conv_b),
        'w_proj_b': w_proj_b, 'w_pool': w_pool, 'pool_scale': rows(pool_scale),
        'w_proj_c': w_proj_c, 'w_o': w_o, 'ln1_g': rows(ln1_g), 'ln1_b': rows(ln1_b),
        'w_ff1': w_ff1, 'w_ff2': w_ff2, 'ln2_g': rows(ln2_g), 'ln2_b': rows(ln2_b),
        'state_conv': state_conv, 'state_pool': state_pool,
    }
    mats = None
    for l in range(DEPTH):
        (hs, nconv_s, xc_new, vn), mats = _sample_mixer(xs, mats, p, l)
        conv_s.append(nconv_s)
        pool_s.append(jnp.concatenate([state_pool[l][:, 1:], xc_new[:, None, :]], axis=1))
        chunk_v_s.append(vn[:, None, :])
        hp, nconv, npool, w1, w2 = _mixer_prompt(xp, bp, seq, TM_MIXER, mats, p, l)
        conv_p.append(nconv)
        pool_p.append(npool)
        xp, xs, mats = _ffn(hp, hs, TM_FFN, w1, w2, p, l, convert_next=l + 1 < DEPTH)
    return (xp.reshape(bp, seq, D), xs.reshape(bs, 1, D), jnp.stack(conv_p), jnp.stack(pool_p),
            jnp.stack(conv_s), jnp.stack(pool_s), jnp.stack(chunk_v_s))
```

```python
import functools

import jax
import jax.numpy as jnp
from jax import lax
from jax.experimental import pallas as pl
from jax.experimental.pallas import tpu as pltpu

D = 1024
N_GROUPS_A = 4
CHUNK = 128
G_A = D // N_GROUPS_A
CONV_W = 3
POOL_WINDOWS = (2, 4, 8, 16)
G_C = D // len(POOL_WINDOWS)
POOL_BUF = max(POOL_WINDOWS) - 1
D_FF = 4 * D
DEPTH = 2
ALPHA = float((2 * DEPTH) ** 0.25)
LN_EPS = 1e-5
PAST_LEN = 16384

COL_U, COL_V, COL_BG, COL_CG, COL_XB, COL_XC, COL_GATE = 0, 1, 2, 3, 4, 5, 6
N_COLS = 9
N_ROW_PIECES = 8
ROW_PIECE = D // N_ROW_PIECES

SUBLANES = 8
LANES = 128
N_LT = D // LANES
LT_G = G_C // LANES
Z_HEAD = SUBLANES
P_HEAD = 32
P_LO = 16
VEC_CONV_W = 6
VEC_ROWS = 16
P_CARRY = 16
assert CONV_W - 1 <= Z_HEAD and POOL_BUF <= P_CARRY <= P_HEAD - P_LO and P_CARRY % SUBLANES == 0
assert PAST_LEN % CHUNK == 0 and PAST_LEN >= max(POOL_WINDOWS)

TM_MIXER = 512
TM_FFN = 1024
SUB_MIXER = 256
SUB_FFN = 256
VMEM_LIMIT_MIXER = 56 * 1024 * 1024
VMEM_LIMIT_FFN = 48 * 1024 * 1024

F32 = jnp.float32
BF16 = jnp.bfloat16


def _dot(a, b):
    return jnp.dot(a, b, preferred_element_type=F32)


def _layer_norm(x, g, b):
    mu = jnp.mean(x, axis=-1, keepdims=True)
    xc = x - mu
    var = jnp.mean(xc * xc, axis=-1, keepdims=True)
    return xc * lax.rsqrt(var + LN_EPS) * g + b


def _mixer_prompt_kernel(x_ref, win_ref, vec_ref, wsp_ref, bsp_ref, wpa_ref,
                         wpb_ref, wpool_ref, wpc_ref, wo_ref, wf1_ref, wf2_ref,
                         h_ref, nconv_ref, npool_ref, wf1o_ref, wf2o_ref,
                         xb_s, vn_s, u_s, ha_s, hb_s, hc_s, acc_s, z_s, p_s, la_s, lb_s, lc_s, *, tm, sub):
    t = pl.program_id(1)

    @pl.when(t == 0)
    def _():
        z_s[:, 0:Z_HEAD, :] = jnp.zeros((N_LT, Z_HEAD, LANES), F32)
        p_s[:, 0:P_HEAD, :] = jnp.zeros((N_LT, P_HEAD, LANES), F32)
        la_s[:, 0:P_LO, :] = jnp.zeros((N_LT, P_LO, LANES), F32)
        lb_s[:, 0:P_LO, :] = jnp.zeros((N_LT - LT_G, P_LO, LANES), F32)
        lc_s[:, 0:P_LO, :] = jnp.zeros((N_LT - 2 * LT_G, P_LO, LANES), F32)

    row = lax.broadcasted_iota(jnp.int32, (CHUNK, CHUNK), 0)
    col = lax.broadcasted_iota(jnp.int32, (CHUNK, CHUNK), 1)
    causal = col <= row
    w_spatial = [jnp.where(causal, wsp_ref[g], 0.0).astype(BF16) for g in range(N_GROUPS_A)]
    lnv_g, lnv_b, conv_b, psc, ln1_g, ln1_b = (vec_ref[r:r + 1, :] for r in range(VEC_CONV_W))
    cw = vec_ref[VEC_CONV_W:VEC_CONV_W + CONV_W, :]

    for r0 in range(0, tm, sub):
        rs = slice(r0, r0 + sub)
        xb_s[rs, :] = x_ref[rs, :].astype(BF16)

        def proj(k):
            return _dot(xb_s[rs, :], win_ref[:, k * D:(k + 1) * D])

        xc = proj(COL_XC)
        p0, p1 = P_HEAD + r0, P_HEAD + r0 + sub
        for j in range(N_LT):
            p_s[j, p0:p1, :] = xc[:, j * LANES:(j + 1) * LANES]
        lo = P_LO if r0 == 0 else p0
        for j in range(N_LT):
            la_s[j, lo:p1, :] = p_s[j, lo:p1, :] + p_s[j, lo - 1:p1 - 1, :]
        for j in range(N_LT - LT_G):
            lb_s[j, lo:p1, :] = la_s[j + LT_G, lo:p1, :] + la_s[j + LT_G, lo - 2:p1 - 2, :]
        for j in range(N_LT - 2 * LT_G):
            lc_s[j, lo:p1, :] = lb_s[j + LT_G, lo:p1, :] + lb_s[j + LT_G, lo - 4:p1 - 4, :]
        sum16 = [lc_s[j + LT_G, p0:p1, :] + lc_s[j + LT_G, p0 - 8:p1 - 8, :] for j in range(LT_G)]
        sums = tuple(jnp.concatenate(tiles, axis=-1) for tiles in (
            [la_s[j, p0:p1, :] for j in range(LT_G)], [lb_s[j, p0:p1, :] for j in range(LT_G)],
            [lc_s[j, p0:p1, :] for j in range(LT_G)], sum16))
        pos1 = t * tm + r0 + lax.broadcasted_iota(jnp.int32, (sub, 1), 0) + 1
        for g, w in enumerate(POOL_WINDOWS):
            cs = slice(g * G_C, (g + 1) * G_C)
            inv_cnt = 1.0 / jnp.minimum(pos1, w).astype(F32)
            d = sums[g] * inv_cnt - xc[:, cs]
            hc_s[rs, cs] = (_dot(d.astype(BF16), wpool_ref[g]) * psc[:, cs]).astype(BF16)
        acc_s[rs, :] = jax.nn.sigmoid(proj(COL_GATE + 2)) * _dot(hc_s[rs, :], wpc_ref[...])

        vn_s[rs, :] = _layer_norm(proj(COL_V), lnv_g, lnv_b).astype(BF16)
        u_s[rs, :] = proj(COL_U)
        for g in range(N_GROUPS_A):
            bias = bsp_ref[:, g:g + 1]
            cs = slice(g * G_A, (g + 1) * G_A)
            for c0 in range(r0, r0 + sub, CHUNK):
                ch = slice(c0, c0 + CHUNK)
                s = _dot(w_spatial[g], vn_s[ch, cs]) + bias
                ha_s[ch, cs] = (u_s[ch, cs] * s).astype(BF16)
        acc_s[rs, :] += jax.nn.sigmoid(proj(COL_GATE + 0)) * _dot(ha_s[rs, :], wpa_ref[...])

        z = proj(COL_CG) * proj(COL_XB)
        z0 = Z_HEAD + r0
        for j in range(N_LT):
            z_s[j, z0:z0 + sub, :] = z[:, j * LANES:(j + 1) * LANES]
        z_m2 = jnp.concatenate([z_s[j, z0 - 2:z0 - 2 + sub, :] for j in range(N_LT)], axis=-1)
        z_m1 = jnp.concatenate([z_s[j, z0 - 1:z0 - 1 + sub, :] for j in range(N_LT)], axis=-1)
        y = conv_b + (cw[0:1] * z_m2 + cw[1:2] * z_m1 + cw[2:3] * z)
        hb_s[rs, :] = (proj(COL_BG) * y).astype(BF16)
        acc_s[rs, :] += jax.nn.sigmoid(proj(COL_GATE + 1)) * _dot(hb_s[rs, :], wpb_ref[...])

        o = _dot(acc_s[rs, :].astype(BF16), wo_ref[...])
        h_ref[rs, :] = _layer_norm(ALPHA * x_ref[rs, :] + o, ln1_g, ln1_b)

    tail = z_s[:, tm:tm + Z_HEAD, :]
    z_s[:, 0:Z_HEAD, :] = tail
    hist = p_s[:, tm + P_HEAD - P_CARRY:tm + P_HEAD, :]
    p_s[:, P_HEAD - P_CARRY:P_HEAD, :] = hist
    for j in range(N_LT):
        nconv_ref[:, j * LANES:(j + 1) * LANES] = z_s[j, tm + Z_HEAD - (CONV_W - 1):tm + Z_HEAD, :]
        npool_ref[:, j * LANES:(j + 1) * LANES] = p_s[j, tm + P_HEAD - POOL_BUF:tm + P_HEAD, :]

    wf1o_ref[...] = wf1_ref[...].astype(BF16)
    wf2o_ref[...] = wf2_ref[...].astype(BF16)


def _sample_mixer_kernel(x_ref, sconv_ref, sp0_ref, sp1_ref, sp2_ref, sp3_ref, win_ref, lnvg_ref, lnvb_ref, wsd_ref,
                         bsd_ref, wpa_ref, cw_ref, cb_ref, wpb_ref, wpool_ref, psc_ref, wpc_ref, wo_ref, l1g_ref,
                         l1b_ref, h_ref, z_ref, xc_ref, vn_ref, *rest, emit_bf16):
    if emit_bf16:
        wino_ref, wpao_ref, wpbo_ref, wpoolo_ref, wpco_ref, woo_ref = rest[:len(MIXER_MATS)]
    else:
        wino_ref = wpao_ref = wpbo_ref = wpoolo_ref = wpco_ref = woo_ref = None
    xb_s, proj_s, wpa_s, wpb_s, wpc_s, wo_s = rest[len(MIXER_MATS) if emit_bf16 else 0:]
    k = pl.program_id(0)

    @pl.when(k == 0)
    def _():
        xb_s[...] = x_ref[...].astype(BF16)

    @pl.when(k < N_COLS)
    def _():
        w = win_ref[...].astype(BF16)
        if emit_bf16:
            wino_ref[...] = w
        proj_s[k] = _dot(xb_s[...], w)

    @pl.when(k < N_ROW_PIECES)
    def _():
        rows = pl.ds(pl.multiple_of(k * ROW_PIECE, ROW_PIECE), ROW_PIECE)
        for src, dst, keep in ((wpa_ref, wpao_ref, wpa_s), (wpb_ref, wpbo_ref, wpb_s), (wpc_ref, wpco_ref, wpc_s),
                               (wo_ref, woo_ref, wo_s)):
            piece = src[...].astype(BF16)
            if emit_bf16:
                dst[...] = piece
            keep[rows, :] = piece

    @pl.when(k == N_COLS)
    def _():
        x = x_ref[...]
        wpa, wpb, wpc, wo = wpa_s[...], wpb_s[...], wpc_s[...], wo_s[...]
        wpool = wpool_ref[...].astype(BF16)
        if emit_bf16:
            wpoolo_ref[...] = wpool

        vn = _layer_norm(proj_s[COL_V], lnvg_ref[...], lnvb_ref[...])
        vn_ref[...] = vn
        s = wsd_ref[...] * vn + bsd_ref[...]
        ha = (proj_s[COL_U] * s).astype(BF16)
        acc = jax.nn.sigmoid(proj_s[COL_GATE + 0]) * _dot(ha, wpa)

        z = proj_s[COL_CG] * proj_s[COL_XB]
        z_ref[...] = z
        cw = cw_ref[...]
        y = cb_ref[...] + (cw[0:1] * sconv_ref[0] + cw[1:2] * sconv_ref[1] + cw[2:3] * z)
        hb = (proj_s[COL_BG] * y).astype(BF16)
        acc = acc + jax.nn.sigmoid(proj_s[COL_GATE + 1]) * _dot(hb, wpb)

        xc = proj_s[COL_XC]
        xc_ref[...] = xc
        psc = psc_ref[...]
        hc = []
        for g, (w, sp_ref) in enumerate(zip(POOL_WINDOWS, (sp0_ref, sp1_ref, sp2_ref, sp3_ref))):
            cs = slice(g * G_C, (g + 1) * G_C)
            tot = xc[:, cs]
            for j in range(w - 1):
                tot = tot + sp_ref[w - 2 - j]
            d = tot * (1.0 / w) - xc[:, cs]
            hc.append((_dot(d.astype(BF16), wpool[g]) * psc[:, cs]).astype(BF16))
        hc = jnp.concatenate(hc, axis=-1)
        acc = acc + jax.nn.sigmoid(proj_s[COL_GATE + 2]) * _dot(hc, wpc)

        o = _dot(acc.astype(BF16), wo)
        h_ref[...] = _layer_norm(ALPHA * x + o, l1g_ref[...], l1b_ref[...])


def _ffn_rows(h, hb, w1_ref, w2_ref, gb_ref):
    acc = None
    for j in range(D_FF // D):
        f = _dot(hb, w1_ref[:, j * D:(j + 1) * D])
        a = jnp.square(jnp.maximum(f, 0.0)).astype(BF16)
        c = _dot(a, w2_ref[j * D:(j + 1) * D, :])
        acc = c if acc is None else acc + c
    return _layer_norm(ALPHA * h + acc, gb_ref[0:1, :], gb_ref[1:2, :])


def _ffn_kernel(*refs, sub, n_cvt):
    h_ref, hs_ref, w1_ref, w2_ref, gb_ref = refs[:5]
    cvt_in = refs[5:5 + n_cvt]
    o_ref, os_ref = refs[5 + n_cvt:7 + n_cvt]
    cvt_out = refs[7 + n_cvt:7 + 2 * n_cvt]
    (hb_s,) = refs[7 + 2 * n_cvt:]

    for r0 in range(0, h_ref.shape[0], sub):
        rs = slice(r0, r0 + sub)
        hb_s[rs, :] = h_ref[rs, :].astype(BF16)
        o_ref[rs, :] = _ffn_rows(h_ref[rs, :], hb_s[rs, :], w1_ref, w2_ref, gb_ref)

    @pl.when(pl.program_id(0) == pl.num_programs(0) - 1)
    def _():
        hs = hs_ref[...]
        os_ref[...] = _ffn_rows(hs, hs.astype(BF16), w1_ref, w2_ref, gb_ref)

    for src, dst in zip(cvt_in, cvt_out):
        dst[...] = src[...].astype(BF16)


def _resident(shape):
    nd = len(shape)
    return pl.BlockSpec(shape, lambda *_: (0,) * nd, pipeline_mode=pl.Buffered(1))


def _layer_resident(a, l):
    return pl.BlockSpec((None,) + a.shape[1:], lambda *_: (l,) + (0,) * (a.ndim - 1), pipeline_mode=pl.Buffered(1))


MIXER_MATS = ('w_in', 'w_proj_a', 'w_proj_b', 'w_pool', 'w_proj_c', 'w_o')


def _mixer_prompt(x2d, batch, seq, tm, mats, p, l):
    assert x2d.shape == (batch * seq, D) and x2d.dtype == F32 and seq % tm == 0 and tm % SUB_MIXER == 0
    assert SUB_MIXER % CHUNK == 0 and SUB_MIXER >= P_CARRY
    nt = seq // tm
    nsteps = batch * nt
    assert D % nsteps == 0 and (D // nsteps) % (2 * SUBLANES) == 0
    row_block = pl.BlockSpec((tm, D), lambda b, t: (b * nt + t, 0))
    r1, r2 = D // nsteps, D_FF // nsteps
    lay = lambda k: (p[k], _layer_resident(p[k], l))
    mat = lambda k: (mats[k], _resident(mats[k].shape))
    operands = [(x2d, row_block), mat('w_in'), lay('mixer_vec'), lay('w_spatial'), lay('b_spatial_t'),
                mat('w_proj_a'), mat('w_proj_b'), mat('w_pool'), mat('w_proj_c'), mat('w_o'),
                (p['w_ff1'], pl.BlockSpec((None, r1, D_FF), lambda b, t: (l, b * nt + t, 0))),
                (p['w_ff2'], pl.BlockSpec((None, r2, D), lambda b, t: (l, b * nt + t, 0)))]
    return pl.pallas_call(
        functools.partial(_mixer_prompt_kernel, tm=tm, sub=min(tm, SUB_MIXER)),
        grid=(batch, nt),
        in_specs=[spec for _, spec in operands],
        out_specs=[row_block,
                   pl.BlockSpec((None, CONV_W - 1, D), lambda b, t: (b, 0, 0)),
                   pl.BlockSpec((None, POOL_BUF, D), lambda b, t: (b, 0, 0)),
                   pl.BlockSpec((r1, D_FF), lambda b, t: (b * nt + t, 0)),
                   pl.BlockSpec((r2, D), lambda b, t: (b * nt + t, 0))],
        out_shape=[jax.ShapeDtypeStruct((batch * seq, D), F32),
                   jax.ShapeDtypeStruct((batch, CONV_W - 1, D), F32),
                   jax.ShapeDtypeStruct((batch, POOL_BUF, D), F32),
                   jax.ShapeDtypeStruct((D, D_FF), BF16),
                   jax.ShapeDtypeStruct((D_FF, D), BF16)],
        scratch_shapes=[pltpu.VMEM((tm, D), BF16),
                        pltpu.VMEM((tm, D), BF16),
                        pltpu.VMEM((tm, D), F32),
                        pltpu.VMEM((tm, D), BF16),
                        pltpu.VMEM((tm, D), BF16),
                        pltpu.VMEM((tm, D), BF16),
                        pltpu.VMEM((tm, D), F32),
                        pltpu.VMEM((N_LT, Z_HEAD + tm, LANES), F32),
                        pltpu.VMEM((N_LT, P_HEAD + tm, LANES), F32),
                        pltpu.VMEM((N_LT, P_HEAD + tm, LANES), F32),
                        pltpu.VMEM((N_LT - LT_G, P_HEAD + tm, LANES), F32),
                        pltpu.VMEM((N_LT - 2 * LT_G, P_HEAD + tm, LANES), F32)],
        compiler_params=pltpu.CompilerParams(dimension_semantics=("arbitrary", "arbitrary"),
                                             vmem_limit_bytes=VMEM_LIMIT_MIXER),
        name="mixer_prompt",
    )(*[a for a, _ in operands])


def _sample_mixer(x2d, mats, p, l):
    n = x2d.shape[0]
    emit_bf16 = mats is None
    lay = lambda k: (p[k], _layer_resident(p[k], l))
    col_block = lambda k: (0, jnp.minimum(k, N_COLS - 1))
    row_piece = lambda k: (jnp.minimum(k, N_ROW_PIECES - 1), 0)
    if emit_bf16:
        w_in = (p['w_in'], pl.BlockSpec((None, D, D), lambda k: (l,) + col_block(k)))
        sq = lambda name: (p[name], pl.BlockSpec((None, ROW_PIECE, D), lambda k: (l,) + row_piece(k)))
        w_pool = lay('w_pool')
    else:
        w_in = (mats['w_in'], pl.BlockSpec((D, D), col_block))
        sq = lambda name: (mats[name], pl.BlockSpec((ROW_PIECE, D), row_piece))
        w_pool = (mats['w_pool'], _resident(mats['w_pool'].shape))
    history = [jnp.swapaxes(p['state_conv'][l], 0, 1)]
    history += [jnp.swapaxes(p['state_pool'][l, :, POOL_BUF - (w - 1):, g * G_C:(g + 1) * G_C], 0, 1)
                for g, w in enumerate(POOL_WINDOWS)]
    operands = [(x2d, _resident(x2d.shape))] + [(a, _resident(a.shape)) for a in history] + [
                w_in,
                lay('lnv_g'), lay('lnv_b'), lay('w_spatial_d'), lay('b_spatial_d'), sq('w_proj_a'), lay('conv_w'),
                lay('conv_b'), sq('w_proj_b'), w_pool, lay('pool_scale'), sq('w_proj_c'), sq('w_o'),
                lay('ln1_g'), lay('ln1_b')]
    once = lambda shape: pl.BlockSpec(shape, lambda k: (0,) * len(shape), pipeline_mode=pl.Buffered(1))
    out_specs = [once((n, D))] * 4
    out_shape = [jax.ShapeDtypeStruct((n, D), F32)] * 4
    if emit_bf16:
        sq_out = pl.BlockSpec((ROW_PIECE, D), row_piece)
        out_specs += [pl.BlockSpec((D, D), col_block), sq_out, sq_out, once(p['w_pool'].shape[1:]), sq_out, sq_out]
        out_shape += [jax.ShapeDtypeStruct(p[k].shape[1:], BF16) for k in MIXER_MATS]
    outs = pl.pallas_call(
        functools.partial(_sample_mixer_kernel, emit_bf16=emit_bf16),
        grid=(N_COLS + 1,),
        in_specs=[spec for _, spec in operands],
        out_specs=out_specs,
        out_shape=out_shape,
        scratch_shapes=[pltpu.VMEM((n, D), BF16),
                        pltpu.VMEM((N_COLS, n, D), F32)]
                       + [pltpu.VMEM((D, D), BF16)] * 4,
        compiler_params=pltpu.CompilerParams(
            dimension_semantics=("arbitrary",), vmem_limit_bytes=VMEM_LIMIT_MIXER,
            allow_input_fusion=[1 <= i <= len(history) for i in range(len(operands))]),
        name="sample_mixer",
    )(*[a for a, _ in operands])
    return outs[:4], (dict(zip(MIXER_MATS, outs[4:])) if emit_bf16 else mats)


def _ffn(h2d, hs2d, tm, w1, w2, p, l, convert_next):
    n = h2d.shape[0]
    assert h2d.shape == (n, D) and h2d.dtype == F32 and n % tm == 0 and tm % SUB_FFN == 0
    nsteps = n // tm
    assert not convert_next or all((p[k].shape[-2] // nsteps) % (2 * SUBLANES) == 0 for k in MIXER_MATS)
    row_block = pl.BlockSpec((tm, D), lambda i: (i, 0))
    lay = lambda k: (p[k], _layer_resident(p[k], l))
    operands = [(h2d, row_block), (hs2d, _resident(hs2d.shape)), (w1, _resident(w1.shape)), (w2, _resident(w2.shape)),
                lay('ffn_vec')]
    out_specs = [row_block, pl.BlockSpec(hs2d.shape, lambda i: (0, 0))]
    out_shape = [jax.ShapeDtypeStruct((n, D), F32), jax.ShapeDtypeStruct(hs2d.shape, F32)]
    n_cvt = len(MIXER_MATS) if convert_next else 0
    if convert_next:
        for k in MIXER_MATS:
            a = p[k]
            r = a.shape[-2] // nsteps
            lead = a.ndim - 3
            blk = a.shape[1:-2] + (r, a.shape[-1])
            operands.append((a, pl.BlockSpec((None,) + blk, lambda i, lead=lead: (l + 1,) + (0,) * lead + (i, 0))))
            out_specs.append(pl.BlockSpec(blk, lambda i, lead=lead: (0,) * lead + (i, 0)))
            out_shape.append(jax.ShapeDtypeStruct(a.shape[1:], BF16))
    outs = pl.pallas_call(
        functools.partial(_ffn_kernel, sub=min(tm, SUB_FFN), n_cvt=n_cvt),
        grid=(nsteps,),
        in_specs=[spec for _, spec in operands],
        out_specs=out_specs,
        out_shape=out_shape,
        scratch_shapes=[pltpu.VMEM((tm, D), BF16)],
        compiler_params=pltpu.CompilerParams(dimension_semantics=("arbitrary",),
                                             vmem_limit_bytes=VMEM_LIMIT_FFN),
        name="ffn",
    )(*[a for a, _ in operands])
    return outs[0], outs[1], (dict(zip(MIXER_MATS, outs[2:])) if convert_next else None)


def kernel(x_prompt, x_sample, state_conv, state_pool, w_in, lnv_g, lnv_b, w_spatial, b_spatial, w_proj_a, conv_w, conv_b, w_proj_b, w_pool, pool_scale, w_proj_c, w_o, ln1_g, ln1_b, w_ff1, w_ff2, ln2_g, ln2_b):
    bp, seq, _ = x_prompt.shape
    bs = x_sample.shape[0]
    assert x_prompt.shape == (bp, seq, D) and x_sample.shape == (bs, 1, D) and w_in.shape == (DEPTH, D, N_COLS * D)
    assert state_conv.shape == (DEPTH, bs, CONV_W - 1, D) and state_pool.shape == (DEPTH, bs, POOL_BUF, D)
    xp = x_prompt.reshape(bp * seq, D)
    xs = x_sample.reshape(bs, D)
    conv_p, pool_p, conv_s, pool_s, chunk_v_s = [], [], [], [], []
    rows = lambda a: a[:, None, :]
    p = {
        'w_in': w_in, 'lnv_g': rows(lnv_g), 'lnv_b': rows(lnv_b),
        'w_spatial': w_spatial, 'b_spatial_t': jnp.swapaxes(b_spatial, 1, 2),
        'w_spatial_d': rows(jnp.repeat(w_spatial[:, :, 0, 0], G_A, axis=1)),
        'b_spatial_d': rows(jnp.repeat(b_spatial[:, :, 0], G_A, axis=1)),
        'w_proj_a': w_proj_a, 'conv_w': conv_w, 'conv_b': rows(conv_b),
        'w_proj_b': w_proj_b, 'w_pool': w_pool, 'pool_scale': rows(pool_scale),
        'w_proj_c': w_proj_c, 'w_o': w_o, 'ln1_g': rows(ln1_g), 'ln1_b': rows(ln1_b),
        'w_ff1': w_ff1, 'w_ff2': w_ff2, 'ln2_g': rows(ln2_g), 'ln2_b': rows(ln2_b),
        'state_conv': state_conv, 'state_pool': state_pool,
    }
    pad_rows = lambda a: jnp.pad(a, ((0, 0), (0, VEC_ROWS - a.shape[1]), (0, 0)))
    p['mixer_vec'] = pad_rows(jnp.stack([lnv_g, lnv_b, conv_b, pool_scale, ln1_g, ln1_b] + [conv_w[:, k] for k in range(CONV_W)], axis=1))
    p['ffn_vec'] = pad_rows(jnp.stack([ln2_g, ln2_b], axis=1))
    mats = None
    for l in range(DEPTH):
        (hs, z_new, xc_new, vn), mats = _sample_mixer(xs, mats, p, l)
        conv_s.append(jnp.concatenate([state_conv[l][:, 1:], z_new[:, None, :]], axis=1))
        pool_s.append(jnp.concatenate([state_pool[l][:, 1:], xc_new[:, None, :]], axis=1))
        chunk_v_s.append(vn[:, None, :])
        hp, nconv, npool, w1, w2 = _mixer_prompt(xp, bp, seq, TM_MIXER, mats, p, l)
        conv_p.append(nconv)
        pool_p.append(npool)
        xp, xs, mats = _ffn(hp, hs, TM_FFN, w1, w2, p, l, convert_next=l + 1 < DEPTH)
    return (xp.reshape(bp, seq, D), xs.reshape(bs, 1, D), jnp.stack(conv_p), jnp.stack(pool_p),
            jnp.stack(conv_s), jnp.stack(pool_s), jnp.stack(chunk_v_s))
```

```python
import functools

import jax
import jax.numpy as jnp
from jax import lax
from jax.experimental import pallas as pl
from jax.experimental.pallas import tpu as pltpu

D = 1024
N_GROUPS_A = 4
CHUNK = 128
G_A = D // N_GROUPS_A
CONV_W = 3
POOL_WINDOWS = (2, 4, 8, 16)
G_C = D // len(POOL_WINDOWS)
POOL_BUF = max(POOL_WINDOWS) - 1
D_FF = 4 * D
DEPTH = 2
ALPHA = float((2 * DEPTH) ** 0.25)
LN_EPS = 1e-5
PAST_LEN = 16384

COL_U, COL_V, COL_BG, COL_CG, COL_XB, COL_XC, COL_GATE = 0, 1, 2, 3, 4, 5, 6
N_COLS = 9
N_ROW_PIECES = 8
ROW_PIECE = D // N_ROW_PIECES

SUBLANES = 8
LANES = 128
N_LT = D // LANES
LT_G = G_C // LANES
Z_HEAD = SUBLANES
P_HEAD = 32
P_LO = 16
VEC_CONV_W = 6
VEC_ROWS = 16
P_CARRY = 16
assert CONV_W - 1 <= Z_HEAD and POOL_BUF <= P_CARRY <= P_HEAD - P_LO and P_CARRY % SUBLANES == 0
assert PAST_LEN % CHUNK == 0 and PAST_LEN >= max(POOL_WINDOWS)

TM_MIXER = 512
TM_FFN = 1024
SUB_MIXER = 256
SUB_FFN = 256
VMEM_LIMIT_MIXER = 56 * 1024 * 1024
VMEM_LIMIT_FFN = 48 * 1024 * 1024

F32 = jnp.float32
BF16 = jnp.bfloat16


def _dot(a, b):
    return jnp.dot(a, b, preferred_element_type=F32)


def _layer_norm(x, g, b):
    mu = jnp.mean(x, axis=-1, keepdims=True)
    xc = x - mu
    var = jnp.mean(xc * xc, axis=-1, keepdims=True)
    return xc * lax.rsqrt(var + LN_EPS) * g + b


def _mixer_prompt_kernel(x_ref, win_ref, vec_ref, wsp_ref, bsp_ref, wpa_ref,
                         wpb_ref, wpool_ref, wpc_ref, wo_ref, wf1_ref, wf2_ref,
                         h_ref, nconv_ref, npool_ref, wf1o_ref, wf2o_ref,
                         xb_s, vn_s, u_s, ha_s, hb_s, hc_s, acc_s, z_s, p_s, la_s, lb_s, lc_s, *, tm, sub):
    t = pl.program_id(1)

    @pl.when(t == 0)
    def _():
        z_s[:, 0:Z_HEAD, :] = jnp.zeros((N_LT, Z_HEAD, LANES), F32)
        p_s[:, 0:P_HEAD, :] = jnp.zeros((N_LT, P_HEAD, LANES), F32)
        la_s[:, 0:P_LO, :] = jnp.zeros((N_LT, P_LO, LANES), F32)
        lb_s[:, 0:P_LO, :] = jnp.zeros((N_LT - LT_G, P_LO, LANES), F32)
        lc_s[:, 0:P_LO, :] = jnp.zeros((N_LT - 2 * LT_G, P_LO, LANES), F32)

    row = lax.broadcasted_iota(jnp.int32, (CHUNK, CHUNK), 0)
    col = lax.broadcasted_iota(jnp.int32, (CHUNK, CHUNK), 1)
    causal = col <= row
    vec = lambda r, cols=slice(None): vec_ref[r:r + 1, cols]
    ROW_LNV_G, ROW_LNV_B, ROW_CONV_B, ROW_PSC, ROW_LN1_G, ROW_LN1_B = range(VEC_CONV_W)

    for r0 in range(0, tm, sub):
        rs = slice(r0, r0 + sub)
        xb_s[rs, :] = x_ref[rs, :].astype(BF16)

        def proj(k):
            return _dot(xb_s[rs, :], win_ref[:, k * D:(k + 1) * D])

        xc = proj(COL_XC)
        p0, p1 = P_HEAD + r0, P_HEAD + r0 + sub
        for j in range(N_LT):
            p_s[j, p0:p1, :] = xc[:, j * LANES:(j + 1) * LANES]
        lo = P_LO if r0 == 0 else p0
        for j in range(N_LT):
            la_s[j, lo:p1, :] = p_s[j, lo:p1, :] + p_s[j, lo - 1:p1 - 1, :]
        for j in range(N_LT - LT_G):
            lb_s[j, lo:p1, :] = la_s[j + LT_G, lo:p1, :] + la_s[j + LT_G, lo - 2:p1 - 2, :]
        for j in range(N_LT - 2 * LT_G):
            lc_s[j, lo:p1, :] = lb_s[j + LT_G, lo:p1, :] + lb_s[j + LT_G, lo - 4:p1 - 4, :]
        sum16 = [lc_s[j + LT_G, p0:p1, :] + lc_s[j + LT_G, p0 - 8:p1 - 8, :] for j in range(LT_G)]
        sums = tuple(jnp.concatenate(tiles, axis=-1) for tiles in (
            [la_s[j, p0:p1, :] for j in range(LT_G)], [lb_s[j, p0:p1, :] for j in range(LT_G)],
            [lc_s[j, p0:p1, :] for j in range(LT_G)], sum16))
        pos1 = t * tm + r0 + lax.broadcasted_iota(jnp.int32, (sub, 1), 0) + 1
        for g, w in enumerate(POOL_WINDOWS):
            cs = slice(g * G_C, (g + 1) * G_C)
            inv_cnt = 1.0 / jnp.minimum(pos1, w).astype(F32)
            d = sums[g] * inv_cnt - xc[:, cs]
            hc_s[rs, cs] = (_dot(d.astype(BF16), wpool_ref[g]) * vec(ROW_PSC, cs)).astype(BF16)
        acc_s[rs, :] = jax.nn.sigmoid(proj(COL_GATE + 2)) * _dot(hc_s[rs, :], wpc_ref[...])

        vn_s[rs, :] = _layer_norm(proj(COL_V), vec(ROW_LNV_G), vec(ROW_LNV_B)).astype(BF16)
        u_s[rs, :] = proj(COL_U)
        for g in range(N_GROUPS_A):
            w_g = jnp.where(causal, wsp_ref[g], 0.0).astype(BF16)
            bias = bsp_ref[:, g:g + 1]
            cs = slice(g * G_A, (g + 1) * G_A)
            for c0 in range(r0, r0 + sub, CHUNK):
                ch = slice(c0, c0 + CHUNK)
                s = _dot(w_g, vn_s[ch, cs]) + bias
                ha_s[ch, cs] = (u_s[ch, cs] * s).astype(BF16)
        acc_s[rs, :] += jax.nn.sigmoid(proj(COL_GATE + 0)) * _dot(ha_s[rs, :], wpa_ref[...])

        z = proj(COL_CG) * proj(COL_XB)
        z0 = Z_HEAD + r0
        for j in range(N_LT):
            z_s[j, z0:z0 + sub, :] = z[:, j * LANES:(j + 1) * LANES]
        z_m2 = jnp.concatenate([z_s[j, z0 - 2:z0 - 2 + sub, :] for j in range(N_LT)], axis=-1)
        z_m1 = jnp.concatenate([z_s[j, z0 - 1:z0 - 1 + sub, :] for j in range(N_LT)], axis=-1)
        y = vec(ROW_CONV_B) + (vec(VEC_CONV_W) * z_m2 + vec(VEC_CONV_W + 1) * z_m1 + vec(VEC_CONV_W + 2) * z)
        hb_s[rs, :] = (proj(COL_BG) * y).astype(BF16)
        acc_s[rs, :] += jax.nn.sigmoid(proj(COL_GATE + 1)) * _dot(hb_s[rs, :], wpb_ref[...])

        o = _dot(acc_s[rs, :].astype(BF16), wo_ref[...])
        h_ref[rs, :] = _layer_norm(ALPHA * x_ref[rs, :] + o, vec(ROW_LN1_G), vec(ROW_LN1_B))

    tail = z_s[:, tm:tm + Z_HEAD, :]
    z_s[:, 0:Z_HEAD, :] = tail
    hist = p_s[:, tm + P_HEAD - P_CARRY:tm + P_HEAD, :]
    p_s[:, P_HEAD - P_CARRY:P_HEAD, :] = hist
    for j in range(N_LT):
        nconv_ref[:, j * LANES:(j + 1) * LANES] = z_s[j, tm + Z_HEAD - (CONV_W - 1):tm + Z_HEAD, :]
        npool_ref[:, j * LANES:(j + 1) * LANES] = p_s[j, tm + P_HEAD - POOL_BUF:tm + P_HEAD, :]

    wf1o_ref[...] = wf1_ref[...].astype(BF16)
    wf2o_ref[...] = wf2_ref[...].astype(BF16)


def _sample_mixer_kernel(x_ref, sconv_ref, sp0_ref, sp1_ref, sp2_ref, sp3_ref, win_ref, lnvg_ref, lnvb_ref, wsd_ref,
                         bsd_ref, wpa_ref, cw_ref, cb_ref, wpb_ref, wpool_ref, psc_ref, wpc_ref, wo_ref, l1g_ref,
                         l1b_ref, h_ref, z_ref, xc_ref, vn_ref, *rest, emit_bf16):
    if emit_bf16:
        wino_ref, wpao_ref, wpbo_ref, wpoolo_ref, wpco_ref, woo_ref = rest[:len(MIXER_MATS)]
    else:
        wino_ref = wpao_ref = wpbo_ref = wpoolo_ref = wpco_ref = woo_ref = None
    xb_s, proj_s, wpa_s, wpb_s, wpc_s, wo_s = rest[len(MIXER_MATS) if emit_bf16 else 0:]
    k = pl.program_id(0)

    @pl.when(k == 0)
    def _():
        xb_s[...] = x_ref[...].astype(BF16)

    @pl.when(k < N_COLS)
    def _():
        w = win_ref[...].astype(BF16)
        if emit_bf16:
            wino_ref[...] = w
        proj_s[k] = _dot(xb_s[...], w)

    @pl.when(k < N_ROW_PIECES)
    def _():
        rows = pl.ds(pl.multiple_of(k * ROW_PIECE, ROW_PIECE), ROW_PIECE)
        for src, dst, keep in ((wpa_ref, wpao_ref, wpa_s), (wpb_ref, wpbo_ref, wpb_s), (wpc_ref, wpco_ref, wpc_s),
                               (wo_ref, woo_ref, wo_s)):
            piece = src[...].astype(BF16)
            if emit_bf16:
                dst[...] = piece
            keep[rows, :] = piece

    @pl.when(k == N_COLS)
    def _():
        x = x_ref[...]
        wpa, wpb, wpc, wo = wpa_s[...], wpb_s[...], wpc_s[...], wo_s[...]
        wpool = wpool_ref[...].astype(BF16)
        if emit_bf16:
            wpoolo_ref[...] = wpool

        vn = _layer_norm(proj_s[COL_V], lnvg_ref[...], lnvb_ref[...])
        vn_ref[...] = vn
        s = wsd_ref[...] * vn + bsd_ref[...]
        ha = (proj_s[COL_U] * s).astype(BF16)
        acc = jax.nn.sigmoid(proj_s[COL_GATE + 0]) * _dot(ha, wpa)

        z = proj_s[COL_CG] * proj_s[COL_XB]
        z_ref[...] = z
        cw = cw_ref[...]
        y = cb_ref[...] + (cw[0:1] * sconv_ref[0] + cw[1:2] * sconv_ref[1] + cw[2:3] * z)
        hb = (proj_s[COL_BG] * y).astype(BF16)
        acc = acc + jax.nn.sigmoid(proj_s[COL_GATE + 1]) * _dot(hb, wpb)

        xc = proj_s[COL_XC]
        xc_ref[...] = xc
        psc = psc_ref[...]
        hc = []
        for g, (w, sp_ref) in enumerate(zip(POOL_WINDOWS, (sp0_ref, sp1_ref, sp2_ref, sp3_ref))):
            cs = slice(g * G_C, (g + 1) * G_C)
            tot = xc[:, cs]
            for j in range(w - 1):
                tot = tot + sp_ref[w - 2 - j]
            d = tot * (1.0 / w) - xc[:, cs]
            hc.append((_dot(d.astype(BF16), wpool[g]) * psc[:, cs]).astype(BF16))
        hc = jnp.concatenate(hc, axis=-1)
        acc = acc + jax.nn.sigmoid(proj_s[COL_GATE + 2]) * _dot(hc, wpc)

        o = _dot(acc.astype(BF16), wo)
        h_ref[...] = _layer_norm(ALPHA * x + o, l1g_ref[...], l1b_ref[...])


def _ffn_rows(h, hb, w1_ref, w2_ref, gb_ref):
    acc = None
    for j in range(D_FF // D):
        f = _dot(hb, w1_ref[:, j * D:(j + 1) * D])
        a = jnp.square(jnp.maximum(f, 0.0)).astype(BF16)
        c = _dot(a, w2_ref[j * D:(j + 1) * D, :])
        acc = c if acc is None else acc + c
    return _layer_norm(ALPHA * h + acc, gb_ref[0:1, :], gb_ref[1:2, :])


def _ffn_kernel(*refs, sub, n_cvt):
    h_ref, hs_ref, w1_ref, w2_ref, gb_ref = refs[:5]
    cvt_in = refs[5:5 + n_cvt]
    o_ref, os_ref = refs[5 + n_cvt:7 + n_cvt]
    cvt_out = refs[7 + n_cvt:7 + 2 * n_cvt]
    (hb_s,) = refs[7 + 2 * n_cvt:]

    for r0 in range(0, h_ref.shape[0], sub):
        rs = slice(r0, r0 + sub)
        hb_s[rs, :] = h_ref[rs, :].astype(BF16)
        o_ref[rs, :] = _ffn_rows(h_ref[rs, :], hb_s[rs, :], w1_ref, w2_ref, gb_ref)

    @pl.when(pl.program_id(0) == pl.num_programs(0) - 1)
    def _():
        hs = hs_ref[...]
        os_ref[...] = _ffn_rows(hs, hs.astype(BF16), w1_ref, w2_ref, gb_ref)

    for src, dst in zip(cvt_in, cvt_out):
        dst[...] = src[...].astype(BF16)


def _resident(shape):
    nd = len(shape)
    return pl.BlockSpec(shape, lambda *_: (0,) * nd, pipeline_mode=pl.Buffered(1))


def _layer_resident(a, l):
    return pl.BlockSpec((None,) + a.shape[1:], lambda *_: (l,) + (0,) * (a.ndim - 1), pipeline_mode=pl.Buffered(1))


MIXER_MATS = ('w_in', 'w_proj_a', 'w_proj_b', 'w_pool', 'w_proj_c', 'w_o')


def _mixer_prompt(x2d, batch, seq, tm, mats, p, l):
    assert x2d.shape == (batch * seq, D) and x2d.dtype == F32 and seq % tm == 0 and tm % SUB_MIXER == 0
    assert SUB_MIXER % CHUNK == 0 and SUB_MIXER >= P_CARRY
    nt = seq // tm
    nsteps = batch * nt
    assert D % nsteps == 0 and (D // nsteps) % (2 * SUBLANES) == 0
    row_block = pl.BlockSpec((tm, D), lambda b, t: (b * nt + t, 0))
    r1, r2 = D // nsteps, D_FF // nsteps
    lay = lambda k: (p[k], _layer_resident(p[k], l))
    mat = lambda k: (mats[k], _resident(mats[k].shape))
    operands = [(x2d, row_block), mat('w_in'), lay('mixer_vec'), lay('w_spatial'), lay('b_spatial_t'),
                mat('w_proj_a'), mat('w_proj_b'), mat('w_pool'), mat('w_proj_c'), mat('w_o'),
                (p['w_ff1'], pl.BlockSpec((None, r1, D_FF), lambda b, t: (l, b * nt + t, 0))),
                (p['w_ff2'], pl.BlockSpec((None, r2, D), lambda b, t: (l, b * nt + t, 0)))]
    return pl.pallas_call(
        functools.partial(_mixer_prompt_kernel, tm=tm, sub=min(tm, SUB_MIXER)),
        grid=(batch, nt),
        in_specs=[spec for _, spec in operands],
        out_specs=[row_block,
                   pl.BlockSpec((None, CONV_W - 1, D), lambda b, t: (b, 0, 0)),
                   pl.BlockSpec((None, POOL_BUF, D), lambda b, t: (b, 0, 0)),
                   pl.BlockSpec((r1, D_FF), lambda b, t: (b * nt + t, 0)),
                   pl.BlockSpec((r2, D), lambda b, t: (b * nt + t, 0))],
        out_shape=[jax.ShapeDtypeStruct((batch * seq, D), F32),
                   jax.ShapeDtypeStruct((batch, CONV_W - 1, D), F32),
                   jax.ShapeDtypeStruct((batch, POOL_BUF, D), F32),
                   jax.ShapeDtypeStruct((D, D_FF), BF16),
                   jax.ShapeDtypeStruct((D_FF, D), BF16)],
        scratch_shapes=[pltpu.VMEM((tm, D), BF16),
                        pltpu.VMEM((tm, D), BF16),
                        pltpu.VMEM((tm, D), F32),
                        pltpu.VMEM((tm, D), BF16),
                        pltpu.VMEM((tm, D), BF16),
                        pltpu.VMEM((tm, D), BF16),
                        pltpu.VMEM((tm, D), F32),
                        pltpu.VMEM((N_LT, Z_HEAD + tm, LANES), F32),
                        pltpu.VMEM((N_LT, P_HEAD + tm, LANES), F32),
                        pltpu.VMEM((N_LT, P_HEAD + tm, LANES), F32),
                        pltpu.VMEM((N_LT - LT_G, P_HEAD + tm, LANES), F32),
                        pltpu.VMEM((N_LT - 2 * LT_G, P_HEAD + tm, LANES), F32)],
        compiler_params=pltpu.CompilerParams(dimension_semantics=("arbitrary", "arbitrary"),
                                             vmem_limit_bytes=VMEM_LIMIT_MIXER),
        name="mixer_prompt",
    )(*[a for a, _ in operands])


def _sample_mixer(x2d, mats, p, l):
    n = x2d.shape[0]
    emit_bf16 = mats is None
    lay = lambda k: (p[k], _layer_resident(p[k], l))
    col_block = lambda k: (0, jnp.minimum(k, N_COLS - 1))
    row_piece = lambda k: (jnp.minimum(k, N_ROW_PIECES - 1), 0)
    if emit_bf16:
        w_in = (p['w_in'], pl.BlockSpec((None, D, D), lambda k: (l,) + col_block(k)))
        sq = lambda name: (p[name], pl.BlockSpec((None, ROW_PIECE, D), lambda k: (l,) + row_piece(k)))
        w_pool = lay('w_pool')
    else:
        w_in = (mats['w_in'], pl.BlockSpec((D, D), col_block))
        sq = lambda name: (mats[name], pl.BlockSpec((ROW_PIECE, D), row_piece))
        w_pool = (mats['w_pool'], _resident(mats['w_pool'].shape))
    history = [jnp.swapaxes(p['state_conv'][l], 0, 1)]
    history += [jnp.swapaxes(p['state_pool'][l, :, POOL_BUF - (w - 1):, g * G_C:(g + 1) * G_C], 0, 1)
                for g, w in enumerate(POOL_WINDOWS)]
    operands = [(x2d, _resident(x2d.shape))] + [(a, _resident(a.shape)) for a in history] + [
                w_in,
                lay('lnv_g'), lay('lnv_b'), lay('w_spatial_d'), lay('b_spatial_d'), sq('w_proj_a'), lay('conv_w'),
                lay('conv_b'), sq('w_proj_b'), w_pool, lay('pool_scale'), sq('w_proj_c'), sq('w_o'),
                lay('ln1_g'), lay('ln1_b')]
    once = lambda shape: pl.BlockSpec(shape, lambda k: (0,) * len(shape), pipeline_mode=pl.Buffered(1))
    out_specs = [once((n, D))] * 4
    out_shape = [jax.ShapeDtypeStruct((n, D), F32)] * 4
    if emit_bf16:
        sq_out = pl.BlockSpec((ROW_PIECE, D), row_piece)
        out_specs += [pl.BlockSpec((D, D), col_block), sq_out, sq_out, once(p['w_pool'].shape[1:]), sq_out, sq_out]
        out_shape += [jax.ShapeDtypeStruct(p[k].shape[1:], BF16) for k in MIXER_MATS]
    outs = pl.pallas_call(
        functools.partial(_sample_mixer_kernel, emit_bf16=emit_bf16),
        grid=(N_COLS + 1,),
        in_specs=[spec for _, spec in operands],
        out_specs=out_specs,
        out_shape=out_shape,
        scratch_shapes=[pltpu.VMEM((n, D), BF16),
                        pltpu.VMEM((N_COLS, n, D), F32)]
                       + [pltpu.VMEM((D, D), BF16)] * 4,
        compiler_params=pltpu.CompilerParams(
            dimension_semantics=("arbitrary",), vmem_limit_bytes=VMEM_LIMIT_MIXER,
            allow_input_fusion=[1 <= i <= len(history) for i in range(len(operands))]),
        name="sample_mixer",
    )(*[a for a, _ in operands])
    return outs[:4], (dict(zip(MIXER_MATS, outs[4:])) if emit_bf16 else mats)


def _ffn(h2d, hs2d, tm, w1, w2, p, l, convert_next):
    n = h2d.shape[0]
    assert h2d.shape == (n, D) and h2d.dtype == F32 and n % tm == 0 and tm % SUB_FFN == 0
    nsteps = n // tm
    assert not convert_next or all((p[k].shape[-2] // nsteps) % (2 * SUBLANES) == 0 for k in MIXER_MATS)
    row_block = pl.BlockSpec((tm, D), lambda i: (i, 0))
    lay = lambda k: (p[k], _layer_resident(p[k], l))
    operands = [(h2d, row_block), (hs2d, _resident(hs2d.shape)), (w1, _resident(w1.shape)), (w2, _resident(w2.shape)),
                lay('ffn_vec')]
    out_specs = [row_block, pl.BlockSpec(hs2d.shape, lambda i: (0, 0))]
    out_shape = [jax.ShapeDtypeStruct((n, D), F32), jax.ShapeDtypeStruct(hs2d.shape, F32)]
    n_cvt = len(MIXER_MATS) if convert_next else 0
    if convert_next:
        for k in MIXER_MATS:
            a = p[k]
            r = a.shape[-2] // nsteps
            lead = a.ndim - 3
            blk = a.shape[1:-2] + (r, a.shape[-1])
            operands.append((a, pl.BlockSpec((None,) + blk, lambda i, lead=lead: (l + 1,) + (0,) * lead + (i, 0))))
            out_specs.append(pl.BlockSpec(blk, lambda i, lead=lead: (0,) * lead + (i, 0)))
            out_shape.append(jax.ShapeDtypeStruct(a.shape[1:], BF16))
    outs = pl.pallas_call(
        functools.partial(_ffn_kernel, sub=min(tm, SUB_FFN), n_cvt=n_cvt),
        grid=(nsteps,),
        in_specs=[spec for _, spec in operands],
        out_specs=out_specs,
        out_shape=out_shape,
        scratch_shapes=[pltpu.VMEM((tm, D), BF16)],
        compiler_params=pltpu.CompilerParams(dimension_semantics=("arbitrary",),
                                             vmem_limit_bytes=VMEM_LIMIT_FFN),
        name="ffn",
    )(*[a for a, _ in operands])
    return outs[0], outs[1], (dict(zip(MIXER_MATS, outs[2:])) if convert_next else None)


def kernel(x_prompt, x_sample, state_conv, state_pool, w_in, lnv_g, lnv_b, w_spatial, b_spatial, w_proj_a, conv_w, conv_b, w_proj_b, w_pool, pool_scale, w_proj_c, w_o, ln1_g, ln1_b, w_ff1, w_ff2, ln2_g, ln2_b):
    bp, seq, _ = x_prompt.shape
    bs = x_sample.shape[0]
    assert x_prompt.shape == (bp, seq, D) and x_sample.shape == (bs, 1, D) and w_in.shape == (DEPTH, D, N_COLS * D)
    assert state_conv.shape == (DEPTH, bs, CONV_W - 1, D) and state_pool.shape == (DEPTH, bs, POOL_BUF, D)
    xp = x_prompt.reshape(bp * seq, D)
    xs = x_sample.reshape(bs, D)
    conv_p, pool_p, conv_s, pool_s, chunk_v_s = [], [], [], [], []
    rows = lambda a: a[:, None, :]
    p = {
        'w_in': w_in, 'lnv_g': rows(lnv_g), 'lnv_b': rows(lnv_b),
        'w_spatial': w_spatial, 'b_spatial_t': jnp.swapaxes(b_spatial, 1, 2),
        'w_spatial_d': rows(jnp.repeat(w_spatial[:, :, 0, 0], G_A, axis=1)),
        'b_spatial_d': rows(jnp.repeat(b_spatial[:, :, 0], G_A, axis=1)),
        'w_proj_a': w_proj_a, 'conv_w': conv_w, 'conv_b': rows(conv_b),
        'w_proj_b': w_proj_b, 'w_pool': w_pool, 'pool_scale': rows(pool_scale),
        'w_proj_c': w_proj_c, 'w_o': w_o, 'ln1_g': rows(ln1_g), 'ln1_b': rows(ln1_b),
        'w_ff1': w_ff1, 'w_ff2': w_ff2, 'ln2_g': rows(ln2_g), 'ln2_b': rows(ln2_b),
        'state_conv': state_conv, 'state_pool': state_pool,
    }
    pad_rows = lambda a: jnp.pad(a, ((0, 0), (0, VEC_ROWS - a.shape[1]), (0, 0)))
    p['mixer_vec'] = pad_rows(jnp.stack([lnv_g, lnv_b, conv_b, pool_scale, ln1_g, ln1_b] + [conv_w[:, k] for k in range(CONV_W)], axis=1))
    p['ffn_vec'] = pad_rows(jnp.stack([ln2_g, ln2_b], axis=1))
    mats = None
    for l in range(DEPTH):
        (hs, z_new, xc_new, vn), mats = _sample_mixer(xs, mats, p, l)
        conv_s.append(jnp.concatenate([state_conv[l][:, 1:], z_new[:, None, :]], axis=1))
        pool_s.append(jnp.concatenate([state_pool[l][:, 1:], xc_new[:, None, :]], axis=1))
        chunk_v_s.append(vn[:, None, :])
        hp, nconv, npool, w1, w2 = _mixer_prompt(xp, bp, seq, TM_MIXER, mats, p, l)
        conv_p.append(nconv)
        pool_p.append(npool)
        xp, xs, mats = _ffn(hp, hs, TM_FFN, w1, w2, p, l, convert_next=l + 1 < DEPTH)
    return (xp.reshape(bp, seq, D), xs.reshape(bs, 1, D), jnp.stack(conv_p), jnp.stack(pool_p),
            jnp.stack(conv_s), jnp.stack(pool_s), jnp.stack(chunk_v_s))
```

```python
import functools

import jax
import jax.numpy as jnp
from jax import lax
from jax.experimental import pallas as pl
from jax.experimental.pallas import tpu as pltpu

D = 1024
N_GROUPS_A = 4
CHUNK = 128
G_A = D // N_GROUPS_A
CONV_W = 3
POOL_WINDOWS = (2, 4, 8, 16)
G_C = D // len(POOL_WINDOWS)
POOL_BUF = max(POOL_WINDOWS) - 1
D_FF = 4 * D
DEPTH = 2
ALPHA = float((2 * DEPTH) ** 0.25)
LN_EPS = 1e-5
PAST_LEN = 16384

COL_U, COL_V, COL_BG, COL_CG, COL_XB, COL_XC, COL_GATE = 0, 1, 2, 3, 4, 5, 6
N_COLS = 9
N_ROW_PIECES = 8
ROW_PIECE = D // N_ROW_PIECES

SUBLANES = 8
LANES = 128
N_LT = D // LANES
LT_G = G_C // LANES
Z_HEAD = SUBLANES
P_HEAD = 32
P_LO = 16
VEC_CONV_W = 6
VEC_ROWS = 16
P_CARRY = 16
assert CONV_W - 1 <= Z_HEAD and POOL_BUF <= P_CARRY <= P_HEAD - P_LO and P_CARRY % SUBLANES == 0
assert PAST_LEN % CHUNK == 0 and PAST_LEN >= max(POOL_WINDOWS)

TM_MIXER = 512
TM_FFN = 1024
SUB_MIXER = 256
SUB_FFN = 256
VMEM_LIMIT_MIXER = 56 * 1024 * 1024
VMEM_LIMIT_FFN = 48 * 1024 * 1024

F32 = jnp.float32
BF16 = jnp.bfloat16


def _dot(a, b):
    return jnp.dot(a, b, preferred_element_type=F32)


def _layer_norm(x, g, b):
    mu = jnp.mean(x, axis=-1, keepdims=True)
    xc = x - mu
    var = jnp.mean(xc * xc, axis=-1, keepdims=True)
    return xc * lax.rsqrt(var + LN_EPS) * g + b


def _mixer_prompt_kernel(x_ref, win_ref, vec_ref, wsp_ref, bsp_ref, wpa_ref,
                         wpb_ref, wpool_ref, wpc_ref, wo_ref, wf1_ref, wf2_ref,
                         h_ref, nconv_ref, npool_ref, wf1o_ref, wf2o_ref,
                         xb_s, vn_s, u_s, ha_s, hb_s, hc_s, acc_s, z_s, p_s, la_s, lb_s, lc_s, *, tm, sub):
    t = pl.program_id(1)

    @pl.when(t == 0)
    def _():
        z_s[:, 0:Z_HEAD, :] = jnp.zeros((N_LT, Z_HEAD, LANES), F32)
        p_s[:, 0:P_HEAD, :] = jnp.zeros((N_LT, P_HEAD, LANES), F32)
        la_s[:, 0:P_LO, :] = jnp.zeros((N_LT, P_LO, LANES), F32)
        lb_s[:, 0:P_LO, :] = jnp.zeros((N_LT - LT_G, P_LO, LANES), F32)
        lc_s[:, 0:P_LO, :] = jnp.zeros((N_LT - 2 * LT_G, P_LO, LANES), F32)

    row = lax.broadcasted_iota(jnp.int32, (CHUNK, CHUNK), 0)
    col = lax.broadcasted_iota(jnp.int32, (CHUNK, CHUNK), 1)
    causal = col <= row
    vec = lambda r, cols=slice(None): vec_ref[r:r + 1, cols]
    ROW_LNV_G, ROW_LNV_B, ROW_CONV_B, ROW_PSC, ROW_LN1_G, ROW_LN1_B = range(VEC_CONV_W)

    for r0 in range(0, tm, sub):
        rs = slice(r0, r0 + sub)
        xb_s[rs, :] = x_ref[rs, :].astype(BF16)

        def proj(k):
            return _dot(xb_s[rs, :], win_ref[:, k * D:(k + 1) * D])

        xc = proj(COL_XC)
        p0, p1 = P_HEAD + r0, P_HEAD + r0 + sub
        for j in range(N_LT):
            p_s[j, p0:p1, :] = xc[:, j * LANES:(j + 1) * LANES]
        lo = P_LO if r0 == 0 else p0
        for j in range(N_LT):
            la_s[j, lo:p1, :] = p_s[j, lo:p1, :] + p_s[j, lo - 1:p1 - 1, :]
        for j in range(N_LT - LT_G):
            lb_s[j, lo:p1, :] = la_s[j + LT_G, lo:p1, :] + la_s[j + LT_G, lo - 2:p1 - 2, :]
        for j in range(N_LT - 2 * LT_G):
            lc_s[j, lo:p1, :] = lb_s[j + LT_G, lo:p1, :] + lb_s[j + LT_G, lo - 4:p1 - 4, :]
        sum16 = [lc_s[j + LT_G, p0:p1, :] + lc_s[j + LT_G, p0 - 8:p1 - 8, :] for j in range(LT_G)]
        sums = tuple(jnp.concatenate(tiles, axis=-1) for tiles in (
            [la_s[j, p0:p1, :] for j in range(LT_G)], [lb_s[j, p0:p1, :] for j in range(LT_G)],
            [lc_s[j, p0:p1, :] for j in range(LT_G)], sum16))
        pos1 = t * tm + r0 + lax.broadcasted_iota(jnp.int32, (sub, 1), 0) + 1
        for g, w in enumerate(POOL_WINDOWS):
            cs = slice(g * G_C, (g + 1) * G_C)
            inv_cnt = 1.0 / jnp.minimum(pos1, w).astype(F32)
            d = sums[g] * inv_cnt - xc[:, cs]
            hc_s[rs, cs] = (_dot(d.astype(BF16), wpool_ref[g]) * vec(ROW_PSC, cs)).astype(BF16)
        acc_s[rs, :] = jax.nn.sigmoid(proj(COL_GATE + 2)) * _dot(hc_s[rs, :], wpc_ref[...])

        vn_s[rs, :] = _layer_norm(proj(COL_V), vec(ROW_LNV_G), vec(ROW_LNV_B)).astype(BF16)
        u_s[rs, :] = proj(COL_U)
        for g in range(N_GROUPS_A):
            w_g = jnp.where(causal, wsp_ref[g], 0.0).astype(BF16)
            bias = bsp_ref[:, g:g + 1]
            cs = slice(g * G_A, (g + 1) * G_A)
            for c0 in range(r0, r0 + sub, CHUNK):
                ch = slice(c0, c0 + CHUNK)
                s = _dot(w_g, vn_s[ch, cs]) + bias
                ha_s[ch, cs] = (u_s[ch, cs] * s).astype(BF16)
        acc_s[rs, :] += jax.nn.sigmoid(proj(COL_GATE + 0)) * _dot(ha_s[rs, :], wpa_ref[...])

        z = proj(COL_CG) * proj(COL_XB)
        z0 = Z_HEAD + r0
        for j in range(N_LT):
            z_s[j, z0:z0 + sub, :] = z[:, j * LANES:(j + 1) * LANES]
        z_m2 = jnp.concatenate([z_s[j, z0 - 2:z0 - 2 + sub, :] for j in range(N_LT)], axis=-1)
        z_m1 = jnp.concatenate([z_s[j, z0 - 1:z0 - 1 + sub, :] for j in range(N_LT)], axis=-1)
        y = vec(ROW_CONV_B) + (vec(VEC_CONV_W) * z_m2 + vec(VEC_CONV_W + 1) * z_m1 + vec(VEC_CONV_W + 2) * z)
        hb_s[rs, :] = (proj(COL_BG) * y).astype(BF16)
        acc_s[rs, :] += jax.nn.sigmoid(proj(COL_GATE + 1)) * _dot(hb_s[rs, :], wpb_ref[...])

        o = _dot(acc_s[rs, :].astype(BF16), wo_ref[...])
        h_ref[rs, :] = _layer_norm(ALPHA * x_ref[rs, :] + o, vec(ROW_LN1_G), vec(ROW_LN1_B))

    tail = z_s[:, tm:tm + Z_HEAD, :]
    z_s[:, 0:Z_HEAD, :] = tail
    hist = p_s[:, tm + P_HEAD - P_CARRY:tm + P_HEAD, :]
    p_s[:, P_HEAD - P_CARRY:P_HEAD, :] = hist
    for j in range(N_LT):
        nconv_ref[:, j * LANES:(j + 1) * LANES] = z_s[j, tm + Z_HEAD - (CONV_W - 1):tm + Z_HEAD, :]
        npool_ref[:, j * LANES:(j + 1) * LANES] = p_s[j, tm + P_HEAD - POOL_BUF:tm + P_HEAD, :]

    wf1o_ref[...] = wf1_ref[...].astype(BF16)
    wf2o_ref[...] = wf2_ref[...].astype(BF16)


def _sample_mixer_kernel(x_ref, sconv_ref, sp0_ref, sp1_ref, sp2_ref, sp3_ref, win_ref, lnvg_ref, lnvb_ref, wsd_ref,
                         bsd_ref, wpa_ref, cw_ref, cb_ref, wpb_ref, wpool_ref, psc_ref, wpc_ref, wo_ref, l1g_ref,
                         l1b_ref, h_ref, z_ref, xc_ref, vn_ref, *rest, emit_bf16):
    if emit_bf16:
        wino_ref, wpao_ref, wpbo_ref, wpoolo_ref, wpco_ref, woo_ref = rest[:len(MIXER_MATS)]
    else:
        wino_ref = wpao_ref = wpbo_ref = wpoolo_ref = wpco_ref = woo_ref = None
    xb_s, proj_s, wpa_s, wpb_s, wpc_s, wo_s = rest[len(MIXER_MATS) if emit_bf16 else 0:]
    k = pl.program_id(0)

    @pl.when(k == 0)
    def _():
        xb_s[...] = x_ref[...].astype(BF16)

    @pl.when(k < N_COLS)
    def _():
        w = win_ref[...].astype(BF16)
        if emit_bf16:
            wino_ref[...] = w
        proj_s[k] = _dot(xb_s[...], w)

    @pl.when(k < N_ROW_PIECES)
    def _():
        rows = pl.ds(pl.multiple_of(k * ROW_PIECE, ROW_PIECE), ROW_PIECE)
        for src, dst, keep in ((wpa_ref, wpao_ref, wpa_s), (wpb_ref, wpbo_ref, wpb_s), (wpc_ref, wpco_ref, wpc_s),
                               (wo_ref, woo_ref, wo_s)):
            piece = src[...].astype(BF16)
            if emit_bf16:
                dst[...] = piece
            keep[rows, :] = piece

    @pl.when(k == N_COLS)
    def _():
        x = x_ref[...]
        wpa, wpb, wpc, wo = wpa_s[...], wpb_s[...], wpc_s[...], wo_s[...]
        wpool = wpool_ref[...].astype(BF16)
        if emit_bf16:
            wpoolo_ref[...] = wpool

        vn = _layer_norm(proj_s[COL_V], lnvg_ref[...], lnvb_ref[...])
        vn_ref[...] = vn
        s = wsd_ref[...] * vn + bsd_ref[...]
        ha = (proj_s[COL_U] * s).astype(BF16)
        acc = jax.nn.sigmoid(proj_s[COL_GATE + 0]) * _dot(ha, wpa)

        z = proj_s[COL_CG] * proj_s[COL_XB]
        z_ref[...] = z
        cw = cw_ref[...]
        y = cb_ref[...] + (cw[0:1] * sconv_ref[0] + cw[1:2] * sconv_ref[1] + cw[2:3] * z)
        hb = (proj_s[COL_BG] * y).astype(BF16)
        acc = acc + jax.nn.sigmoid(proj_s[COL_GATE + 1]) * _dot(hb, wpb)

        xc = proj_s[COL_XC]
        xc_ref[...] = xc
        psc = psc_ref[...]
        hc = []
        for g, (w, sp_ref) in enumerate(zip(POOL_WINDOWS, (sp0_ref, sp1_ref, sp2_ref, sp3_ref))):
            cs = slice(g * G_C, (g + 1) * G_C)
            tot = xc[:, cs]
            for j in range(w - 1):
                tot = tot + sp_ref[w - 2 - j]
            d = tot * (1.0 / w) - xc[:, cs]
            hc.append((_dot(d.astype(BF16), wpool[g]) * psc[:, cs]).astype(BF16))
        hc = jnp.concatenate(hc, axis=-1)
        acc = acc + jax.nn.sigmoid(proj_s[COL_GATE + 2]) * _dot(hc, wpc)

        o = _dot(acc.astype(BF16), wo)
        h_ref[...] = _layer_norm(ALPHA * x + o, l1g_ref[...], l1b_ref[...])


def _ffn_rows(load_h, load_hb, w1_ref, w2_ref, gb_ref):
    acc = None
    for j in range(D_FF // D):
        f = _dot(load_hb(), w1_ref[:, j * D:(j + 1) * D])
        a = jnp.square(jnp.maximum(f, 0.0)).astype(BF16)
        c = _dot(a, w2_ref[j * D:(j + 1) * D, :])
        acc = c if acc is None else acc + c
    return _layer_norm(ALPHA * load_h() + acc, gb_ref[0:1, :], gb_ref[1:2, :])


def _ffn_kernel(*refs, sub, n_cvt):
    h_ref, hs_ref, w1_ref, w2_ref, gb_ref = refs[:5]
    cvt_in = refs[5:5 + n_cvt]
    o_ref, os_ref = refs[5 + n_cvt:7 + n_cvt]
    cvt_out = refs[7 + n_cvt:7 + 2 * n_cvt]
    (hb_s,) = refs[7 + 2 * n_cvt:]

    for r0 in range(0, h_ref.shape[0], sub):
        rs = slice(r0, r0 + sub)
        hb_s[rs, :] = h_ref[rs, :].astype(BF16)
        o_ref[rs, :] = _ffn_rows(lambda rs=rs: h_ref[rs, :], lambda rs=rs: hb_s[rs, :], w1_ref, w2_ref, gb_ref)

    @pl.when(pl.program_id(0) == pl.num_programs(0) - 1)
    def _():
        os_ref[...] = _ffn_rows(lambda: hs_ref[...], lambda: hs_ref[...].astype(BF16), w1_ref, w2_ref, gb_ref)

    for src, dst in zip(cvt_in, cvt_out):
        dst[...] = src[...].astype(BF16)


def _resident(shape):
    nd = len(shape)
    return pl.BlockSpec(shape, lambda *_: (0,) * nd, pipeline_mode=pl.Buffered(1))


def _layer_resident(a, l):
    return pl.BlockSpec((None,) + a.shape[1:], lambda *_: (l,) + (0,) * (a.ndim - 1), pipeline_mode=pl.Buffered(1))


MIXER_MATS = ('w_in', 'w_proj_a', 'w_proj_b', 'w_pool', 'w_proj_c', 'w_o')


def _mixer_prompt(x2d, batch, seq, tm, mats, p, l):
    assert x2d.shape == (batch * seq, D) and x2d.dtype == F32 and seq % tm == 0 and tm % SUB_MIXER == 0
    assert SUB_MIXER % CHUNK == 0 and SUB_MIXER >= P_CARRY
    nt = seq // tm
    nsteps = batch * nt
    assert D % nsteps == 0 and (D // nsteps) % (2 * SUBLANES) == 0
    row_block = pl.BlockSpec((tm, D), lambda b, t: (b * nt + t, 0))
    r1, r2 = D // nsteps, D_FF // nsteps
    lay = lambda k: (p[k], _layer_resident(p[k], l))
    mat = lambda k: (mats[k], _resident(mats[k].shape))
    operands = [(x2d, row_block), mat('w_in'), lay('mixer_vec'), lay('w_spatial'), lay('b_spatial_t'),
                mat('w_proj_a'), mat('w_proj_b'), mat('w_pool'), mat('w_proj_c'), mat('w_o'),
                (p['w_ff1'], pl.BlockSpec((None, r1, D_FF), lambda b, t: (l, b * nt + t, 0))),
                (p['w_ff2'], pl.BlockSpec((None, r2, D), lambda b, t: (l, b * nt + t, 0)))]
    return pl.pallas_call(
        functools.partial(_mixer_prompt_kernel, tm=tm, sub=min(tm, SUB_MIXER)),
        grid=(batch, nt),
        in_specs=[spec for _, spec in operands],
        out_specs=[row_block,
                   pl.BlockSpec((None, CONV_W - 1, D), lambda b, t: (b, 0, 0)),
                   pl.BlockSpec((None, POOL_BUF, D), lambda b, t: (b, 0, 0)),
                   pl.BlockSpec((r1, D_FF), lambda b, t: (b * nt + t, 0)),
                   pl.BlockSpec((r2, D), lambda b, t: (b * nt + t, 0))],
        out_shape=[jax.ShapeDtypeStruct((batch * seq, D), F32),
                   jax.ShapeDtypeStruct((batch, CONV_W - 1, D), F32),
                   jax.ShapeDtypeStruct((batch, POOL_BUF, D), F32),
                   jax.ShapeDtypeStruct((D, D_FF), BF16),
                   jax.ShapeDtypeStruct((D_FF, D), BF16)],
        scratch_shapes=[pltpu.VMEM((tm, D), BF16),
                        pltpu.VMEM((tm, D), BF16),
                        pltpu.VMEM((tm, D), F32),
                        pltpu.VMEM((tm, D), BF16),
                        pltpu.VMEM((tm, D), BF16),
                        pltpu.VMEM((tm, D), BF16),
                        pltpu.VMEM((tm, D), F32),
                        pltpu.VMEM((N_LT, Z_HEAD + tm, LANES), F32),
                        pltpu.VMEM((N_LT, P_HEAD + tm, LANES), F32),
                        pltpu.VMEM((N_LT, P_HEAD + tm, LANES), F32),
                        pltpu.VMEM((N_LT - LT_G, P_HEAD + tm, LANES), F32),
                        pltpu.VMEM((N_LT - 2 * LT_G, P_HEAD + tm, LANES), F32)],
        compiler_params=pltpu.CompilerParams(dimension_semantics=("arbitrary", "arbitrary"),
                                             vmem_limit_bytes=VMEM_LIMIT_MIXER),
        name="mixer_prompt",
    )(*[a for a, _ in operands])


def _sample_mixer(x2d, mats, p, l):
    n = x2d.shape[0]
    emit_bf16 = mats is None
    lay = lambda k: (p[k], _layer_resident(p[k], l))
    col_block = lambda k: (0, jnp.minimum(k, N_COLS - 1))
    row_piece = lambda k: (jnp.minimum(k, N_ROW_PIECES - 1), 0)
    if emit_bf16:
        w_in = (p['w_in'], pl.BlockSpec((None, D, D), lambda k: (l,) + col_block(k)))
        sq = lambda name: (p[name], pl.BlockSpec((None, ROW_PIECE, D), lambda k: (l,) + row_piece(k)))
        w_pool = lay('w_pool')
    else:
        w_in = (mats['w_in'], pl.BlockSpec((D, D), col_block))
        sq = lambda name: (mats[name], pl.BlockSpec((ROW_PIECE, D), row_piece))
        w_pool = (mats['w_pool'], _resident(mats['w_pool'].shape))
    history = [jnp.swapaxes(p['state_conv'][l], 0, 1)]
    history += [jnp.swapaxes(p['state_pool'][l, :, POOL_BUF - (w - 1):, g * G_C:(g + 1) * G_C], 0, 1)
                for g, w in enumerate(POOL_WINDOWS)]
    operands = [(x2d, _resident(x2d.shape))] + [(a, _resident(a.shape)) for a in history] + [
                w_in,
                lay('lnv_g'), lay('lnv_b'), lay('w_spatial_d'), lay('b_spatial_d'), sq('w_proj_a'), lay('conv_w'),
                lay('conv_b'), sq('w_proj_b'), w_pool, lay('pool_scale'), sq('w_proj_c'), sq('w_o'),
                lay('ln1_g'), lay('ln1_b')]
    once = lambda shape: pl.BlockSpec(shape, lambda k: (0,) * len(shape), pipeline_mode=pl.Buffered(1))
    out_specs = [once((n, D))] * 4
    out_shape = [jax.ShapeDtypeStruct((n, D), F32)] * 4
    if emit_bf16:
        sq_out = pl.BlockSpec((ROW_PIECE, D), row_piece)
        out_specs += [pl.BlockSpec((D, D), col_block), sq_out, sq_out, once(p['w_pool'].shape[1:]), sq_out, sq_out]
        out_shape += [jax.ShapeDtypeStruct(p[k].shape[1:], BF16) for k in MIXER_MATS]
    outs = pl.pallas_call(
        functools.partial(_sample_mixer_kernel, emit_bf16=emit_bf16),
        grid=(N_COLS + 1,),
        in_specs=[spec for _, spec in operands],
        out_specs=out_specs,
        out_shape=out_shape,
        scratch_shapes=[pltpu.VMEM((n, D), BF16),
                        pltpu.VMEM((N_COLS, n, D), F32)]
                       + [pltpu.VMEM((D, D), BF16)] * 4,
        compiler_params=pltpu.CompilerParams(
            dimension_semantics=("arbitrary",), vmem_limit_bytes=VMEM_LIMIT_MIXER,
            allow_input_fusion=[1 <= i <= len(history) for i in range(len(operands))]),
        name="sample_mixer",
    )(*[a for a, _ in operands])
    return outs[:4], (dict(zip(MIXER_MATS, outs[4:])) if emit_bf16 else mats)


def _ffn(h2d, hs2d, tm, w1, w2, p, l, convert_next):
    n = h2d.shape[0]
    assert h2d.shape == (n, D) and h2d.dtype == F32 and n % tm == 0 and tm % SUB_FFN == 0
    nsteps = n // tm
    assert not convert_next or all((p[k].shape[-2] // nsteps) % (2 * SUBLANES) == 0 for k in MIXER_MATS)
    row_block = pl.BlockSpec((tm, D), lambda i: (i, 0))
    lay = lambda k: (p[k], _layer_resident(p[k], l))
    operands = [(h2d, row_block), (hs2d, _resident(hs2d.shape)), (w1, _resident(w1.shape)), (w2, _resident(w2.shape)),
                lay('ffn_vec')]
    out_specs = [row_block, pl.BlockSpec(hs2d.shape, lambda i: (0, 0))]
    out_shape = [jax.ShapeDtypeStruct((n, D), F32), jax.ShapeDtypeStruct(hs2d.shape, F32)]
    n_cvt = len(MIXER_MATS) if convert_next else 0
    if convert_next:
        for k in MIXER_MATS:
            a = p[k]
            r = a.shape[-2] // nsteps
            lead = a.ndim - 3
            blk = a.shape[1:-2] + (r, a.shape[-1])
            operands.append((a, pl.BlockSpec((None,) + blk, lambda i, lead=lead: (l + 1,) + (0,) * lead + (i, 0))))
            out_specs.append(pl.BlockSpec(blk, lambda i, lead=lead: (0,) * lead + (i, 0)))
            out_shape.append(jax.ShapeDtypeStruct(a.shape[1:], BF16))
    outs = pl.pallas_call(
        functools.partial(_ffn_kernel, sub=min(tm, SUB_FFN), n_cvt=n_cvt),
        grid=(nsteps,),
        in_specs=[spec for _, spec in operands],
        out_specs=out_specs,
        out_shape=out_shape,
        scratch_shapes=[pltpu.VMEM((tm, D), BF16)],
        compiler_params=pltpu.CompilerParams(dimension_semantics=("arbitrary",),
                                             vmem_limit_bytes=VMEM_LIMIT_FFN),
        name="ffn",
    )(*[a for a, _ in operands])
    return outs[0], outs[1], (dict(zip(MIXER_MATS, outs[2:])) if convert_next else None)


def kernel(x_prompt, x_sample, state_conv, state_pool, w_in, lnv_g, lnv_b, w_spatial, b_spatial, w_proj_a, conv_w, conv_b, w_proj_b, w_pool, pool_scale, w_proj_c, w_o, ln1_g, ln1_b, w_ff1, w_ff2, ln2_g, ln2_b):
    bp, seq, _ = x_prompt.shape
    bs = x_sample.shape[0]
    assert x_prompt.shape == (bp, seq, D) and x_sample.shape == (bs, 1, D) and w_in.shape == (DEPTH, D, N_COLS * D)
    assert state_conv.shape == (DEPTH, bs, CONV_W - 1, D) and state_pool.shape == (DEPTH, bs, POOL_BUF, D)
    xp = x_prompt.reshape(bp * seq, D)
    xs = x_sample.reshape(bs, D)
    conv_p, pool_p, conv_s, pool_s, chunk_v_s = [], [], [], [], []
    rows = lambda a: a[:, None, :]
    p = {
        'w_in': w_in, 'lnv_g': rows(lnv_g), 'lnv_b': rows(lnv_b),
        'w_spatial': w_spatial, 'b_spatial_t': jnp.swapaxes(b_spatial, 1, 2),
        'w_spatial_d': rows(jnp.repeat(w_spatial[:, :, 0, 0], G_A, axis=1)),
        'b_spatial_d': rows(jnp.repeat(b_spatial[:, :, 0], G_A, axis=1)),
        'w_proj_a': w_proj_a, 'conv_w': conv_w, 'conv_b': rows(conv_b),
        'w_proj_b': w_proj_b, 'w_pool': w_pool, 'pool_scale': rows(pool_scale),
        'w_proj_c': w_proj_c, 'w_o': w_o, 'ln1_g': rows(ln1_g), 'ln1_b': rows(ln1_b),
        'w_ff1': w_ff1, 'w_ff2': w_ff2, 'ln2_g': rows(ln2_g), 'ln2_b': rows(ln2_b),
        'state_conv': state_conv, 'state_pool': state_pool,
    }
    pad_rows = lambda a: jnp.pad(a, ((0, 0), (0, VEC_ROWS - a.shape[1]), (0, 0)))
    p['mixer_vec'] = pad_rows(jnp.stack([lnv_g, lnv_b, conv_b, pool_scale, ln1_g, ln1_b] + [conv_w[:, k] for k in range(CONV_W)], axis=1))
    p['ffn_vec'] = pad_rows(jnp.stack([ln2_g, ln2_b], axis=1))
    mats = None
    for l in range(DEPTH):
        (hs, z_new, xc_new, vn), mats = _sample_mixer(xs, mats, p, l)
        conv_s.append(jnp.concatenate([state_conv[l][:, 1:], z_new[:, None, :]], axis=1))
        pool_s.append(jnp.concatenate([state_pool[l][:, 1:], xc_new[:, None, :]], axis=1))
        chunk_v_s.append(vn[:, None, :])
        hp, nconv, npool, w1, w2 = _mixer_prompt(xp, bp, seq, TM_MIXER, mats, p, l)
        conv_p.append(nconv)
        pool_p.append(npool)
        xp, xs, mats = _ffn(hp, hs, TM_FFN, w1, w2, p, l, convert_next=l + 1 < DEPTH)
    return (xp.reshape(bp, seq, D), xs.reshape(bs, 1, D), jnp.stack(conv_p), jnp.stack(pool_p),
            jnp.stack(conv_s), jnp.stack(pool_s), jnp.stack(chunk_v_s))
```
